```python
import jax, jax.numpy as jnp
from jax import lax
import numpy as np

D_MODEL = 2048
BATCH = 4
SEQ = 2048
DEPTH = 1

HEAD_DIM = 128
N_HEADS_FOX = 8
N_HEADS_MOBA = 8
WIDTH_FOX = N_HEADS_FOX * HEAD_DIM
WIDTH_MOBA = N_HEADS_MOBA * HEAD_DIM
FOX_Q_BLOCK = 128
MOBA_BLOCK = 256
MOBA_TOPK = 3
MOBA_Q_CHUNK = 16
D_FF = -(-(8 * D_MODEL) // (3 * 256)) * 256
ROPE_THETA = 10000.0
RMS_EPS = 1e-6
IN_COLS = 3 * WIDTH_FOX + 3 * WIDTH_MOBA + N_HEADS_FOX + 2 * D_MODEL

kernel_name = "hybrid_fox_moba_gated_swiglu"


def rms_norm(x, g):
    xf = x.astype(jnp.float32)
    y = xf * lax.rsqrt(jnp.mean(xf * xf, axis=-1, keepdims=True) + RMS_EPS)
    return (y * g.astype(jnp.float32)).astype(x.dtype)


def to_heads(t, n_heads):
    b, s, _ = t.shape
    return t.reshape(b, s, n_heads, HEAD_DIM).transpose(0, 2, 1, 3)


def from_heads(t):
    b, h, s, d = t.shape
    return t.transpose(0, 2, 1, 3).reshape(b, s, h * d)


def rotary(t):
    s = t.shape[2]
    half = HEAD_DIM // 2
    inv_freq = ROPE_THETA ** (-jnp.arange(half, dtype=jnp.float32) / half)
    ang = jnp.arange(s, dtype=jnp.float32)[:, None] * inv_freq[None, :]
    cos, sin = jnp.cos(ang), jnp.sin(ang)
    tf = t.astype(jnp.float32)
    t1, t2 = tf[..., :half], tf[..., half:]
    out = jnp.concatenate([t1 * cos - t2 * sin, t2 * cos + t1 * sin], axis=-1)
    return out.astype(t.dtype)


def forgetting_attention(q, k, v, log_f):
    b, h, s, _ = q.shape
    c = jnp.cumsum(log_f, axis=-1)
    scale = HEAD_DIM ** -0.5
    kpos = jnp.arange(s)

    def block(start):
        qb = lax.dynamic_slice_in_dim(q, start, FOX_Q_BLOCK, axis=2)
        cb = lax.dynamic_slice_in_dim(c, start, FOX_Q_BLOCK, axis=2)
        qpos = start + jnp.arange(FOX_Q_BLOCK)
        logits = jnp.einsum('bhqd,bhkd->bhqk', qb, k,
                            preferred_element_type=jnp.float32) * scale
        logits = logits + (cb[..., :, None] - c[..., None, :])
        logits = jnp.where(kpos[None, :] <= qpos[:, None], logits, -jnp.inf)
        p = jax.nn.softmax(logits, axis=-1)
        return jnp.einsum('bhqk,bhkd->bhqd', p.astype(v.dtype), v)

    starts = jnp.arange(0, s, FOX_Q_BLOCK)
    out = lax.map(block, starts)
    return out.transpose(1, 2, 0, 3, 4).reshape(b, h, s, HEAD_DIM)


def moba_attention(q, k, v):
    b, h, s, _ = q.shape
    n_blocks = -(-s // MOBA_BLOCK)
    s_pad = n_blocks * MOBA_BLOCK
    pad = ((0, 0), (0, 0), (0, s_pad - s), (0, 0))
    kp = jnp.pad(k, pad)
    vp = jnp.pad(v, pad)
    kb = kp.reshape(b, h, n_blocks, MOBA_BLOCK, HEAD_DIM)
    vb = vp.reshape(b, h, n_blocks, MOBA_BLOCK, HEAD_DIM)
    k_mean = jnp.mean(kb.astype(jnp.float32), axis=3)
    topk = min(MOBA_TOPK, n_blocks)
    scale = HEAD_DIM ** -0.5
    block_ids = jnp.arange(n_blocks)
    gather = jax.vmap(jax.vmap(lambda tbl, ids: tbl[ids]))

    def chunk(start):
        qc = lax.dynamic_slice_in_dim(q, start, MOBA_Q_CHUNK, axis=2)
        blk = start // MOBA_BLOCK
        qpos = start + jnp.arange(MOBA_Q_CHUNK)
        gate = jnp.einsum('bhqd,bhnd->bhqn', qc.astype(jnp.float32), k_mean)
        gate = jnp.where(block_ids < blk, gate, -jnp.inf)
        _, idx = lax.top_k(gate, topk)
        valid = idx < blk
        k_sel = gather(kb, idx)
        v_sel = gather(vb, idx)
        s_sel = jnp.einsum('bhqd,bhqnkd->bhqnk', qc, k_sel,
                           preferred_element_type=jnp.float32) * scale
        s_sel = jnp.where(valid[..., None], s_sel, -jnp.inf)
        s_sel = s_sel.reshape(b, h, MOBA_Q_CHUNK, topk * MOBA_BLOCK)
        k_own = lax.dynamic_slice_in_dim(kp, blk * MOBA_BLOCK, MOBA_BLOCK, axis=2)
        v_own = lax.dynamic_slice_in_dim(vp, blk * MOBA_BLOCK, MOBA_BLOCK, axis=2)
        kpos = blk * MOBA_BLOCK + jnp.arange(MOBA_BLOCK)
        s_own = jnp.einsum('bhqd,bhkd->bhqk', qc, k_own,
                           preferred_element_type=jnp.float32) * scale
        s_own = jnp.where(kpos[None, :] <= qpos[:, None], s_own, -jnp.inf)
        p = jax.nn.softmax(jnp.concatenate([s_sel, s_own], axis=-1), axis=-1)
        p_sel = p[..., :topk * MOBA_BLOCK].reshape(b, h, MOBA_Q_CHUNK, topk, MOBA_BLOCK)
        p_own = p[..., topk * MOBA_BLOCK:]
        out = (jnp.einsum('bhqnk,bhqnkd->bhqd', p_sel.astype(v.dtype), v_sel)
               + jnp.einsum('bhqk,bhkd->bhqd', p_own.astype(v.dtype), v_own))
        return out

    starts = jnp.arange(0, s, MOBA_Q_CHUNK)
    out = lax.map(chunk, starts)
    return out.transpose(1, 2, 0, 3, 4).reshape(b, h, s, HEAD_DIM)


def split_offsets():
    sizes = [WIDTH_FOX] * 3 + [WIDTH_MOBA] * 3 + [N_HEADS_FOX, D_MODEL]
    offs, acc = [], 0
    for sz in sizes:
        acc += sz
        offs.append(acc)
    return offs


def setup_inputs(seed: int = 0) -> dict:
    key = jax.random.key(seed)
    ks = jax.random.split(key, 12)
    f32 = jnp.float32

    def nrm(k, shape, fan_in):
        return jax.random.normal(k, shape, f32) * (fan_in ** -0.5)

    x = jax.random.normal(ks[0], (BATCH, SEQ, D_MODEL), f32)
    norm_mix = 1.0 + 0.02 * jax.random.normal(ks[1], (DEPTH, D_MODEL), f32)
    w_in = nrm(ks[2], (DEPTH, D_MODEL, IN_COLS), D_MODEL)
    b_forget = (jnp.linspace(1.0, 5.0, N_HEADS_FOX, dtype=f32)[None, :]
                + 0.1 * jax.random.normal(ks[3], (DEPTH, N_HEADS_FOX), f32))
    w_o_fox = nrm(ks[4], (DEPTH, WIDTH_FOX, D_MODEL), WIDTH_FOX)
    w_o_moba = nrm(ks[5], (DEPTH, WIDTH_MOBA, D_MODEL), WIDTH_MOBA)
    w_out = nrm(ks[6], (DEPTH, D_MODEL, D_MODEL), D_MODEL)
    norm_ffn = 1.0 + 0.02 * jax.random.normal(ks[7], (DEPTH, D_MODEL), f32)
    w_gate_up = nrm(ks[8], (DEPTH, D_MODEL, 2 * D_FF), D_MODEL)
    w_down = nrm(ks[9], (DEPTH, D_FF, D_MODEL), D_FF)
    norm_final = 1.0 + 0.02 * jax.random.normal(ks[10], (D_MODEL,), f32)
    return {"x": x, "norm_mix": norm_mix, "w_in": w_in, "b_forget": b_forget,
            "w_o_fox": w_o_fox, "w_o_moba": w_o_moba, "w_out": w_out,
            "norm_ffn": norm_ffn, "w_gate_up": w_gate_up, "w_down": w_down,
            "norm_final": norm_final}


def reference(x, norm_mix, w_in, b_forget, w_o_fox, w_o_moba, w_out,
              norm_ffn, w_gate_up, w_down, norm_final):
    offs = split_offsets()
    for layer in range(DEPTH):
        h = rms_norm(x, norm_mix[layer])
        proj = h @ w_in[layer]
        q_a, k_a, v_a, q_b, k_b, v_b, f_logit, g_a, g_b = jnp.split(proj, offs, axis=-1)
        log_f = jax.nn.log_sigmoid(
            (f_logit + b_forget[layer]).astype(jnp.float32)).transpose(0, 2, 1)
        o_a = forgetting_attention(to_heads(q_a, N_HEADS_FOX), to_heads(k_a, N_HEADS_FOX),
                                   to_heads(v_a, N_HEADS_FOX), log_f)
        o_b = moba_attention(rotary(to_heads(q_b, N_HEADS_MOBA)),
                             rotary(to_heads(k_b, N_HEADS_MOBA)),
                             to_heads(v_b, N_HEADS_MOBA))
        y = (jax.nn.sigmoid(g_a) * (from_heads(o_a) @ w_o_fox[layer])
             + jax.nn.sigmoid(g_b) * (from_heads(o_b) @ w_o_moba[layer]))
        x = x + y @ w_out[layer]
        h = rms_norm(x, norm_ffn[layer])
        gate, up = jnp.split(h @ w_gate_up[layer], [D_FF], axis=-1)
        x = x + (jax.nn.silu(gate) * up) @ w_down[layer]
    return rms_norm(x, norm_final)
```

```python
import functools
import math

import jax
import jax.numpy as jnp
from jax import lax
from jax.experimental import pallas as pl
from jax.experimental.pallas import tpu as pltpu

F32 = jnp.float32
BF16 = jnp.bfloat16

HEAD_DIM = 128
N_HEADS = 8
WIDTH = N_HEADS * HEAD_DIM
MOBA_BLOCK = 256
MOBA_TOPK = 3
ROPE_THETA = 10000.0
RMS_EPS = 1e-6
LOG2E = math.log2(math.e)
QK_SCALE = HEAD_DIM ** -0.5
LANES = 128
VMEM_LIMIT = 56 * 1024 * 1024


def _tiles():
    return dict(
        inproj_tm=1024, inproj_tn=1024,
        fox_tq=256, fox_tk=256,
        mix_tm=1024, mix_tn=1024,
        resid_tm=512,
        ffn_tm=512, ffn_tf=512,
        cum_chunk=256,
    )


def _params(sem):
    return pltpu.CompilerParams(dimension_semantics=sem, vmem_limit_bytes=VMEM_LIMIT)


def _rms(x, g):
    ms = jnp.mean(x * x, axis=-1, keepdims=True)
    return x * lax.rsqrt(ms + RMS_EPS) * g


def _dot_nt(a, b):
    return lax.dot_general(a, b, (((1,), (1,)), ((), ())), preferred_element_type=F32)


def _inproj_kernel(x_ref, g_ref, w_ref, wf_ref, cs_ref, sn_ref, o_ref, f_ref, h_scr,
                   *, rot_lo, rot_hi):
    j = pl.program_id(1)

    @pl.when(j == 0)
    def _():
        h = _rms(x_ref[...], g_ref[...]).astype(BF16)
        h_scr[...] = h
        f_ref[...] = jnp.dot(h, wf_ref[...], preferred_element_type=F32)

    acc = jnp.dot(h_scr[...], w_ref[...], preferred_element_type=F32)
    is_rot = jnp.logical_and(j >= rot_lo, j < rot_hi)

    @pl.when(is_rot)
    def _():
        cs = cs_ref[...]
        sn = sn_ref[...]
        for hh in range(acc.shape[1] // HEAD_DIM):
            t = acc[:, hh * HEAD_DIM:(hh + 1) * HEAD_DIM]
            r = pltpu.roll(t, HEAD_DIM // 2, axis=1)
            o_ref[:, hh * HEAD_DIM:(hh + 1) * HEAD_DIM] = (t * cs + r * sn).astype(BF16)

    @pl.when(jnp.logical_not(is_rot))
    def _():
        o_ref[...] = acc.astype(BF16)


def _inproj(x2, g, w_main, w_f, cs, sn, seq):
    t = _tiles()
    tm, tn = t["inproj_tm"], t["inproj_tn"]
    rows, d = x2.shape
    ncols = w_main.shape[1]
    pos_blocks = seq // tm
    kern = functools.partial(_inproj_kernel, rot_lo=(3 * WIDTH) // tn, rot_hi=(5 * WIDTH) // tn)
    return pl.pallas_call(
        kern,
        out_shape=(jax.ShapeDtypeStruct((rows, ncols), BF16),
                   jax.ShapeDtypeStruct((rows, LANES), F32)),
        grid=(rows // tm, ncols // tn),
        in_specs=[
            pl.BlockSpec((tm, d), lambda i, j: (i, 0)),
            pl.BlockSpec((1, d), lambda i, j: (0, 0)),
            pl.BlockSpec((d, tn), lambda i, j: (0, j)),
            pl.BlockSpec((d, LANES), lambda i, j: (0, 0)),
            pl.BlockSpec((tm, HEAD_DIM), lambda i, j: (i % pos_blocks, 0)),
            pl.BlockSpec((tm, HEAD_DIM), lambda i, j: (i % pos_blocks, 0)),
        ],
        out_specs=(pl.BlockSpec((tm, tn), lambda i, j: (i, j)),
                   pl.BlockSpec((tm, LANES), lambda i, j: (i, 0))),
        scratch_shapes=[pltpu.VMEM((tm, d), BF16)],
        compiler_params=_params(("parallel", "arbitrary")),
        name="inproj",
    )(x2, g, w_main, w_f, cs, sn)


def _forget_kernel(f_ref, b_ref, tri_ref, c_ref, *, chunk):
    z = f_ref[...] + b_ref[...]
    lf = jnp.minimum(z, 0.0) - jnp.log1p(jnp.exp(-jnp.abs(z)))
    lft = lf.T[:N_HEADS, :]
    tri = tri_ref[...]
    carry = jnp.zeros((N_HEADS, 1), F32)
    for n in range(lft.shape[1] // chunk):
        xc = lft[:, n * chunk:(n + 1) * chunk]
        hi = xc.astype(BF16)
        r1 = xc - hi.astype(F32)
        mid = r1.astype(BF16)
        lo = (r1 - mid.astype(F32)).astype(BF16)
        loc = (jnp.dot(hi, tri, preferred_element_type=F32)
               + jnp.dot(mid, tri, preferred_element_type=F32)
               + jnp.dot(lo, tri, preferred_element_type=F32)) + carry
        c_ref[0, :, n * chunk:(n + 1) * chunk] = loc
        carry = loc[:, chunk - 1:chunk]


def _forget_cumsum(f_pad, b_pad, batch, seq):
    chunk = _tiles()["cum_chunk"]
    r = lax.broadcasted_iota(jnp.int32, (chunk, chunk), 0)
    c = lax.broadcasted_iota(jnp.int32, (chunk, chunk), 1)
    tri = (r <= c).astype(BF16)
    return pl.pallas_call(
        functools.partial(_forget_kernel, chunk=chunk),
        out_shape=jax.ShapeDtypeStruct((batch, N_HEADS, seq), F32),
        grid=(batch,),
        in_specs=[
            pl.BlockSpec((seq, LANES), lambda b: (b, 0)),
            pl.BlockSpec((1, LANES), lambda b: (0, 0)),
            pl.BlockSpec((chunk, chunk), lambda b: (0, 0)),
        ],
        out_specs=pl.BlockSpec((1, N_HEADS, seq), lambda b: (b, 0, 0)),
        compiler_params=_params(("parallel",)),
        name="forget_cumsum",
    )(f_pad, b_pad, tri)


def _softmax_tile(s, vt):
    m = jnp.max(s, axis=-1, keepdims=True)
    p = jnp.exp2(s - m)
    l = jnp.sum(p, axis=-1, keepdims=True)
    acc = jnp.dot(p.astype(BF16), vt, preferred_element_type=F32)
    return m, l, acc


def _softmax_update(carry, s, vt):
    m, l, acc = carry
    m_new = jnp.maximum(m, jnp.max(s, axis=-1, keepdims=True))
    alpha = jnp.exp2(m - m_new)
    p = jnp.exp2(s - m_new)
    l = alpha * l + jnp.sum(p, axis=-1, keepdims=True)
    acc = alpha * acc + jnp.dot(p.astype(BF16), vt, preferred_element_type=F32)
    return m_new, l, acc


def _fox_kernel(q_ref, k_ref, v_ref, c_ref, o_ref, *, tq):
    i = pl.program_id(2)
    q = (q_ref[0].astype(F32) * (QK_SCALE * LOG2E)).astype(BF16)
    c0 = c_ref[0, 0, i][:, 0:1]

    def scores(j):
        kt = k_ref[0, pl.ds(pl.multiple_of(j * tq, tq), tq), :]
        bias = (c0 - c_ref[0, 0, j]) * LOG2E
        return _dot_nt(q, kt) + bias

    def values(j):
        return v_ref[0, pl.ds(pl.multiple_of(j * tq, tq), tq), :]

    row = lax.broadcasted_iota(jnp.int32, (tq, tq), 0)
    col = lax.broadcasted_iota(jnp.int32, (tq, tq), 1)
    s_diag = jnp.where(col <= row, scores(i), -jnp.inf)
    carry = _softmax_tile(s_diag, values(i))
    carry = lax.fori_loop(0, i, lambda j, c: _softmax_update(c, scores(j), values(j)), carry)
    _, l, acc = carry
    o_ref[0] = (acc / l).astype(BF16)


def _fox(p3, c4, batch, seq):
    tq = _tiles()["fox_tq"]
    assert _tiles()["fox_tk"] == tq
    return pl.pallas_call(
        functools.partial(_fox_kernel, tq=tq),
        out_shape=jax.ShapeDtypeStruct((batch, seq, WIDTH), BF16),
        grid=(batch, N_HEADS, seq // tq),
        in_specs=[
            pl.BlockSpec((1, tq, HEAD_DIM), lambda b, h, i: (b, i, h)),
            pl.BlockSpec((1, seq, HEAD_DIM), lambda b, h, i: (b, 0, N_HEADS + h)),
            pl.BlockSpec((1, seq, HEAD_DIM), lambda b, h, i: (b, 0, 2 * N_HEADS + h)),
            pl.BlockSpec((1, 1, seq // tq, 1, tq), lambda b, h, i: (b, h, 0, 0, 0)),
        ],
        out_specs=pl.BlockSpec((1, tq, HEAD_DIM), lambda b, h, i: (b, i, h)),
        compiler_params=_params(("parallel", "parallel", "arbitrary")),
        name="fox_attention",
    )(p3, p3, p3, c4)


def _moba_kernel(q_ref, k_ref, v_ref, o_ref, km_scr, *, n_blocks):
    i = pl.program_id(2)
    blk = MOBA_BLOCK

    @pl.when(i == 0)
    def _():
        km_scr[...] = jnp.zeros_like(km_scr)
        for n in range(n_blocks):
            kb = k_ref[0, n * blk:(n + 1) * blk, :].astype(F32)
            km_scr[n:n + 1, :] = jnp.mean(kb, axis=0, keepdims=True)

    q_raw = q_ref[0]
    km = km_scr[...]
    km_hi = km.astype(BF16)
    km_lo = (km - km_hi.astype(F32)).astype(BF16)
    gate = _dot_nt(q_raw, km_hi) + _dot_nt(q_raw, km_lo)
    lane = lax.broadcasted_iota(jnp.int32, gate.shape, 1)
    valid = lane < i
    g = jnp.where(valid, gate, -jnp.inf)
    rank = jnp.zeros(gate.shape, jnp.int32)
    for n in range(n_blocks - 1):
        gn = g[:, n:n + 1]
        ahead = jnp.logical_or(gn > g, jnp.logical_and(gn == g, n < lane))
        rank = rank + ahead.astype(jnp.int32)
    sel = jnp.logical_and(valid, rank < MOBA_TOPK).astype(F32)

    q = (q_raw.astype(F32) * (QK_SCALE * LOG2E)).astype(BF16)

    def kv(j):
        start = pl.multiple_of(j * blk, blk)
        return k_ref[0, pl.ds(start, blk), :], v_ref[0, pl.ds(start, blk), :]

    row = lax.broadcasted_iota(jnp.int32, (blk, blk), 0)
    col = lax.broadcasted_iota(jnp.int32, (blk, blk), 1)
    k_own, v_own = kv(i)
    carry = _softmax_tile(jnp.where(col <= row, _dot_nt(q, k_own), -jnp.inf), v_own)

    def past(j, c):
        kt, vt = kv(j)
        chosen = jnp.sum(jnp.where(lane == j, sel, 0.0), axis=-1, keepdims=True) > 0.5
        return _softmax_update(c, jnp.where(chosen, _dot_nt(q, kt), -jnp.inf), vt)

    _, l, acc = lax.fori_loop(0, i, past, carry)
    o_ref[0] = (acc / l).astype(BF16)


def _moba(p3, batch, seq):
    n_blocks = seq // MOBA_BLOCK
    assert seq % MOBA_BLOCK == 0 and n_blocks <= LANES
    return pl.pallas_call(
        functools.partial(_moba_kernel, n_blocks=n_blocks),
        out_shape=jax.ShapeDtypeStruct((batch, seq, WIDTH), BF16),
        grid=(batch, N_HEADS, n_blocks),
        in_specs=[
            pl.BlockSpec((1, MOBA_BLOCK, HEAD_DIM), lambda b, h, i: (b, i, 3 * N_HEADS + h)),
            pl.BlockSpec((1, seq, HEAD_DIM), lambda b, h, i: (b, 0, 4 * N_HEADS + h)),
            pl.BlockSpec((1, seq, HEAD_DIM), lambda b, h, i: (b, 0, 5 * N_HEADS + h)),
        ],
        out_specs=pl.BlockSpec((1, MOBA_BLOCK, HEAD_DIM), lambda b, h, i: (b, i, h)),
        scratch_shapes=[pltpu.VMEM((LANES, HEAD_DIM), F32)],
        compiler_params=_params(("parallel", "parallel", "arbitrary")),
        name="moba_attention",
    )(p3, p3, p3)


def _mix_kernel(oa_ref, ob_ref, wa_ref, wb_ref, ga_ref, gb_ref, y_ref):
    ya = jnp.dot(oa_ref[...], wa_ref[...], preferred_element_type=F32)
    yb = jnp.dot(ob_ref[...], wb_ref[...], preferred_element_type=F32)
    ga = jax.nn.sigmoid(ga_ref[...].astype(F32))
    gb = jax.nn.sigmoid(gb_ref[...].astype(F32))
    y_ref[...] = (ga * ya + gb * yb).astype(BF16)


def _mix(oa, ob, wa, wb, p2):
    t = _tiles()
    tm, tn = t["mix_tm"], t["mix_tn"]
    rows, kdim = oa.shape
    d = wa.shape[1]
    ga_blk = (6 * WIDTH) // tn
    gb_blk = (6 * WIDTH + d) // tn
    return pl.pallas_call(
        _mix_kernel,
        out_shape=jax.ShapeDtypeStruct((rows, d), BF16),
        grid=(rows // tm, d // tn),
        in_specs=[
            pl.BlockSpec((tm, kdim), lambda i, j: (i, 0)),
            pl.BlockSpec((tm, kdim), lambda i, j: (i, 0)),
            pl.BlockSpec((kdim, tn), lambda i, j: (0, j)),
            pl.BlockSpec((kdim, tn), lambda i, j: (0, j)),
            pl.BlockSpec((tm, tn), lambda i, j: (i, ga_blk + j)),
            pl.BlockSpec((tm, tn), lambda i, j: (i, gb_blk + j)),
        ],
        out_specs=pl.BlockSpec((tm, tn), lambda i, j: (i, j)),
        compiler_params=_params(("parallel", "parallel")),
        name="gated_mix",
    )(oa, ob, wa, wb, p2, p2)


def _resid_kernel(x_ref, y_ref, w_ref, g_ref, x1_ref, h_ref):
    x1 = x_ref[...] + jnp.dot(y_ref[...], w_ref[...], preferred_element_type=F32)
    x1_ref[...] = x1
    h_ref[...] = _rms(x1, g_ref[...]).astype(BF16)


def _resid(x2, y, w_out, g):
    tm = _tiles()["resid_tm"]
    rows, d = x2.shape
    return pl.pallas_call(
        _resid_kernel,
        out_shape=(jax.ShapeDtypeStruct((rows, d), F32),
                   jax.ShapeDtypeStruct((rows, d), BF16)),
        grid=(rows // tm,),
        in_specs=[
            pl.BlockSpec((tm, d), lambda i: (i, 0)),
            pl.BlockSpec((tm, d), lambda i: (i, 0)),
            pl.BlockSpec((d, d), lambda i: (0, 0)),
            pl.BlockSpec((1, d), lambda i: (0, 0)),
        ],
        out_specs=(pl.BlockSpec((tm, d), lambda i: (i, 0)),
                   pl.BlockSpec((tm, d), lambda i: (i, 0))),
        compiler_params=_params(("parallel",)),
        name="out_residual",
    )(x2, y, w_out, g)


def _ffn_kernel(h_ref, x1_ref, wg_ref, wu_ref, wd_ref, gf_ref, o_ref, acc_scr):
    k = pl.program_id(1)
    h = h_ref[...]
    gate = jnp.dot(h, wg_ref[...], preferred_element_type=F32)
    up = jnp.dot(h, wu_ref[...], preferred_element_type=F32)
    a = (gate * jax.nn.sigmoid(gate) * up).astype(BF16)
    part = jnp.dot(a, wd_ref[...], preferred_element_type=F32)

    @pl.when(k == 0)
    def _():
        acc_scr[...] = part

    @pl.when(k > 0)
    def _():
        acc_scr[...] += part

    @pl.when(k == pl.num_programs(1) - 1)
    def _():
        o_ref[...] = _rms(x1_ref[...] + acc_scr[...], gf_ref[...])


def _ffn(h2, x1, wg, wu, wd, gf):
    t = _tiles()
    tm, tf = t["ffn_tm"], t["ffn_tf"]
    rows, d = h2.shape
    dff = wd.shape[0]
    return pl.pallas_call(
        _ffn_kernel,
        out_shape=jax.ShapeDtypeStruct((rows, d), F32),
        grid=(rows // tm, dff // tf),
        in_specs=[
            pl.BlockSpec((tm, d), lambda i, k: (i, 0)),
            pl.BlockSpec((tm, d), lambda i, k: (i, 0)),
            pl.BlockSpec((d, tf), lambda i, k: (0, k)),
            pl.BlockSpec((d, tf), lambda i, k: (0, k)),
            pl.BlockSpec((tf, d), lambda i, k: (k, 0)),
            pl.BlockSpec((1, d), lambda i, k: (0, 0)),
        ],
        out_specs=pl.BlockSpec((tm, d), lambda i, k: (i, 0)),
        scratch_shapes=[pltpu.VMEM((tm, d), F32)],
        compiler_params=_params(("parallel", "arbitrary")),
        name="swiglu_ffn",
    )(h2, x1, wg, wu, wd, gf)


def _rope_tables(seq):
    half = HEAD_DIM // 2
    inv_freq = ROPE_THETA ** (-jnp.arange(half, dtype=F32) / half)
    ang = jnp.arange(seq, dtype=F32)[:, None] * inv_freq[None, :]
    cos, sin = jnp.cos(ang), jnp.sin(ang)
    return jnp.concatenate([cos, cos], axis=-1), jnp.concatenate([-sin, sin], axis=-1)


def kernel(x, norm_mix, w_in, b_forget, w_o_fox, w_o_moba, w_out, norm_ffn, w_gate_up, w_down,
           norm_final):
    batch, seq, d = x.shape
    assert w_in.shape[0] == 1
    dff = w_down.shape[1]
    qkv_cols = 6 * WIDTH
    cs, sn = _rope_tables(seq)
    x2 = x.reshape(batch * seq, d)
    wl = w_in[0]
    w_main = jnp.concatenate([wl[:, :qkv_cols], wl[:, qkv_cols + N_HEADS:]], axis=1).astype(BF16)
    w_f = jnp.pad(wl[:, qkv_cols:qkv_cols + N_HEADS], ((0, 0), (0, LANES - N_HEADS))).astype(BF16)
    b_pad = jnp.pad(b_forget[0][None, :], ((0, 0), (0, LANES - N_HEADS)))

    p2, f_pad = _inproj(x2, norm_mix[0][None, :], w_main, w_f, cs, sn, seq)
    c = _forget_cumsum(f_pad, b_pad, batch, seq)
    p3 = p2.reshape(batch, seq, p2.shape[1])
    tq = _tiles()["fox_tq"]
    o_a = _fox(p3, c.reshape(batch, N_HEADS, seq // tq, 1, tq), batch, seq)
    o_b = _moba(p3, batch, seq)
    y = _mix(o_a.reshape(batch * seq, WIDTH), o_b.reshape(batch * seq, WIDTH),
             w_o_fox[0].astype(BF16), w_o_moba[0].astype(BF16), p2)
    x1, h2 = _resid(x2, y, w_out[0].astype(BF16), norm_ffn[0][None, :])
    wgu = w_gate_up[0]
    out = _ffn(h2, x1, wgu[:, :dff].astype(BF16), wgu[:, dff:].astype(BF16),
               w_down[0].astype(BF16), norm_final[None, :])
    return out.reshape(batch, seq, d)
```

```python
import functools
import math

import jax
import jax.numpy as jnp
from jax import lax
from jax.experimental import pallas as pl
from jax.experimental.pallas import tpu as pltpu

F32 = jnp.float32
BF16 = jnp.bfloat16

HEAD_DIM = 128
N_HEADS = 8
WIDTH = N_HEADS * HEAD_DIM
MOBA_BLOCK = 256
MOBA_TOPK = 3
ROPE_THETA = 10000.0
RMS_EPS = 1e-6
LOG2E = math.log2(math.e)
QK_SCALE = HEAD_DIM ** -0.5
LANES = 128
MASK_BIAS = -1e30
VMEM_LIMIT = 56 * 1024 * 1024


def _tiles():
    return dict(
        inproj_tm=1024, inproj_tn=1024,
        attn_tq=256, attn_heads=4,
        mix_tm=1024, mix_tn=1024,
        resid_tm=512,
        ffn_tm=512, ffn_tf=512,
        cum_chunk=256,
    )


def _params(sem):
    return pltpu.CompilerParams(dimension_semantics=sem, vmem_limit_bytes=VMEM_LIMIT)


def _rms(x, g):
    ms = jnp.mean(x * x, axis=-1, keepdims=True)
    return x * lax.rsqrt(ms + RMS_EPS) * g


def _dot_nt(a, b):
    return lax.dot_general(a, b, (((1,), (1,)), ((), ())), preferred_element_type=F32)


def _inproj_kernel(x_ref, g_ref, w_ref, wf_ref, cs_ref, sn_ref, o_ref, f_ref, h_scr,
                   *, rot_lo, rot_hi):
    j = pl.program_id(1)

    @pl.when(j == 0)
    def _():
        h = _rms(x_ref[...], g_ref[...]).astype(BF16)
        h_scr[...] = h
        f_ref[...] = jnp.dot(h, wf_ref[...], preferred_element_type=F32)

    acc = jnp.dot(h_scr[...], w_ref[...], preferred_element_type=F32)
    is_rot = jnp.logical_and(j >= rot_lo, j < rot_hi)

    @pl.when(is_rot)
    def _():
        cs = cs_ref[...]
        sn = sn_ref[...]
        for hh in range(acc.shape[1] // HEAD_DIM):
            t = acc[:, hh * HEAD_DIM:(hh + 1) * HEAD_DIM]
            r = pltpu.roll(t, HEAD_DIM // 2, axis=1)
            o_ref[:, hh * HEAD_DIM:(hh + 1) * HEAD_DIM] = (t * cs + r * sn).astype(BF16)

    @pl.when(jnp.logical_not(is_rot))
    def _():
        o_ref[...] = acc.astype(BF16)


def _inproj(x2, g, w_main, w_f, cs, sn, seq):
    t = _tiles()
    tm, tn = t["inproj_tm"], t["inproj_tn"]
    rows, d = x2.shape
    ncols = w_main.shape[1]
    pos_blocks = seq // tm
    kern = functools.partial(_inproj_kernel, rot_lo=(3 * WIDTH) // tn, rot_hi=(5 * WIDTH) // tn)
    return pl.pallas_call(
        kern,
        out_shape=(jax.ShapeDtypeStruct((rows, ncols), BF16),
                   jax.ShapeDtypeStruct((rows, LANES), F32)),
        grid=(rows // tm, ncols // tn),
        in_specs=[
            pl.BlockSpec((tm, d), lambda i, j: (i, 0)),
            pl.BlockSpec((1, d), lambda i, j: (0, 0)),
            pl.BlockSpec((d, tn), lambda i, j: (0, j)),
            pl.BlockSpec((d, LANES), lambda i, j: (0, 0)),
            pl.BlockSpec((tm, HEAD_DIM), lambda i, j: (i % pos_blocks, 0)),
            pl.BlockSpec((tm, HEAD_DIM), lambda i, j: (i % pos_blocks, 0)),
        ],
        out_specs=(pl.BlockSpec((tm, tn), lambda i, j: (i, j)),
                   pl.BlockSpec((tm, LANES), lambda i, j: (i, 0))),
        scratch_shapes=[pltpu.VMEM((tm, d), BF16)],
        compiler_params=_params(("parallel", "arbitrary")),
        name="inproj",
    )(x2, g, w_main, w_f, cs, sn)


def _forget_kernel(f_ref, b_ref, tri_ref, c_ref, *, chunk):
    z = f_ref[...] + b_ref[...]
    lf = jnp.minimum(z, 0.0) - jnp.log1p(jnp.exp(-jnp.abs(z)))
    lft = lf.T[:N_HEADS, :]
    tri = tri_ref[...]
    carry = jnp.zeros((N_HEADS, 1), F32)
    for n in range(lft.shape[1] // chunk):
        xc = lft[:, n * chunk:(n + 1) * chunk]
        hi = xc.astype(BF16)
        r1 = xc - hi.astype(F32)
        mid = r1.astype(BF16)
        lo = (r1 - mid.astype(F32)).astype(BF16)
        loc = (jnp.dot(hi, tri, preferred_element_type=F32)
               + jnp.dot(mid, tri, preferred_element_type=F32)
               + jnp.dot(lo, tri, preferred_element_type=F32)) + carry
        c_ref[0, :, n * chunk:(n + 1) * chunk] = loc
        carry = loc[:, chunk - 1:chunk]


def _forget_cumsum(f_pad, b_pad, batch, seq):
    chunk = _tiles()["cum_chunk"]
    r = lax.broadcasted_iota(jnp.int32, (chunk, chunk), 0)
    c = lax.broadcasted_iota(jnp.int32, (chunk, chunk), 1)
    tri = (r <= c).astype(BF16)
    return pl.pallas_call(
        functools.partial(_forget_kernel, chunk=chunk),
        out_shape=jax.ShapeDtypeStruct((batch, N_HEADS, seq), F32),
        grid=(batch,),
        in_specs=[
            pl.BlockSpec((seq, LANES), lambda b: (b, 0)),
            pl.BlockSpec((1, LANES), lambda b: (0, 0)),
            pl.BlockSpec((chunk, chunk), lambda b: (0, 0)),
        ],
        out_specs=pl.BlockSpec((1, N_HEADS, seq), lambda b: (b, 0, 0)),
        compiler_params=_params(("parallel",)),
        name="forget_cumsum",
    )(f_pad, b_pad, tri)


def _head(ref, hh, rows=slice(None)):
    return ref[0, rows, hh * HEAD_DIM:(hh + 1) * HEAD_DIM]


def _causal_softmax_pv(s, v, tq):
    length = s.shape[1]
    row = lax.broadcasted_iota(jnp.int32, (tq, tq), 0)
    col = lax.broadcasted_iota(jnp.int32, (tq, tq), 1)
    own = jnp.where(col <= row, s[:, length - tq:], -jnp.inf)
    s = own if length == tq else jnp.concatenate([s[:, :length - tq], own], axis=1)
    m = jnp.max(s, axis=-1, keepdims=True)
    p = jnp.exp2(s - m).astype(BF16)
    v1 = jnp.concatenate([v, jnp.ones(v.shape, v.dtype)], axis=1)
    acc = jnp.dot(p, v1, preferred_element_type=F32)
    return (acc[:, :HEAD_DIM] / acc[:, HEAD_DIM:]).astype(BF16)


def _for_each_query_tile(n_tiles, body):
    i = pl.program_id(2)
    for ii in range(n_tiles):
        pl.when(i == ii)(functools.partial(body, ii))


def _fox_kernel(q_ref, k_ref, v_ref, c_ref, o_ref, *, tq, heads, n_tiles):
    def body(ii):
        length = (ii + 1) * tq
        for hh in range(heads):
            q = (_head(q_ref, hh).astype(F32) * (QK_SCALE * LOG2E)).astype(BF16)
            c = c_ref[0, hh]
            bias = (c[:, ii * tq:ii * tq + 1] - c[:, :length]) * LOG2E
            s = _dot_nt(q, _head(k_ref, hh, slice(0, length))) + bias
            o_ref[0, :, hh * HEAD_DIM:(hh + 1) * HEAD_DIM] = _causal_softmax_pv(
                s, _head(v_ref, hh, slice(0, length)), tq)

    _for_each_query_tile(n_tiles, body)


def _fox(p3, c4, batch, seq):
    t = _tiles()
    tq, heads = t["attn_tq"], t["attn_heads"]
    hw = heads * HEAD_DIM
    groups = N_HEADS // heads
    return pl.pallas_call(
        functools.partial(_fox_kernel, tq=tq, heads=heads, n_tiles=seq // tq),
        out_shape=jax.ShapeDtypeStruct((batch, seq, WIDTH), BF16),
        grid=(batch, groups, seq // tq),
        in_specs=[
            pl.BlockSpec((1, tq, hw), lambda b, g, i: (b, i, g)),
            pl.BlockSpec((1, seq, hw), lambda b, g, i: (b, 0, groups + g)),
            pl.BlockSpec((1, seq, hw), lambda b, g, i: (b, 0, 2 * groups + g)),
            pl.BlockSpec((1, heads, 1, seq), lambda b, g, i: (b, g, 0, 0)),
        ],
        out_specs=pl.BlockSpec((1, tq, hw), lambda b, g, i: (b, i, g)),
        compiler_params=_params(("parallel", "parallel", "arbitrary")),
        name="fox_attention",
    )(p3, p3, p3, c4)


def _moba_select_bias(q_raw, km, ii, n_blocks):
    km_hi = km.astype(BF16)
    km_lo = (km - km_hi.astype(F32)).astype(BF16)
    gate = _dot_nt(q_raw, km_hi) + _dot_nt(q_raw, km_lo)
    lane = lax.broadcasted_iota(jnp.int32, gate.shape, 1)
    valid = lane < ii
    g = jnp.where(valid, gate, -jnp.inf)
    rank = jnp.zeros(gate.shape, jnp.int32)
    for n in range(ii):
        gn = g[:, n:n + 1]
        ahead = jnp.logical_or(gn > g, jnp.logical_and(gn == g, n < lane))
        rank = rank + ahead.astype(jnp.int32)
    visible = jnp.logical_or(jnp.logical_and(valid, rank < MOBA_TOPK), lane == ii)
    return jnp.where(visible, 0.0, MASK_BIAS).astype(BF16)


def _moba_kernel(q_ref, k_ref, v_ref, oh_ref, o_ref, km_scr, *, n_blocks, heads):
    blk = MOBA_BLOCK

    @pl.when(pl.program_id(2) == 0)
    def _():
        km_scr[...] = jnp.zeros_like(km_scr)
        for hh in range(heads):
            for n in range(n_blocks):
                kb = _head(k_ref, hh, slice(n * blk, (n + 1) * blk)).astype(F32)
                km_scr[hh, n:n + 1, :] = jnp.mean(kb, axis=0, keepdims=True)

    def body(ii):
        length = (ii + 1) * blk
        for hh in range(heads):
            q_raw = _head(q_ref, hh)
            bias = _moba_select_bias(q_raw, km_scr[hh], ii, n_blocks)
            q = (q_raw.astype(F32) * (QK_SCALE * LOG2E)).astype(BF16)
            q_aug = jnp.concatenate([q, bias], axis=1)
            k_aug = jnp.concatenate([_head(k_ref, hh, slice(0, length)), oh_ref[:length, :]], axis=1)
            o_ref[0, :, hh * HEAD_DIM:(hh + 1) * HEAD_DIM] = _causal_softmax_pv(
                _dot_nt(q_aug, k_aug), _head(v_ref, hh, slice(0, length)), blk)

    _for_each_query_tile(n_blocks, body)


def _moba(p3, batch, seq):
    n_blocks = seq // MOBA_BLOCK
    assert seq % MOBA_BLOCK == 0 and n_blocks <= LANES
    heads = _tiles()["attn_heads"]
    hw = heads * HEAD_DIM
    groups = N_HEADS // heads
    key_block = lax.broadcasted_iota(jnp.int32, (seq, LANES), 0) // MOBA_BLOCK
    onehot = (key_block == lax.broadcasted_iota(jnp.int32, (seq, LANES), 1)).astype(BF16)
    return pl.pallas_call(
        functools.partial(_moba_kernel, n_blocks=n_blocks, heads=heads),
        out_shape=jax.ShapeDtypeStruct((batch, seq, WIDTH), BF16),
        grid=(batch, groups, n_blocks),
        in_specs=[
            pl.BlockSpec((1, MOBA_BLOCK, hw), lambda b, g, i: (b, i, 3 * groups + g)),
            pl.BlockSpec((1, seq, hw), lambda b, g, i: (b, 0, 4 * groups + g)),
            pl.BlockSpec((1, seq, hw), lambda b, g, i: (b, 0, 5 * groups + g)),
            pl.BlockSpec((seq, LANES), lambda b, g, i: (0, 0)),
        ],
        out_specs=pl.BlockSpec((1, MOBA_BLOCK, hw), lambda b, g, i: (b, i, g)),
        scratch_shapes=[pltpu.VMEM((heads, LANES, HEAD_DIM), F32)],
        compiler_params=_params(("parallel", "parallel", "arbitrary")),
        name="moba_attention",
    )(p3, p3, p3, onehot)


def _mix_kernel(oa_ref, ob_ref, wa_ref, wb_ref, ga_ref, gb_ref, y_ref):
    ya = jnp.dot(oa_ref[...], wa_ref[...], preferred_element_type=F32)
    yb = jnp.dot(ob_ref[...], wb_ref[...], preferred_element_type=F32)
    ga = jax.nn.sigmoid(ga_ref[...].astype(F32))
    gb = jax.nn.sigmoid(gb_ref[...].astype(F32))
    y_ref[...] = (ga * ya + gb * yb).astype(BF16)


def _mix(oa, ob, wa, wb, p2):
    t = _tiles()
    tm, tn = t["mix_tm"], t["mix_tn"]
    rows, kdim = oa.shape
    d = wa.shape[1]
    ga_blk = (6 * WIDTH) // tn
    gb_blk = (6 * WIDTH + d) // tn
    return pl.pallas_call(
        _mix_kernel,
        out_shape=jax.ShapeDtypeStruct((rows, d), BF16),
        grid=(rows // tm, d // tn),
        in_specs=[
            pl.BlockSpec((tm, kdim), lambda i, j: (i, 0)),
            pl.BlockSpec((tm, kdim), lambda i, j: (i, 0)),
            pl.BlockSpec((kdim, tn), lambda i, j: (0, j)),
            pl.BlockSpec((kdim, tn), lambda i, j: (0, j)),
            pl.BlockSpec((tm, tn), lambda i, j: (i, ga_blk + j)),
            pl.BlockSpec((tm, tn), lambda i, j: (i, gb_blk + j)),
        ],
        out_specs=pl.BlockSpec((tm, tn), lambda i, j: (i, j)),
        compiler_params=_params(("parallel", "parallel")),
        name="gated_mix",
    )(oa, ob, wa, wb, p2, p2)


def _resid_kernel(x_ref, y_ref, w_ref, g_ref, x1_ref, h_ref):
    x1 = x_ref[...] + jnp.dot(y_ref[...], w_ref[...], preferred_element_type=F32)
    x1_ref[...] = x1
    h_ref[...] = _rms(x1, g_ref[...]).astype(BF16)


def _resid(x2, y, w_out, g):
    tm = _tiles()["resid_tm"]
    rows, d = x2.shape
    return pl.pallas_call(
        _resid_kernel,
        out_shape=(jax.ShapeDtypeStruct((rows, d), F32),
                   jax.ShapeDtypeStruct((rows, d), BF16)),
        grid=(rows // tm,),
        in_specs=[
            pl.BlockSpec((tm, d), lambda i: (i, 0)),
            pl.BlockSpec((tm, d), lambda i: (i, 0)),
            pl.BlockSpec((d, d), lambda i: (0, 0)),
            pl.BlockSpec((1, d), lambda i: (0, 0)),
        ],
        out_specs=(pl.BlockSpec((tm, d), lambda i: (i, 0)),
                   pl.BlockSpec((tm, d), lambda i: (i, 0))),
        compiler_params=_params(("parallel",)),
        name="out_residual",
    )(x2, y, w_out, g)


def _ffn_kernel(h_ref, x1_ref, wg_ref, wu_ref, wd_ref, gf_ref, o_ref, acc_scr):
    k = pl.program_id(1)
    h = h_ref[...]
    gate = jnp.dot(h, wg_ref[...], preferred_element_type=F32)
    up = jnp.dot(h, wu_ref[...], preferred_element_type=F32)
    a = (gate * jax.nn.sigmoid(gate) * up).astype(BF16)
    part = jnp.dot(a, wd_ref[...], preferred_element_type=F32)

    @pl.when(k == 0)
    def _():
        acc_scr[...] = part

    @pl.when(k > 0)
    def _():
        acc_scr[...] += part

    @pl.when(k == pl.num_programs(1) - 1)
    def _():
        o_ref[...] = _rms(x1_ref[...] + acc_scr[...], gf_ref[...])


def _ffn(h2, x1, wgu, wd, gf):
    t = _tiles()
    tm, tf = t["ffn_tm"], t["ffn_tf"]
    rows, d = h2.shape
    dff = wd.shape[0]
    up_blk = dff // tf
    return pl.pallas_call(
        _ffn_kernel,
        out_shape=jax.ShapeDtypeStruct((rows, d), F32),
        grid=(rows // tm, dff // tf),
        in_specs=[
            pl.BlockSpec((tm, d), lambda i, k: (i, 0)),
            pl.BlockSpec((tm, d), lambda i, k: (i, 0)),
            pl.BlockSpec((d, tf), lambda i, k: (0, k)),
            pl.BlockSpec((d, tf), lambda i, k: (0, up_blk + k)),
            pl.BlockSpec((tf, d), lambda i, k: (k, 0)),
            pl.BlockSpec((1, d), lambda i, k: (0, 0)),
        ],
        out_specs=pl.BlockSpec((tm, d), lambda i, k: (i, 0)),
        scratch_shapes=[pltpu.VMEM((tm, d), F32)],
        compiler_params=_params(("parallel", "arbitrary")),
        name="swiglu_ffn",
    )(h2, x1, wgu, wgu, wd, gf)


def _rope_tables(seq):
    half = HEAD_DIM // 2
    inv_freq = ROPE_THETA ** (-jnp.arange(half, dtype=F32) / half)
    ang = jnp.arange(seq, dtype=F32)[:, None] * inv_freq[None, :]
    cos, sin = jnp.cos(ang), jnp.sin(ang)
    return jnp.concatenate([cos, cos], axis=-1), jnp.concatenate([-sin, sin], axis=-1)


def kernel(x, norm_mix, w_in, b_forget, w_o_fox, w_o_moba, w_out, norm_ffn, w_gate_up, w_down,
           norm_final):
    batch, seq, d = x.shape
    assert w_in.shape[0] == 1
    qkv_cols = 6 * WIDTH
    cs, sn = _rope_tables(seq)
    x2 = x.reshape(batch * seq, d)
    wl = w_in[0]
    w_main = jnp.concatenate([wl[:, :qkv_cols], wl[:, qkv_cols + N_HEADS:]], axis=1).astype(BF16)
    w_f = jnp.pad(wl[:, qkv_cols:qkv_cols + N_HEADS], ((0, 0), (0, LANES - N_HEADS))).astype(BF16)
    b_pad = jnp.pad(b_forget[0][None, :], ((0, 0), (0, LANES - N_HEADS)))

    p2, f_pad = _inproj(x2, norm_mix[0][None, :], w_main, w_f, cs, sn, seq)
    c = _forget_cumsum(f_pad, b_pad, batch, seq)
    p3 = p2.reshape(batch, seq, p2.shape[1])
    o_a = _fox(p3, c.reshape(batch, N_HEADS, 1, seq), batch, seq)
    o_b = _moba(p3, batch, seq)
    y = _mix(o_a.reshape(batch * seq, WIDTH), o_b.reshape(batch * seq, WIDTH),
             w_o_fox[0].astype(BF16), w_o_moba[0].astype(BF16), p2)
    x1, h2 = _resid(x2, y, w_out[0].astype(BF16), norm_ffn[0][None, :])
    out = _ffn(h2, x1, w_gate_up[0].astype(BF16), w_down[0].astype(BF16), norm_final[None, :])
    return out.reshape(batch, seq, d)
```

```python
import functools
import math

import jax
import jax.numpy as jnp
from jax import lax
from jax.experimental import pallas as pl
from jax.experimental.pallas import tpu as pltpu

F32 = jnp.float32
BF16 = jnp.bfloat16

HEAD_DIM = 128
N_HEADS = 8
WIDTH = N_HEADS * HEAD_DIM
MOBA_BLOCK = 256
MOBA_TOPK = 3
ROPE_THETA = 10000.0
RMS_EPS = 1e-6
LOG2E = math.log2(math.e)
QK_SCALE = HEAD_DIM ** -0.5
LANES = 128
MASK_BIAS = -1e30
VMEM_LIMIT = 56 * 1024 * 1024


def _tiles():
    return dict(
        inproj_tm=1024, inproj_tn=1024,
        attn_tq=256, attn_heads=4,
        mix_tm=1024, mix_tn=1024,
        resid_tm=512,
        ffn_tm=512, ffn_tf=512,
        cum_chunk=256,
    )


def _params(sem):
    return pltpu.CompilerParams(dimension_semantics=sem, vmem_limit_bytes=VMEM_LIMIT)


def _rms(x, g):
    ms = jnp.mean(x * x, axis=-1, keepdims=True)
    return x * lax.rsqrt(ms + RMS_EPS) * g


def _dot_nt(a, b):
    return lax.dot_general(a, b, (((1,), (1,)), ((), ())), preferred_element_type=F32)


def _inproj_kernel(x_ref, g_ref, w_ref, wf_ref, cs_ref, sn_ref, o_ref, f_ref, h_scr,
                   *, rot_lo, rot_hi):
    j = pl.program_id(1)

    @pl.when(j == 0)
    def _():
        h = _rms(x_ref[...], g_ref[...]).astype(BF16)
        h_scr[...] = h
        f_ref[...] = jnp.dot(h, wf_ref[...], preferred_element_type=F32)

    acc = jnp.dot(h_scr[...], w_ref[...], preferred_element_type=F32)
    o_ref[...] = acc.astype(BF16)

    @pl.when(jnp.logical_and(j >= rot_lo, j < rot_hi))
    def _():
        cs = cs_ref[...]
        sn = sn_ref[...]
        for hh in range(acc.shape[1] // HEAD_DIM):
            t = acc[:, hh * HEAD_DIM:(hh + 1) * HEAD_DIM]
            r = pltpu.roll(t, HEAD_DIM // 2, axis=1)
            o_ref[:, hh * HEAD_DIM:(hh + 1) * HEAD_DIM] = (t * cs + r * sn).astype(BF16)


def _inproj(x2, g, w_main, w_f, cs, sn, seq):
    t = _tiles()
    tm, tn = t["inproj_tm"], t["inproj_tn"]
    rows, d = x2.shape
    ncols = w_main.shape[1]
    pos_blocks = seq // tm
    kern = functools.partial(_inproj_kernel, rot_lo=(3 * WIDTH) // tn, rot_hi=(5 * WIDTH) // tn)
    return pl.pallas_call(
        kern,
        out_shape=(jax.ShapeDtypeStruct((rows, ncols), BF16),
                   jax.ShapeDtypeStruct((rows, LANES), F32)),
        grid=(rows // tm, ncols // tn),
        in_specs=[
            pl.BlockSpec((tm, d), lambda i, j: (i, 0)),
            pl.BlockSpec((1, d), lambda i, j: (0, 0)),
            pl.BlockSpec((d, tn), lambda i, j: (0, j)),
            pl.BlockSpec((d, LANES), lambda i, j: (0, 0)),
            pl.BlockSpec((tm, HEAD_DIM), lambda i, j: (i % pos_blocks, 0)),
            pl.BlockSpec((tm, HEAD_DIM), lambda i, j: (i % pos_blocks, 0)),
        ],
        out_specs=(pl.BlockSpec((tm, tn), lambda i, j: (i, j)),
                   pl.BlockSpec((tm, LANES), lambda i, j: (i, 0))),
        scratch_shapes=[pltpu.VMEM((tm, d), BF16)],
        compiler_params=_params(("parallel", "arbitrary")),
        name="inproj",
    )(x2, g, w_main, w_f, cs, sn)


def _forget_kernel(f_ref, b_ref, tri_ref, c_ref, *, chunk):
    z = f_ref[...] + b_ref[...]
    lf = jnp.minimum(z, 0.0) - jnp.log1p(jnp.exp(-jnp.abs(z)))
    lft = lf.T[:N_HEADS, :]
    tri = tri_ref[...]
    carry = jnp.zeros((N_HEADS, 1), F32)
    for n in range(lft.shape[1] // chunk):
        xc = lft[:, n * chunk:(n + 1) * chunk]
        hi = xc.astype(BF16)
        r1 = xc - hi.astype(F32)
        mid = r1.astype(BF16)
        lo = (r1 - mid.astype(F32)).astype(BF16)
        loc = (jnp.dot(hi, tri, preferred_element_type=F32)
               + jnp.dot(mid, tri, preferred_element_type=F32)
               + jnp.dot(lo, tri, preferred_element_type=F32)) + carry
        c_ref[0, :, n * chunk:(n + 1) * chunk] = loc
        carry = loc[:, chunk - 1:chunk]


def _forget_cumsum(f_pad, b_pad, batch, seq):
    chunk = _tiles()["cum_chunk"]
    r = lax.broadcasted_iota(jnp.int32, (chunk, chunk), 0)
    c = lax.broadcasted_iota(jnp.int32, (chunk, chunk), 1)
    tri = (r <= c).astype(BF16)
    return pl.pallas_call(
        functools.partial(_forget_kernel, chunk=chunk),
        out_shape=jax.ShapeDtypeStruct((batch, N_HEADS, seq), F32),
        grid=(batch,),
        in_specs=[
            pl.BlockSpec((seq, LANES), lambda b: (b, 0)),
            pl.BlockSpec((1, LANES), lambda b: (0, 0)),
            pl.BlockSpec((chunk, chunk), lambda b: (0, 0)),
        ],
        out_specs=pl.BlockSpec((1, N_HEADS, seq), lambda b: (b, 0, 0)),
        compiler_params=_params(("parallel",)),
        name="forget_cumsum",
    )(f_pad, b_pad, tri)


def _head(ref, hh, rows=slice(None)):
    return ref[0, rows, hh * HEAD_DIM:(hh + 1) * HEAD_DIM]


def _causal_softmax_pv(s, v, tq):
    length = s.shape[1]
    row = lax.broadcasted_iota(jnp.int32, (tq, tq), 0)
    col = lax.broadcasted_iota(jnp.int32, (tq, tq), 1)
    own = jnp.where(col <= row, s[:, length - tq:], -jnp.inf)
    s = own if length == tq else jnp.concatenate([s[:, :length - tq], own], axis=1)
    m = jnp.max(s, axis=-1, keepdims=True)
    p = jnp.exp2(s - m).astype(BF16)
    v1 = jnp.concatenate([v, jnp.ones(v.shape, v.dtype)], axis=1)
    acc = jnp.dot(p, v1, preferred_element_type=F32)
    return (acc[:, :HEAD_DIM] / acc[:, HEAD_DIM:]).astype(BF16)


def _for_each_query_tile(n_tiles, body):
    i = pl.program_id(2)
    for ii in range(n_tiles):
        pl.when(i == ii)(functools.partial(body, ii))


def _fox_kernel(q_ref, k_ref, v_ref, c_ref, o_ref, *, tq, heads, n_tiles):
    def body(ii):
        length = (ii + 1) * tq
        for hh in range(heads):
            q = (_head(q_ref, hh).astype(F32) * (QK_SCALE * LOG2E)).astype(BF16)
            c = c_ref[0, hh]
            bias = (c[:, ii * tq:ii * tq + 1] - c[:, :length]) * LOG2E
            s = _dot_nt(q, _head(k_ref, hh, slice(0, length))) + bias
            o_ref[0, :, hh * HEAD_DIM:(hh + 1) * HEAD_DIM] = _causal_softmax_pv(
                s, _head(v_ref, hh, slice(0, length)), tq)

    _for_each_query_tile(n_tiles, body)


def _fox(p3, c4, batch, seq):
    t = _tiles()
    tq, heads = t["attn_tq"], t["attn_heads"]
    hw = heads * HEAD_DIM
    groups = N_HEADS // heads
    return pl.pallas_call(
        functools.partial(_fox_kernel, tq=tq, heads=heads, n_tiles=seq // tq),
        out_shape=jax.ShapeDtypeStruct((batch, seq, WIDTH), BF16),
        grid=(batch, groups, seq // tq),
        in_specs=[
            pl.BlockSpec((1, tq, hw), lambda b, g, i: (b, i, g)),
            pl.BlockSpec((1, seq, hw), lambda b, g, i: (b, 0, groups + g)),
            pl.BlockSpec((1, seq, hw), lambda b, g, i: (b, 0, 2 * groups + g)),
            pl.BlockSpec((1, heads, 1, seq), lambda b, g, i: (b, g, 0, 0)),
        ],
        out_specs=pl.BlockSpec((1, tq, hw), lambda b, g, i: (b, i, g)),
        compiler_params=_params(("parallel", "parallel", "arbitrary")),
        name="fox_attention",
    )(p3, p3, p3, c4)


def _moba_select_bias(q_raw, km, ii, n_blocks):
    km_hi = km.astype(BF16)
    km_lo = (km - km_hi.astype(F32)).astype(BF16)
    gate = _dot_nt(q_raw, km_hi) + _dot_nt(q_raw, km_lo)
    lane = lax.broadcasted_iota(jnp.int32, gate.shape, 1)
    valid = lane < ii
    g = jnp.where(valid, gate, -jnp.inf)
    rank = jnp.zeros(gate.shape, jnp.int32)
    for n in range(ii):
        gn = g[:, n:n + 1]
        ahead = jnp.logical_or(gn > g, jnp.logical_and(gn == g, n < lane))
        rank = rank + ahead.astype(jnp.int32)
    visible = jnp.logical_or(jnp.logical_and(valid, rank < MOBA_TOPK), lane == ii)
    return jnp.where(visible, 0.0, MASK_BIAS).astype(BF16)


def _moba_kernel(q_ref, k_ref, v_ref, oh_ref, o_ref, km_scr, *, n_blocks, heads):
    blk = MOBA_BLOCK

    @pl.when(pl.program_id(2) == 0)
    def _():
        km_scr[...] = jnp.zeros_like(km_scr)
        for hh in range(heads):
            for n in range(n_blocks):
                kb = _head(k_ref, hh, slice(n * blk, (n + 1) * blk)).astype(F32)
                km_scr[hh, n:n + 1, :] = jnp.mean(kb, axis=0, keepdims=True)

    def body(ii):
        length = (ii + 1) * blk
        for hh in range(heads):
            q_raw = _head(q_ref, hh)
            bias = _moba_select_bias(q_raw, km_scr[hh], ii, n_blocks)
            q = (q_raw.astype(F32) * (QK_SCALE * LOG2E)).astype(BF16)
            q_aug = jnp.concatenate([q, bias], axis=1)
            k_aug = jnp.concatenate([_head(k_ref, hh, slice(0, length)), oh_ref[:length, :]], axis=1)
            o_ref[0, :, hh * HEAD_DIM:(hh + 1) * HEAD_DIM] = _causal_softmax_pv(
                _dot_nt(q_aug, k_aug), _head(v_ref, hh, slice(0, length)), blk)

    _for_each_query_tile(n_blocks, body)


def _moba(p3, batch, seq):
    n_blocks = seq // MOBA_BLOCK
    assert seq % MOBA_BLOCK == 0 and n_blocks <= LANES
    heads = _tiles()["attn_heads"]
    hw = heads * HEAD_DIM
    groups = N_HEADS // heads
    key_block = lax.broadcasted_iota(jnp.int32, (seq, LANES), 0) // MOBA_BLOCK
    onehot = (key_block == lax.broadcasted_iota(jnp.int32, (seq, LANES), 1)).astype(BF16)
    return pl.pallas_call(
        functools.partial(_moba_kernel, n_blocks=n_blocks, heads=heads),
        out_shape=jax.ShapeDtypeStruct((batch, seq, WIDTH), BF16),
        grid=(batch, groups, n_blocks),
        in_specs=[
            pl.BlockSpec((1, MOBA_BLOCK, hw), lambda b, g, i: (b, i, 3 * groups + g)),
            pl.BlockSpec((1, seq, hw), lambda b, g, i: (b, 0, 4 * groups + g)),
            pl.BlockSpec((1, seq, hw), lambda b, g, i: (b, 0, 5 * groups + g)),
            pl.BlockSpec((seq, LANES), lambda b, g, i: (0, 0)),
        ],
        out_specs=pl.BlockSpec((1, MOBA_BLOCK, hw), lambda b, g, i: (b, i, g)),
        scratch_shapes=[pltpu.VMEM((heads, LANES, HEAD_DIM), F32)],
        compiler_params=_params(("parallel", "parallel", "arbitrary")),
        name="moba_attention",
    )(p3, p3, p3, onehot)


def _mix_kernel(oa_ref, ob_ref, wa_ref, wb_ref, ga_ref, gb_ref, y_ref):
    ya = jnp.dot(oa_ref[...], wa_ref[...], preferred_element_type=F32)
    yb = jnp.dot(ob_ref[...], wb_ref[...], preferred_element_type=F32)
    ga = jax.nn.sigmoid(ga_ref[...].astype(F32))
    gb = jax.nn.sigmoid(gb_ref[...].astype(F32))
    y_ref[...] = (ga * ya + gb * yb).astype(BF16)


def _mix(oa, ob, wa, wb, p2):
    t = _tiles()
    tm, tn = t["mix_tm"], t["mix_tn"]
    rows, kdim = oa.shape
    d = wa.shape[1]
    ga_blk = (6 * WIDTH) // tn
    gb_blk = (6 * WIDTH + d) // tn
    return pl.pallas_call(
        _mix_kernel,
        out_shape=jax.ShapeDtypeStruct((rows, d), BF16),
        grid=(rows // tm, d // tn),
        in_specs=[
            pl.BlockSpec((tm, kdim), lambda i, j: (i, 0)),
            pl.BlockSpec((tm, kdim), lambda i, j: (i, 0)),
            pl.BlockSpec((kdim, tn), lambda i, j: (0, j)),
            pl.BlockSpec((kdim, tn), lambda i, j: (0, j)),
            pl.BlockSpec((tm, tn), lambda i, j: (i, ga_blk + j)),
            pl.BlockSpec((tm, tn), lambda i, j: (i, gb_blk + j)),
        ],
        out_specs=pl.BlockSpec((tm, tn), lambda i, j: (i, j)),
        compiler_params=_params(("parallel", "parallel")),
        name="gated_mix",
    )(oa, ob, wa, wb, p2, p2)


def _resid_kernel(x_ref, y_ref, w_ref, g_ref, x1_ref, h_ref):
    x1 = x_ref[...] + jnp.dot(y_ref[...], w_ref[...], preferred_element_type=F32)
    x1_ref[...] = x1
    h_ref[...] = _rms(x1, g_ref[...]).astype(BF16)


def _resid(x2, y, w_out, g):
    tm = _tiles()["resid_tm"]
    rows, d = x2.shape
    return pl.pallas_call(
        _resid_kernel,
        out_shape=(jax.ShapeDtypeStruct((rows, d), F32),
                   jax.ShapeDtypeStruct((rows, d), BF16)),
        grid=(rows // tm,),
        in_specs=[
            pl.BlockSpec((tm, d), lambda i: (i, 0)),
            pl.BlockSpec((tm, d), lambda i: (i, 0)),
            pl.BlockSpec((d, d), lambda i: (0, 0)),
            pl.BlockSpec((1, d), lambda i: (0, 0)),
        ],
        out_specs=(pl.BlockSpec((tm, d), lambda i: (i, 0)),
                   pl.BlockSpec((tm, d), lambda i: (i, 0))),
        compiler_params=_params(("parallel",)),
        name="out_residual",
    )(x2, y, w_out, g)


def _ffn_kernel(h_ref, x1_ref, wg_ref, wu_ref, wd_ref, gf_ref, o_ref, acc_scr):
    k = pl.program_id(1)

    @pl.when(k == 0)
    def _():
        acc_scr[...] = x1_ref[...]

    h = h_ref[...]
    gate = jnp.dot(h, wg_ref[...], preferred_element_type=F32)
    up = jnp.dot(h, wu_ref[...], preferred_element_type=F32)
    a = (gate * jax.nn.sigmoid(gate) * up).astype(BF16)
    acc_scr[...] += jnp.dot(a, wd_ref[...], preferred_element_type=F32)

    @pl.when(k == pl.num_programs(1) - 1)
    def _():
        o_ref[...] = _rms(acc_scr[...], gf_ref[...])


def _ffn(h2, x1, wgu, wd, gf):
    t = _tiles()
    tm, tf = t["ffn_tm"], t["ffn_tf"]
    rows, d = h2.shape
    dff = wd.shape[0]
    up_blk = dff // tf
    return pl.pallas_call(
        _ffn_kernel,
        out_shape=jax.ShapeDtypeStruct((rows, d), F32),
        grid=(rows // tm, dff // tf),
        in_specs=[
            pl.BlockSpec((tm, d), lambda i, k: (i, 0)),
            pl.BlockSpec((tm, d), lambda i, k: (i, 0)),
            pl.BlockSpec((d, tf), lambda i, k: (0, k)),
            pl.BlockSpec((d, tf), lambda i, k: (0, up_blk + k)),
            pl.BlockSpec((tf, d), lambda i, k: (k, 0)),
            pl.BlockSpec((1, d), lambda i, k: (0, 0)),
        ],
        out_specs=pl.BlockSpec((tm, d), lambda i, k: (i, 0)),
        scratch_shapes=[pltpu.VMEM((tm, d), F32)],
        compiler_params=_params(("parallel", "arbitrary")),
        name="swiglu_ffn",
    )(h2, x1, wgu, wgu, wd, gf)


def _rope_tables(seq):
    half = HEAD_DIM // 2
    inv_freq = ROPE_THETA ** (-jnp.arange(half, dtype=F32) / half)
    ang = jnp.arange(seq, dtype=F32)[:, None] * inv_freq[None, :]
    cos, sin = jnp.cos(ang), jnp.sin(ang)
    return jnp.concatenate([cos, cos], axis=-1), jnp.concatenate([-sin, sin], axis=-1)


def kernel(x, norm_mix, w_in, b_forget, w_o_fox, w_o_moba, w_out, norm_ffn, w_gate_up, w_down,
           norm_final):
    batch, seq, d = x.shape
    assert w_in.shape[0] == 1
    qkv_cols = 6 * WIDTH
    cs, sn = _rope_tables(seq)
    x2 = x.reshape(batch * seq, d)
    wl = w_in[0]
    w_main = jnp.concatenate([wl[:, :qkv_cols], wl[:, qkv_cols + N_HEADS:]], axis=1).astype(BF16)
    w_f = jnp.pad(wl[:, qkv_cols:qkv_cols + N_HEADS], ((0, 0), (0, LANES - N_HEADS))).astype(BF16)
    b_pad = jnp.pad(b_forget[0][None, :], ((0, 0), (0, LANES - N_HEADS)))

    p2, f_pad = _inproj(x2, norm_mix[0][None, :], w_main, w_f, cs, sn, seq)
    c = _forget_cumsum(f_pad, b_pad, batch, seq)
    p3 = p2.reshape(batch, seq, p2.shape[1])
    o_a = _fox(p3, c.reshape(batch, N_HEADS, 1, seq), batch, seq)
    o_b = _moba(p3, batch, seq)
    y = _mix(o_a.reshape(batch * seq, WIDTH), o_b.reshape(batch * seq, WIDTH),
             w_o_fox[0].astype(BF16), w_o_moba[0].astype(BF16), p2)
    x1, h2 = _resid(x2, y, w_out[0].astype(BF16), norm_ffn[0][None, :])
    out = _ffn(h2, x1, w_gate_up[0].astype(BF16), w_down[0].astype(BF16), norm_final[None, :])
    return out.reshape(batch, seq, d)
```

```python
import functools
import math

import jax
import jax.numpy as jnp
from jax import lax
from jax.experimental import pallas as pl
from jax.experimental.pallas import tpu as pltpu

F32 = jnp.float32
BF16 = jnp.bfloat16

HEAD_DIM = 128
N_HEADS = 8
WIDTH = N_HEADS * HEAD_DIM
MOBA_BLOCK = 256
MOBA_TOPK = 3
ROPE_THETA = 10000.0
RMS_EPS = 1e-6
LOG2E = math.log2(math.e)
QK_SCALE = HEAD_DIM ** -0.5
LANES = 128
MASK_BIAS = -1e30
VMEM_LIMIT = 56 * 1024 * 1024


def _tiles():
    return dict(
        norm_tm=512, proj_tm=1024, proj_tn=1024, cast_rows=256,
        attn_tq=256, attn_heads=4,
        mix_tm=1024, mix_tn=1024,
        resid_tm=512,
        ffn_tm=512, ffn_tf=512,
        cum_chunk=256,
    )


def _params(sem):
    return pltpu.CompilerParams(dimension_semantics=sem, vmem_limit_bytes=VMEM_LIMIT)


def _rms(x, g):
    ms = jnp.mean(x * x, axis=-1, keepdims=True)
    return x * lax.rsqrt(ms + RMS_EPS) * g


def _dot_nt(a, b):
    return lax.dot_general(a, b, (((1,), (1,)), ((), ())), preferred_element_type=F32)


def _split_bf16(v):
    hi = v.astype(BF16)
    return hi, (v - hi.astype(F32)).astype(BF16)


def _norm_kernel(x_ref, g_ref, wf_ref, h_ref, f_ref):
    h = _rms(x_ref[...], g_ref[...])
    h_hi, h_lo = _split_bf16(h)
    h_ref[...] = h_hi
    w_hi, w_lo = _split_bf16(wf_ref[...])
    f_ref[...] = (jnp.dot(h_hi, w_hi, preferred_element_type=F32)
                  + jnp.dot(h_hi, w_lo, preferred_element_type=F32)
                  + jnp.dot(h_lo, w_hi, preferred_element_type=F32))


def _norm(x2, g, w2d):
    tm = _tiles()["norm_tm"]
    rows, d = x2.shape
    f_blk = (6 * WIDTH) // LANES
    return pl.pallas_call(
        _norm_kernel,
        out_shape=(jax.ShapeDtypeStruct((rows, d), BF16),
                   jax.ShapeDtypeStruct((rows, LANES), F32)),
        grid=(rows // tm,),
        in_specs=[
            pl.BlockSpec((tm, d), lambda i: (i, 0)),
            pl.BlockSpec((1, d), lambda i: (0, 0)),
            pl.BlockSpec((d, LANES), lambda i: (0, f_blk)),
        ],
        out_specs=(pl.BlockSpec((tm, d), lambda i: (i, 0)),
                   pl.BlockSpec((tm, LANES), lambda i: (i, 0))),
        compiler_params=_params(("parallel",)),
        name="pre_norm",
    )(x2, g, w2d)


def _cast_weight_tile(w_ref, w2_ref, w_scr, shift):
    d, tn = w_scr.shape
    rows = _tiles()["cast_rows"]

    def chunk(r, _):
        rs = pl.ds(pl.multiple_of(r * rows, rows), rows)
        if shift == 0:
            w_scr[rs, :] = w_ref[rs, :].astype(BF16)
        else:
            w = jnp.concatenate([w_ref[rs, :], w2_ref[rs, :]], axis=1)
            w_scr[rs, :] = pltpu.roll(w, tn + LANES - shift, axis=1)[:, :tn].astype(BF16)
        return 0

    lax.fori_loop(0, d // rows, chunk, 0)


def _proj_kernel(h_ref, w_ref, w2_ref, o_ref, w_scr, *, n_aligned):
    j = pl.program_id(0)

    @pl.when(jnp.logical_and(pl.program_id(1) == 0, j < n_aligned))
    def _():
        _cast_weight_tile(w_ref, w2_ref, w_scr, 0)

    @pl.when(jnp.logical_and(pl.program_id(1) == 0, j >= n_aligned))
    def _():
        _cast_weight_tile(w_ref, w2_ref, w_scr, N_HEADS)

    o_ref[...] = jnp.dot(h_ref[...], w_scr[...], preferred_element_type=F32).astype(BF16)


def _proj_rot_kernel(h_ref, w_ref, cs_ref, sn_ref, o_ref, w_scr):
    @pl.when(pl.program_id(1) == 0)
    def _():
        _cast_weight_tile(w_ref, None, w_scr, 0)

    acc = jnp.dot(h_ref[...], w_scr[...], preferred_element_type=F32)
    cs = cs_ref[...]
    sn = sn_ref[...]
    for hh in range(acc.shape[1] // HEAD_DIM):
        t = acc[:, hh * HEAD_DIM:(hh + 1) * HEAD_DIM]
        r = pltpu.roll(t, HEAD_DIM // 2, axis=1)
        o_ref[:, hh * HEAD_DIM:(hh + 1) * HEAD_DIM] = (t * cs + r * sn).astype(BF16)


def _proj_plain(h, w2d):
    t = _tiles()
    tm, tn = t["proj_tm"], t["proj_tn"]
    rows, d = h.shape
    per_group = WIDTH // tn
    n_aligned = 4 * per_group
    n_gate = (2 * d) // tn
    gate0 = (6 * WIDTH) // tn

    def w_blk(j, i):
        aligned = jnp.where(j < 3 * per_group, j, j + 2 * per_group)
        return 0, jnp.where(j < n_aligned, aligned, gate0 + (j - n_aligned))

    def w2_blk(j, i):
        nxt = jnp.maximum(j - n_aligned, 0) + 1
        return 0, (gate0 + nxt) * (tn // LANES)

    return pl.pallas_call(
        functools.partial(_proj_kernel, n_aligned=n_aligned),
        out_shape=jax.ShapeDtypeStruct((rows, (n_aligned + n_gate) * tn), BF16),
        grid=(n_aligned + n_gate, rows // tm),
        in_specs=[
            pl.BlockSpec((tm, d), lambda j, i: (i, 0)),
            pl.BlockSpec((d, tn), w_blk),
            pl.BlockSpec((d, LANES), w2_blk),
        ],
        out_specs=pl.BlockSpec((tm, tn), lambda j, i: (i, j)),
        scratch_shapes=[pltpu.VMEM((d, tn), BF16)],
        compiler_params=_params(("arbitrary", "arbitrary")),
        name="proj_plain",
    )(h, w2d, w2d)


def _proj_rot(h, w2d, cs, sn, seq):
    t = _tiles()
    tm, tn = t["proj_tm"], t["proj_tn"]
    rows, d = h.shape
    first = (3 * WIDTH) // tn
    pos_blocks = seq // tm
    return pl.pallas_call(
        _proj_rot_kernel,
        out_shape=jax.ShapeDtypeStruct((rows, 2 * WIDTH), BF16),
        grid=((2 * WIDTH) // tn, rows // tm),
        in_specs=[
            pl.BlockSpec((tm, d), lambda j, i: (i, 0)),
            pl.BlockSpec((d, tn), lambda j, i: (0, first + j)),
            pl.BlockSpec((tm, HEAD_DIM), lambda j, i: (i % pos_blocks, 0)),
            pl.BlockSpec((tm, HEAD_DIM), lambda j, i: (i % pos_blocks, 0)),
        ],
        out_specs=pl.BlockSpec((tm, tn), lambda j, i: (i, j)),
        scratch_shapes=[pltpu.VMEM((d, tn), BF16)],
        compiler_params=_params(("arbitrary", "arbitrary")),
        name="proj_rotary",
    )(h, w2d, cs, sn)


def _forget_kernel(f_ref, b_ref, tri_ref, c_ref, *, chunk):
    z = f_ref[...] + b_ref[...]
    lf = jnp.minimum(z, 0.0) - jnp.log1p(jnp.exp(-jnp.abs(z)))
    lft = lf.T[:N_HEADS, :]
    tri = tri_ref[...]
    carry = jnp.zeros((N_HEADS, 1), F32)
    for n in range(lft.shape[1] // chunk):
        xc = lft[:, n * chunk:(n + 1) * chunk]
        hi = xc.astype(BF16)
        r1 = xc - hi.astype(F32)
        mid = r1.astype(BF16)
        lo = (r1 - mid.astype(F32)).astype(BF16)
        loc = (jnp.dot(hi, tri, preferred_element_type=F32)
               + jnp.dot(mid, tri, preferred_element_type=F32)
               + jnp.dot(lo, tri, preferred_element_type=F32)) + carry
        c_ref[0, :, n * chunk:(n + 1) * chunk] = loc
        carry = loc[:, chunk - 1:chunk]


def _forget_cumsum(f_pad, b_pad, batch, seq):
    chunk = _tiles()["cum_chunk"]
    r = lax.broadcasted_iota(jnp.int32, (chunk, chunk), 0)
    c = lax.broadcasted_iota(jnp.int32, (chunk, chunk), 1)
    tri = (r <= c).astype(BF16)
    return pl.pallas_call(
        functools.partial(_forget_kernel, chunk=chunk),
        out_shape=jax.ShapeDtypeStruct((batch, N_HEADS, seq), F32),
        grid=(batch,),
        in_specs=[
            pl.BlockSpec((seq, LANES), lambda b: (b, 0)),
            pl.BlockSpec((1, LANES), lambda b: (0, 0)),
            pl.BlockSpec((chunk, chunk), lambda b: (0, 0)),
        ],
        out_specs=pl.BlockSpec((1, N_HEADS, seq), lambda b: (b, 0, 0)),
        compiler_params=_params(("parallel",)),
        name="forget_cumsum",
    )(f_pad, b_pad, tri)


def _head(ref, hh, rows=slice(None)):
    return ref[0, rows, hh * HEAD_DIM:(hh + 1) * HEAD_DIM]


def _causal_softmax_pv(s, v, tq):
    length = s.shape[1]
    row = lax.broadcasted_iota(jnp.int32, (tq, tq), 0)
    col = lax.broadcasted_iota(jnp.int32, (tq, tq), 1)
    own = jnp.where(col <= row, s[:, length - tq:], -jnp.inf)
    s = own if length == tq else jnp.concatenate([s[:, :length - tq], own], axis=1)
    m = jnp.max(s, axis=-1, keepdims=True)
    p = jnp.exp2(s - m).astype(BF16)
    v1 = jnp.concatenate([v, jnp.ones(v.shape, v.dtype)], axis=1)
    acc = jnp.dot(p, v1, preferred_element_type=F32)
    return (acc[:, :HEAD_DIM] / acc[:, HEAD_DIM:]).astype(BF16)


def _for_each_query_tile(n_tiles, body):
    i = pl.program_id(2)
    for ii in range(n_tiles):
        pl.when(i == ii)(functools.partial(body, ii))


def _fox_kernel(q_ref, k_ref, v_ref, c_ref, o_ref, *, tq, heads, n_tiles):
    def body(ii):
        length = (ii + 1) * tq
        for hh in range(heads):
            q = (_head(q_ref, hh).astype(F32) * (QK_SCALE * LOG2E)).astype(BF16)
            c = c_ref[0, hh]
            bias = (c[:, ii * tq:ii * tq + 1] - c[:, :length]) * LOG2E
            s = _dot_nt(q, _head(k_ref, hh, slice(0, length))) + bias
            o_ref[0, :, hh * HEAD_DIM:(hh + 1) * HEAD_DIM] = _causal_softmax_pv(
                s, _head(v_ref, hh, slice(0, length)), tq)

    _for_each_query_tile(n_tiles, body)


def _fox(p3, c4, batch, seq):
    t = _tiles()
    tq, heads = t["attn_tq"], t["attn_heads"]
    hw = heads * HEAD_DIM
    groups = N_HEADS // heads
    return pl.pallas_call(
        functools.partial(_fox_kernel, tq=tq, heads=heads, n_tiles=seq // tq),
        out_shape=jax.ShapeDtypeStruct((batch, seq, WIDTH), BF16),
        grid=(batch, groups, seq // tq),
        in_specs=[
            pl.BlockSpec((1, tq, hw), lambda b, g, i: (b, i, g)),
            pl.BlockSpec((1, seq, hw), lambda b, g, i: (b, 0, groups + g)),
            pl.BlockSpec((1, seq, hw), lambda b, g, i: (b, 0, 2 * groups + g)),
            pl.BlockSpec((1, heads, 1, seq), lambda b, g, i: (b, g, 0, 0)),
        ],
        out_specs=pl.BlockSpec((1, tq, hw), lambda b, g, i: (b, i, g)),
        compiler_params=_params(("parallel", "parallel", "arbitrary")),
        name="fox_attention",
    )(p3, p3, p3, c4)


def _moba_select_bias(q_raw, km, ii, n_blocks):
    km_hi = km.astype(BF16)
    km_lo = (km - km_hi.astype(F32)).astype(BF16)
    gate = _dot_nt(q_raw, km_hi) + _dot_nt(q_raw, km_lo)
    lane = lax.broadcasted_iota(jnp.int32, gate.shape, 1)
    valid = lane < ii
    g = jnp.where(valid, gate, -jnp.inf)
    rank = jnp.zeros(gate.shape, jnp.int32)
    for n in range(ii):
        gn = g[:, n:n + 1]
        ahead = jnp.logical_or(gn > g, jnp.logical_and(gn == g, n < lane))
        rank = rank + ahead.astype(jnp.int32)
    visible = jnp.logical_or(jnp.logical_and(valid, rank < MOBA_TOPK), lane == ii)
    return jnp.where(visible, 0.0, MASK_BIAS).astype(BF16)


def _moba_kernel(q_ref, k_ref, v_ref, oh_ref, o_ref, km_scr, *, n_blocks, heads):
    blk = MOBA_BLOCK

    @pl.when(pl.program_id(2) == 0)
    def _():
        km_scr[...] = jnp.zeros_like(km_scr)
        for hh in range(heads):
            for n in range(n_blocks):
                kb = _head(k_ref, hh, slice(n * blk, (n + 1) * blk)).astype(F32)
                km_scr[hh, n:n + 1, :] = jnp.mean(kb, axis=0, keepdims=True)

    def body(ii):
        length = (ii + 1) * blk
        for hh in range(heads):
            q_raw = _head(q_ref, hh)
            bias = _moba_select_bias(q_raw, km_scr[hh], ii, n_blocks)
            q = (q_raw.astype(F32) * (QK_SCALE * LOG2E)).astype(BF16)
            q_aug = jnp.concatenate([q, bias], axis=1)
            k_aug = jnp.concatenate([_head(k_ref, hh, slice(0, length)), oh_ref[:length, :]], axis=1)
            o_ref[0, :, hh * HEAD_DIM:(hh + 1) * HEAD_DIM] = _causal_softmax_pv(
                _dot_nt(q_aug, k_aug), _head(v_ref, hh, slice(0, length)), blk)

    _for_each_query_tile(n_blocks, body)


def _moba(qk3, p3, batch, seq):
    n_blocks = seq // MOBA_BLOCK
    assert seq % MOBA_BLOCK == 0 and n_blocks <= LANES
    heads = _tiles()["attn_heads"]
    hw = heads * HEAD_DIM
    groups = N_HEADS // heads
    key_block = lax.broadcasted_iota(jnp.int32, (seq, LANES), 0) // MOBA_BLOCK
    onehot = (key_block == lax.broadcasted_iota(jnp.int32, (seq, LANES), 1)).astype(BF16)
    return pl.pallas_call(
        functools.partial(_moba_kernel, n_blocks=n_blocks, heads=heads),
        out_shape=jax.ShapeDtypeStruct((batch, seq, WIDTH), BF16),
        grid=(batch, groups, n_blocks),
        in_specs=[
            pl.BlockSpec((1, MOBA_BLOCK, hw), lambda b, g, i: (b, i, g)),
            pl.BlockSpec((1, seq, hw), lambda b, g, i: (b, 0, groups + g)),
            pl.BlockSpec((1, seq, hw), lambda b, g, i: (b, 0, 3 * groups + g)),
            pl.BlockSpec((seq, LANES), lambda b, g, i: (0, 0)),
        ],
        out_specs=pl.BlockSpec((1, MOBA_BLOCK, hw), lambda b, g, i: (b, i, g)),
        scratch_shapes=[pltpu.VMEM((heads, LANES, HEAD_DIM), F32)],
        compiler_params=_params(("parallel", "parallel", "arbitrary")),
        name="moba_attention",
    )(qk3, qk3, p3, onehot)


def _mix_kernel(oa_ref, ob_ref, wa_ref, wb_ref, ga_ref, gb_ref, y_ref):
    ya = jnp.dot(oa_ref[...], wa_ref[...], preferred_element_type=F32)
    yb = jnp.dot(ob_ref[...], wb_ref[...], preferred_element_type=F32)
    ga = jax.nn.sigmoid(ga_ref[...].astype(F32))
    gb = jax.nn.sigmoid(gb_ref[...].astype(F32))
    y_ref[...] = (ga * ya + gb * yb).astype(BF16)


def _mix(oa, ob, wa, wb, p2):
    t = _tiles()
    tm, tn = t["mix_tm"], t["mix_tn"]
    rows, kdim = oa.shape
    d = wa.shape[1]
    ga_blk = (4 * WIDTH) // tn
    gb_blk = (4 * WIDTH + d) // tn
    return pl.pallas_call(
        _mix_kernel,
        out_shape=jax.ShapeDtypeStruct((rows, d), BF16),
        grid=(rows // tm, d // tn),
        in_specs=[
            pl.BlockSpec((tm, kdim), lambda i, j: (i, 0)),
            pl.BlockSpec((tm, kdim), lambda i, j: (i, 0)),
            pl.BlockSpec((kdim, tn), lambda i, j: (0, j)),
            pl.BlockSpec((kdim, tn), lambda i, j: (0, j)),
            pl.BlockSpec((tm, tn), lambda i, j: (i, ga_blk + j)),
            pl.BlockSpec((tm, tn), lambda i, j: (i, gb_blk + j)),
        ],
        out_specs=pl.BlockSpec((tm, tn), lambda i, j: (i, j)),
        compiler_params=_params(("parallel", "parallel")),
        name="gated_mix",
    )(oa, ob, wa, wb, p2, p2)


def _resid_kernel(x_ref, y_ref, w_ref, g_ref, x1_ref, h_ref):
    x1 = x_ref[...] + jnp.dot(y_ref[...], w_ref[...], preferred_element_type=F32)
    x1_ref[...] = x1
    h_ref[...] = _rms(x1, g_ref[...]).astype(BF16)


def _resid(x2, y, w_out, g):
    tm = _tiles()["resid_tm"]
    rows, d = x2.shape
    return pl.pallas_call(
        _resid_kernel,
        out_shape=(jax.ShapeDtypeStruct((rows, d), F32),
                   jax.ShapeDtypeStruct((rows, d), BF16)),
        grid=(rows // tm,),
        in_specs=[
            pl.BlockSpec((tm, d), lambda i: (i, 0)),
            pl.BlockSpec((tm, d), lambda i: (i, 0)),
            pl.BlockSpec((d, d), lambda i: (0, 0)),
            pl.BlockSpec((1, d), lambda i: (0, 0)),
        ],
        out_specs=(pl.BlockSpec((tm, d), lambda i: (i, 0)),
                   pl.BlockSpec((tm, d), lambda i: (i, 0))),
        compiler_params=_params(("parallel",)),
        name="out_residual",
    )(x2, y, w_out, g)


def _ffn_kernel(h_ref, x1_ref, wg_ref, wu_ref, wd_ref, gf_ref, o_ref, acc_scr):
    k = pl.program_id(1)

    @pl.when(k == 0)
    def _():
        acc_scr[...] = x1_ref[...]

    h = h_ref[...]
    gate = jnp.dot(h, wg_ref[...], preferred_element_type=F32)
    up = jnp.dot(h, wu_ref[...], preferred_element_type=F32)
    a = (gate * jax.nn.sigmoid(gate) * up).astype(BF16)
    acc_scr[...] += jnp.dot(a, wd_ref[...], preferred_element_type=F32)

    @pl.when(k == pl.num_programs(1) - 1)
    def _():
        o_ref[...] = _rms(acc_scr[...], gf_ref[...])


def _ffn(h2, x1, wgu, wd, gf):
    t = _tiles()
    tm, tf = t["ffn_tm"], t["ffn_tf"]
    rows, d = h2.shape
    dff = wd.shape[0]
    up_blk = dff // tf
    return pl.pallas_call(
        _ffn_kernel,
        out_shape=jax.ShapeDtypeStruct((rows, d), F32),
        grid=(rows // tm, dff // tf),
        in_specs=[
            pl.BlockSpec((tm, d), lambda i, k: (i, 0)),
            pl.BlockSpec((tm, d), lambda i, k: (i, 0)),
            pl.BlockSpec((d, tf), lambda i, k: (0, k)),
            pl.BlockSpec((d, tf), lambda i, k: (0, up_blk + k)),
            pl.BlockSpec((tf, d), lambda i, k: (k, 0)),
            pl.BlockSpec((1, d), lambda i, k: (0, 0)),
        ],
        out_specs=pl.BlockSpec((tm, d), lambda i, k: (i, 0)),
        scratch_shapes=[pltpu.VMEM((tm, d), F32)],
        compiler_params=_params(("parallel", "arbitrary")),
        name="swiglu_ffn",
    )(h2, x1, wgu, wgu, wd, gf)


def _rope_tables(seq):
    half = HEAD_DIM // 2
    inv_freq = ROPE_THETA ** (-jnp.arange(half, dtype=F32) / half)
    ang = jnp.arange(seq, dtype=F32)[:, None] * inv_freq[None, :]
    cos, sin = jnp.cos(ang), jnp.sin(ang)
    return jnp.concatenate([cos, cos], axis=-1), jnp.concatenate([-sin, sin], axis=-1)


def kernel(x, norm_mix, w_in, b_forget, w_o_fox, w_o_moba, w_out, norm_ffn, w_gate_up, w_down,
           norm_final):
    batch, seq, d = x.shape
    assert w_in.shape[0] == 1
    cs, sn = _rope_tables(seq)
    x2 = x.reshape(batch * seq, d)
    w2d = w_in[0]
    b_pad = jnp.pad(b_forget[0][None, :], ((0, 0), (0, LANES - N_HEADS)))

    h, f_pad = _norm(x2, norm_mix[0][None, :], w2d)
    p2 = _proj_plain(h, w2d)
    qk = _proj_rot(h, w2d, cs, sn, seq)
    c = _forget_cumsum(f_pad, b_pad, batch, seq)
    p3 = p2.reshape(batch, seq, p2.shape[1])
    o_a = _fox(p3, c.reshape(batch, N_HEADS, 1, seq), batch, seq)
    o_b = _moba(qk.reshape(batch, seq, qk.shape[1]), p3, batch, seq)
    y = _mix(o_a.reshape(batch * seq, WIDTH), o_b.reshape(batch * seq, WIDTH),
             w_o_fox[0].astype(BF16), w_o_moba[0].astype(BF16), p2)
    x1, h2 = _resid(x2, y, w_out[0].astype(BF16), norm_ffn[0][None, :])
    out = _ffn(h2, x1, w_gate_up[0].astype(BF16), w_down[0].astype(BF16), norm_final[None, :])
    return out.reshape(batch, seq, d)
```

```python
import functools
import math

import jax
import jax.numpy as jnp
from jax import lax
from jax.experimental import pallas as pl
from jax.experimental.pallas import tpu as pltpu

F32 = jnp.float32
BF16 = jnp.bfloat16

HEAD_DIM = 128
N_HEADS = 8
WIDTH = N_HEADS * HEAD_DIM
MOBA_BLOCK = 256
MOBA_TOPK = 3
ROPE_THETA = 10000.0
RMS_EPS = 1e-6
LOG2E = math.log2(math.e)
QK_SCALE = HEAD_DIM ** -0.5
LANES = 128
MASK_BIAS = -1e30
VMEM_LIMIT = 56 * 1024 * 1024


def _tiles():
    return dict(
        norm_tm=512, proj_tm=1024, proj_tn=1024,
        attn_tq=256, attn_heads=4,
        mix_tm=1024, mix_tn=1024,
        resid_tm=512,
        ffn_tm=512, ffn_tf=512,
        cum_chunk=256,
    )


def _params(sem):
    return pltpu.CompilerParams(dimension_semantics=sem, vmem_limit_bytes=VMEM_LIMIT)


def _rms(x, g):
    ms = jnp.mean(x * x, axis=-1, keepdims=True)
    return x * lax.rsqrt(ms + RMS_EPS) * g


def _dot_nt(a, b):
    return lax.dot_general(a, b, (((1,), (1,)), ((), ())), preferred_element_type=F32)


def _split_bf16(v):
    hi = v.astype(BF16)
    return hi, (v - hi.astype(F32)).astype(BF16)


def _norm_kernel(x_ref, g_ref, wf_ref, h_ref, f_ref):
    h = _rms(x_ref[...], g_ref[...])
    h_hi, h_lo = _split_bf16(h)
    h_ref[...] = h_hi
    w_hi, w_lo = _split_bf16(wf_ref[...])
    f_ref[...] = _dot_nt(h_hi, w_hi) + _dot_nt(h_hi, w_lo) + _dot_nt(h_lo, w_hi)


def _norm(x2, g, wt):
    tm = _tiles()["norm_tm"]
    rows, d = x2.shape
    f_blk = (6 * WIDTH) // LANES
    return pl.pallas_call(
        _norm_kernel,
        out_shape=(jax.ShapeDtypeStruct((rows, d), BF16),
                   jax.ShapeDtypeStruct((rows, LANES), F32)),
        grid=(rows // tm,),
        in_specs=[
            pl.BlockSpec((tm, d), lambda i: (i, 0)),
            pl.BlockSpec((1, d), lambda i: (0, 0)),
            pl.BlockSpec((LANES, d), lambda i: (f_blk, 0)),
        ],
        out_specs=(pl.BlockSpec((tm, d), lambda i: (i, 0)),
                   pl.BlockSpec((tm, LANES), lambda i: (i, 0))),
        compiler_params=_params(("parallel",)),
        name="pre_norm",
    )(x2, g, wt)


def _cast_weight_tile(w_ref, w2_ref, w_scr, shift):
    if shift == 0:
        w_scr[...] = w_ref[...].astype(BF16)
    else:
        w_scr[...] = jnp.concatenate([w_ref[shift:, :], w2_ref[...]], axis=0).astype(BF16)


def _proj_kernel(h_ref, w_ref, w2_ref, o_ref, w_scr, *, n_aligned):
    j = pl.program_id(0)

    @pl.when(jnp.logical_and(pl.program_id(1) == 0, j < n_aligned))
    def _():
        _cast_weight_tile(w_ref, w2_ref, w_scr, 0)

    @pl.when(jnp.logical_and(pl.program_id(1) == 0, j >= n_aligned))
    def _():
        _cast_weight_tile(w_ref, w2_ref, w_scr, N_HEADS)

    o_ref[...] = _dot_nt(h_ref[...], w_scr[...]).astype(BF16)


def _proj_rot_kernel(h_ref, w_ref, cs_ref, sn_ref, o_ref, w_scr):
    @pl.when(pl.program_id(1) == 0)
    def _():
        _cast_weight_tile(w_ref, None, w_scr, 0)

    acc = _dot_nt(h_ref[...], w_scr[...])
    cs = cs_ref[...]
    sn = sn_ref[...]
    for hh in range(acc.shape[1] // HEAD_DIM):
        t = acc[:, hh * HEAD_DIM:(hh + 1) * HEAD_DIM]
        r = pltpu.roll(t, HEAD_DIM // 2, axis=1)
        o_ref[:, hh * HEAD_DIM:(hh + 1) * HEAD_DIM] = (t * cs + r * sn).astype(BF16)


def _proj_plain(h, wt):
    t = _tiles()
    tm, tn = t["proj_tm"], t["proj_tn"]
    rows, d = h.shape
    per_group = WIDTH // tn
    n_aligned = 4 * per_group
    n_gate = (2 * d) // tn
    gate0 = (6 * WIDTH) // tn

    def w_blk(j, i):
        aligned = jnp.where(j < 3 * per_group, j, j + 2 * per_group)
        return jnp.where(j < n_aligned, aligned, gate0 + (j - n_aligned)), 0

    def w2_blk(j, i):
        nxt = jnp.maximum(j - n_aligned, 0) + 1
        return (gate0 + nxt) * (tn // N_HEADS), 0

    return pl.pallas_call(
        functools.partial(_proj_kernel, n_aligned=n_aligned),
        out_shape=jax.ShapeDtypeStruct((rows, (n_aligned + n_gate) * tn), BF16),
        grid=(n_aligned + n_gate, rows // tm),
        in_specs=[
            pl.BlockSpec((tm, d), lambda j, i: (i, 0)),
            pl.BlockSpec((tn, d), w_blk),
            pl.BlockSpec((N_HEADS, d), w2_blk),
        ],
        out_specs=pl.BlockSpec((tm, tn), lambda j, i: (i, j)),
        scratch_shapes=[pltpu.VMEM((tn, d), BF16)],
        compiler_params=_params(("arbitrary", "arbitrary")),
        name="proj_plain",
    )(h, wt, wt)


def _proj_rot(h, wt, cs, sn, seq):
    t = _tiles()
    tm, tn = t["proj_tm"], t["proj_tn"]
    rows, d = h.shape
    first = (3 * WIDTH) // tn
    pos_blocks = seq // tm
    return pl.pallas_call(
        _proj_rot_kernel,
        out_shape=jax.ShapeDtypeStruct((rows, 2 * WIDTH), BF16),
        grid=((2 * WIDTH) // tn, rows // tm),
        in_specs=[
            pl.BlockSpec((tm, d), lambda j, i: (i, 0)),
            pl.BlockSpec((tn, d), lambda j, i: (first + j, 0)),
            pl.BlockSpec((tm, HEAD_DIM), lambda j, i: (i % pos_blocks, 0)),
            pl.BlockSpec((tm, HEAD_DIM), lambda j, i: (i % pos_blocks, 0)),
        ],
        out_specs=pl.BlockSpec((tm, tn), lambda j, i: (i, j)),
        scratch_shapes=[pltpu.VMEM((tn, d), BF16)],
        compiler_params=_params(("arbitrary", "arbitrary")),
        name="proj_rotary",
    )(h, wt, cs, sn)


def _forget_kernel(f_ref, b_ref, tri_ref, c_ref, *, chunk):
    z = f_ref[...] + b_ref[...]
    lf = jnp.minimum(z, 0.0) - jnp.log1p(jnp.exp(-jnp.abs(z)))
    lft = lf.T[:N_HEADS, :]
    tri = tri_ref[...]
    carry = jnp.zeros((N_HEADS, 1), F32)
    for n in range(lft.shape[1] // chunk):
        xc = lft[:, n * chunk:(n + 1) * chunk]
        hi = xc.astype(BF16)
        r1 = xc - hi.astype(F32)
        mid = r1.astype(BF16)
        lo = (r1 - mid.astype(F32)).astype(BF16)
        loc = (jnp.dot(hi, tri, preferred_element_type=F32)
               + jnp.dot(mid, tri, preferred_element_type=F32)
               + jnp.dot(lo, tri, preferred_element_type=F32)) + carry
        c_ref[0, :, n * chunk:(n + 1) * chunk] = loc
        carry = loc[:, chunk - 1:chunk]


def _forget_cumsum(f_pad, b_pad, batch, seq):
    chunk = _tiles()["cum_chunk"]
    r = lax.broadcasted_iota(jnp.int32, (chunk, chunk), 0)
    c = lax.broadcasted_iota(jnp.int32, (chunk, chunk), 1)
    tri = (r <= c).astype(BF16)
    return pl.pallas_call(
        functools.partial(_forget_kernel, chunk=chunk),
        out_shape=jax.ShapeDtypeStruct((batch, N_HEADS, seq), F32),
        grid=(batch,),
        in_specs=[
            pl.BlockSpec((seq, LANES), lambda b: (b, 0)),
            pl.BlockSpec((1, LANES), lambda b: (0, 0)),
            pl.BlockSpec((chunk, chunk), lambda b: (0, 0)),
        ],
        out_specs=pl.BlockSpec((1, N_HEADS, seq), lambda b: (b, 0, 0)),
        compiler_params=_params(("parallel",)),
        name="forget_cumsum",
    )(f_pad, b_pad, tri)


def _head(ref, hh, rows=slice(None)):
    return ref[0, rows, hh * HEAD_DIM:(hh + 1) * HEAD_DIM]


def _causal_softmax_pv(s, v, tq):
    length = s.shape[1]
    row = lax.broadcasted_iota(jnp.int32, (tq, tq), 0)
    col = lax.broadcasted_iota(jnp.int32, (tq, tq), 1)
    own = jnp.where(col <= row, s[:, length - tq:], -jnp.inf)
    s = own if length == tq else jnp.concatenate([s[:, :length - tq], own], axis=1)
    m = jnp.max(s, axis=-1, keepdims=True)
    p = jnp.exp2(s - m).astype(BF16)
    v1 = jnp.concatenate([v, jnp.ones(v.shape, v.dtype)], axis=1)
    acc = jnp.dot(p, v1, preferred_element_type=F32)
    return (acc[:, :HEAD_DIM] / acc[:, HEAD_DIM:]).astype(BF16)


def _for_each_query_tile(n_tiles, body):
    i = pl.program_id(2)
    for ii in range(n_tiles):
        pl.when(i == ii)(functools.partial(body, ii))


def _fox_kernel(q_ref, k_ref, v_ref, c_ref, o_ref, *, tq, heads, n_tiles):
    def body(ii):
        length = (ii + 1) * tq
        for hh in range(heads):
            q = (_head(q_ref, hh).astype(F32) * (QK_SCALE * LOG2E)).astype(BF16)
            c = c_ref[0, hh]
            bias = (c[:, ii * tq:ii * tq + 1] - c[:, :length]) * LOG2E
            s = _dot_nt(q, _head(k_ref, hh, slice(0, length))) + bias
            o_ref[0, :, hh * HEAD_DIM:(hh + 1) * HEAD_DIM] = _causal_softmax_pv(
                s, _head(v_ref, hh, slice(0, length)), tq)

    _for_each_query_tile(n_tiles, body)


def _fox(p3, c4, batch, seq):
    t = _tiles()
    tq, heads = t["attn_tq"], t["attn_heads"]
    hw = heads * HEAD_DIM
    groups = N_HEADS // heads
    return pl.pallas_call(
        functools.partial(_fox_kernel, tq=tq, heads=heads, n_tiles=seq // tq),
        out_shape=jax.ShapeDtypeStruct((batch, seq, WIDTH), BF16),
        grid=(batch, groups, seq // tq),
        in_specs=[
            pl.BlockSpec((1, tq, hw), lambda b, g, i: (b, i, g)),
            pl.BlockSpec((1, seq, hw), lambda b, g, i: (b, 0, groups + g)),
            pl.BlockSpec((1, seq, hw), lambda b, g, i: (b, 0, 2 * groups + g)),
            pl.BlockSpec((1, heads, 1, seq), lambda b, g, i: (b, g, 0, 0)),
        ],
        out_specs=pl.BlockSpec((1, tq, hw), lambda b, g, i: (b, i, g)),
        compiler_params=_params(("parallel", "parallel", "arbitrary")),
        name="fox_attention",
    )(p3, p3, p3, c4)


def _moba_select_bias(q_raw, km, ii, n_blocks):
    km_hi = km.astype(BF16)
    km_lo = (km - km_hi.astype(F32)).astype(BF16)
    gate = _dot_nt(q_raw, km_hi) + _dot_nt(q_raw, km_lo)
    lane = lax.broadcasted_iota(jnp.int32, gate.shape, 1)
    valid = lane < ii
    g = jnp.where(valid, gate, -jnp.inf)
    rank = jnp.zeros(gate.shape, jnp.int32)
    for n in range(ii):
        gn = g[:, n:n + 1]
        ahead = jnp.logical_or(gn > g, jnp.logical_and(gn == g, n < lane))
        rank = rank + ahead.astype(jnp.int32)
    visible = jnp.logical_or(jnp.logical_and(valid, rank < MOBA_TOPK), lane == ii)
    return jnp.where(visible, 0.0, MASK_BIAS).astype(BF16)


def _moba_kernel(q_ref, k_ref, v_ref, oh_ref, o_ref, km_scr, *, n_blocks, heads):
    blk = MOBA_BLOCK

    @pl.when(pl.program_id(2) == 0)
    def _():
        km_scr[...] = jnp.zeros_like(km_scr)
        for hh in range(heads):
            for n in range(n_blocks):
                kb = _head(k_ref, hh, slice(n * blk, (n + 1) * blk)).astype(F32)
                km_scr[hh, n:n + 1, :] = jnp.mean(kb, axis=0, keepdims=True)

    def body(ii):
        length = (ii + 1) * blk
        for hh in range(heads):
            q_raw = _head(q_ref, hh)
            bias = _moba_select_bias(q_raw, km_scr[hh], ii, n_blocks)
            q = (q_raw.astype(F32) * (QK_SCALE * LOG2E)).astype(BF16)
            q_aug = jnp.concatenate([q, bias], axis=1)
            k_aug = jnp.concatenate([_head(k_ref, hh, slice(0, length)), oh_ref[:length, :]], axis=1)
            o_ref[0, :, hh * HEAD_DIM:(hh + 1) * HEAD_DIM] = _causal_softmax_pv(
                _dot_nt(q_aug, k_aug), _head(v_ref, hh, slice(0, length)), blk)

    _for_each_query_tile(n_blocks, body)


def _moba(qk3, p3, batch, seq):
    n_blocks = seq // MOBA_BLOCK
    assert seq % MOBA_BLOCK == 0 and n_blocks <= LANES
    heads = _tiles()["attn_heads"]
    hw = heads * HEAD_DIM
    groups = N_HEADS // heads
    key_block = lax.broadcasted_iota(jnp.int32, (seq, LANES), 0) // MOBA_BLOCK
    onehot = (key_block == lax.broadcasted_iota(jnp.int32, (seq, LANES), 1)).astype(BF16)
    return pl.pallas_call(
        functools.partial(_moba_kernel, n_blocks=n_blocks, heads=heads),
        out_shape=jax.ShapeDtypeStruct((batch, seq, WIDTH), BF16),
        grid=(batch, groups, n_blocks),
        in_specs=[
            pl.BlockSpec((1, MOBA_BLOCK, hw), lambda b, g, i: (b, i, g)),
            pl.BlockSpec((1, seq, hw), lambda b, g, i: (b, 0, groups + g)),
            pl.BlockSpec((1, seq, hw), lambda b, g, i: (b, 0, 3 * groups + g)),
            pl.BlockSpec((seq, LANES), lambda b, g, i: (0, 0)),
        ],
        out_specs=pl.BlockSpec((1, MOBA_BLOCK, hw), lambda b, g, i: (b, i, g)),
        scratch_shapes=[pltpu.VMEM((heads, LANES, HEAD_DIM), F32)],
        compiler_params=_params(("parallel", "parallel", "arbitrary")),
        name="moba_attention",
    )(qk3, qk3, p3, onehot)


def _mix_kernel(oa_ref, ob_ref, wa_ref, wb_ref, ga_ref, gb_ref, y_ref):
    ya = jnp.dot(oa_ref[...], wa_ref[...], preferred_element_type=F32)
    yb = jnp.dot(ob_ref[...], wb_ref[...], preferred_element_type=F32)
    ga = jax.nn.sigmoid(ga_ref[...].astype(F32))
    gb = jax.nn.sigmoid(gb_ref[...].astype(F32))
    y_ref[...] = (ga * ya + gb * yb).astype(BF16)


def _mix(oa, ob, wa, wb, p2):
    t = _tiles()
    tm, tn = t["mix_tm"], t["mix_tn"]
    rows, kdim = oa.shape
    d = wa.shape[1]
    ga_blk = (4 * WIDTH) // tn
    gb_blk = (4 * WIDTH + d) // tn
    return pl.pallas_call(
        _mix_kernel,
        out_shape=jax.ShapeDtypeStruct((rows, d), BF16),
        grid=(rows // tm, d // tn),
        in_specs=[
            pl.BlockSpec((tm, kdim), lambda i, j: (i, 0)),
            pl.BlockSpec((tm, kdim), lambda i, j: (i, 0)),
            pl.BlockSpec((kdim, tn), lambda i, j: (0, j)),
            pl.BlockSpec((kdim, tn), lambda i, j: (0, j)),
            pl.BlockSpec((tm, tn), lambda i, j: (i, ga_blk + j)),
            pl.BlockSpec((tm, tn), lambda i, j: (i, gb_blk + j)),
        ],
        out_specs=pl.BlockSpec((tm, tn), lambda i, j: (i, j)),
        compiler_params=_params(("parallel", "parallel")),
        name="gated_mix",
    )(oa, ob, wa, wb, p2, p2)


def _resid_kernel(x_ref, y_ref, w_ref, g_ref, x1_ref, h_ref):
    x1 = x_ref[...] + jnp.dot(y_ref[...], w_ref[...], preferred_element_type=F32)
    x1_ref[...] = x1
    h_ref[...] = _rms(x1, g_ref[...]).astype(BF16)


def _resid(x2, y, w_out, g):
    tm = _tiles()["resid_tm"]
    rows, d = x2.shape
    return pl.pallas_call(
        _resid_kernel,
        out_shape=(jax.ShapeDtypeStruct((rows, d), F32),
                   jax.ShapeDtypeStruct((rows, d), BF16)),
        grid=(rows // tm,),
        in_specs=[
            pl.BlockSpec((tm, d), lambda i: (i, 0)),
            pl.BlockSpec((tm, d), lambda i: (i, 0)),
            pl.BlockSpec((d, d), lambda i: (0, 0)),
            pl.BlockSpec((1, d), lambda i: (0, 0)),
        ],
        out_specs=(pl.BlockSpec((tm, d), lambda i: (i, 0)),
                   pl.BlockSpec((tm, d), lambda i: (i, 0))),
        compiler_params=_params(("parallel",)),
        name="out_residual",
    )(x2, y, w_out, g)


def _ffn_kernel(h_ref, x1_ref, wg_ref, wu_ref, wd_ref, gf_ref, o_ref, acc_scr):
    k = pl.program_id(1)

    @pl.when(k == 0)
    def _():
        acc_scr[...] = x1_ref[...]

    h = h_ref[...]
    gate = jnp.dot(h, wg_ref[...], preferred_element_type=F32)
    up = jnp.dot(h, wu_ref[...], preferred_element_type=F32)
    a = (gate * jax.nn.sigmoid(gate) * up).astype(BF16)
    acc_scr[...] += jnp.dot(a, wd_ref[...], preferred_element_type=F32)

    @pl.when(k == pl.num_programs(1) - 1)
    def _():
        o_ref[...] = _rms(acc_scr[...], gf_ref[...])


def _ffn(h2, x1, wgu, wd, gf):
    t = _tiles()
    tm, tf = t["ffn_tm"], t["ffn_tf"]
    rows, d = h2.shape
    dff = wd.shape[0]
    up_blk = dff // tf
    return pl.pallas_call(
        _ffn_kernel,
        out_shape=jax.ShapeDtypeStruct((rows, d), F32),
        grid=(rows // tm, dff // tf),
        in_specs=[
            pl.BlockSpec((tm, d), lambda i, k: (i, 0)),
            pl.BlockSpec((tm, d), lambda i, k: (i, 0)),
            pl.BlockSpec((d, tf), lambda i, k: (0, k)),
            pl.BlockSpec((d, tf), lambda i, k: (0, up_blk + k)),
            pl.BlockSpec((tf, d), lambda i, k: (k, 0)),
            pl.BlockSpec((1, d), lambda i, k: (0, 0)),
        ],
        out_specs=pl.BlockSpec((tm, d), lambda i, k: (i, 0)),
        scratch_shapes=[pltpu.VMEM((tm, d), F32)],
        compiler_params=_params(("parallel", "arbitrary")),
        name="swiglu_ffn",
    )(h2, x1, wgu, wgu, wd, gf)


def _rope_tables(seq):
    half = HEAD_DIM // 2
    inv_freq = ROPE_THETA ** (-jnp.arange(half, dtype=F32) / half)
    ang = jnp.arange(seq, dtype=F32)[:, None] * inv_freq[None, :]
    cos, sin = jnp.cos(ang), jnp.sin(ang)
    return jnp.concatenate([cos, cos], axis=-1), jnp.concatenate([-sin, sin], axis=-1)


def kernel(x, norm_mix, w_in, b_forget, w_o_fox, w_o_moba, w_out, norm_ffn, w_gate_up, w_down,
           norm_final):
    batch, seq, d = x.shape
    assert w_in.shape[0] == 1
    cs, sn = _rope_tables(seq)
    x2 = x.reshape(batch * seq, d)
    wt = w_in[0].T
    b_pad = jnp.pad(b_forget[0][None, :], ((0, 0), (0, LANES - N_HEADS)))

    h, f_pad = _norm(x2, norm_mix[0][None, :], wt)
    p2 = _proj_plain(h, wt)
    qk = _proj_rot(h, wt, cs, sn, seq)
    c = _forget_cumsum(f_pad, b_pad, batch, seq)
    p3 = p2.reshape(batch, seq, p2.shape[1])
    o_a = _fox(p3, c.reshape(batch, N_HEADS, 1, seq), batch, seq)
    o_b = _moba(qk.reshape(batch, seq, qk.shape[1]), p3, batch, seq)
    y = _mix(o_a.reshape(batch * seq, WIDTH), o_b.reshape(batch * seq, WIDTH),
             w_o_fox[0].astype(BF16), w_o_moba[0].astype(BF16), p2)
    x1, h2 = _resid(x2, y, w_out[0].astype(BF16), norm_ffn[0][None, :])
    out = _ffn(h2, x1, w_gate_up[0].astype(BF16), w_down[0].astype(BF16), norm_final[None, :])
    return out.reshape(batch, seq, d)
```

```python
import functools
import math

import jax
import jax.numpy as jnp
from jax import lax
from jax.experimental import pallas as pl
from jax.experimental.pallas import tpu as pltpu

F32 = jnp.float32
BF16 = jnp.bfloat16

HEAD_DIM = 128
N_HEADS = 8
WIDTH = N_HEADS * HEAD_DIM
MOBA_BLOCK = 256
MOBA_TOPK = 3
ROPE_THETA = 10000.0
RMS_EPS = 1e-6
LOG2E = math.log2(math.e)
QK_SCALE = HEAD_DIM ** -0.5
LANES = 128
MASK_BIAS = -1e30
HEAD_PAIR = 4
VMEM_LIMIT = 56 * 1024 * 1024


def _tiles():
    return dict(
        norm_tm=512, proj_tm=1024, proj_tn=1024,
        attn_tq=256, attn_heads=4,
        mix_tm=1024, mix_tn=1024,
        resid_tm=512,
        ffn_tm=512, ffn_tf=512,
        cum_chunk=256,
    )


def _params(sem):
    return pltpu.CompilerParams(dimension_semantics=sem, vmem_limit_bytes=VMEM_LIMIT)


def _rms(x, g):
    ms = jnp.mean(x * x, axis=-1, keepdims=True)
    return x * lax.rsqrt(ms + RMS_EPS) * g


def _dot_nt(a, b):
    return lax.dot_general(a, b, (((1,), (1,)), ((), ())), preferred_element_type=F32)


def _split_bf16(v):
    hi = v.astype(BF16)
    return hi, (v - hi.astype(F32)).astype(BF16)


def _norm_kernel(x_ref, g_ref, wf_ref, h_ref, f_ref):
    h = _rms(x_ref[...], g_ref[...])
    h_hi, h_lo = _split_bf16(h)
    h_ref[...] = h_hi
    w_hi, w_lo = _split_bf16(wf_ref[...])
    f_ref[...] = _dot_nt(h_hi, w_hi) + _dot_nt(h_hi, w_lo) + _dot_nt(h_lo, w_hi)


def _norm(x2, g, wt):
    tm = _tiles()["norm_tm"]
    rows, d = x2.shape
    f_blk = (6 * WIDTH) // LANES
    return pl.pallas_call(
        _norm_kernel,
        out_shape=(jax.ShapeDtypeStruct((rows, d), BF16),
                   jax.ShapeDtypeStruct((rows, LANES), F32)),
        grid=(rows // tm,),
        in_specs=[
            pl.BlockSpec((tm, d), lambda i: (i, 0)),
            pl.BlockSpec((1, d), lambda i: (0, 0)),
            pl.BlockSpec((LANES, d), lambda i: (f_blk, 0)),
        ],
        out_specs=(pl.BlockSpec((tm, d), lambda i: (i, 0)),
                   pl.BlockSpec((tm, LANES), lambda i: (i, 0))),
        compiler_params=_params(("parallel",)),
        name="pre_norm",
    )(x2, g, wt)


def _cast_weight_tile(w_ref, w2_ref, w_scr, shift):
    if shift == 0:
        w_scr[...] = w_ref[...].astype(BF16)
    else:
        w_scr[...] = jnp.concatenate([w_ref[shift:, :], w2_ref[...]], axis=0).astype(BF16)


def _proj_kernel(h_ref, w_ref, w2_ref, o_ref, w_scr, *, n_aligned):
    j = pl.program_id(0)

    @pl.when(jnp.logical_and(pl.program_id(1) == 0, j < n_aligned))
    def _():
        _cast_weight_tile(w_ref, w2_ref, w_scr, 0)

    @pl.when(jnp.logical_and(pl.program_id(1) == 0, j >= n_aligned))
    def _():
        _cast_weight_tile(w_ref, w2_ref, w_scr, N_HEADS)

    o_ref[...] = _dot_nt(h_ref[...], w_scr[...]).astype(BF16)


def _proj_rot_kernel(h_ref, w_ref, cs_ref, sn_ref, o_ref, w_scr):
    @pl.when(pl.program_id(1) == 0)
    def _():
        _cast_weight_tile(w_ref, None, w_scr, 0)

    acc = _dot_nt(h_ref[...], w_scr[...])
    cs = cs_ref[...]
    sn = sn_ref[...]
    for hh in range(acc.shape[1] // HEAD_DIM):
        t = acc[:, hh * HEAD_DIM:(hh + 1) * HEAD_DIM]
        r = pltpu.roll(t, HEAD_DIM // 2, axis=1)
        o_ref[:, hh * HEAD_DIM:(hh + 1) * HEAD_DIM] = (t * cs + r * sn).astype(BF16)


def _proj_plain(h, wt):
    t = _tiles()
    tm, tn = t["proj_tm"], t["proj_tn"]
    rows, d = h.shape
    per_group = WIDTH // tn
    n_aligned = 4 * per_group
    n_gate = (2 * d) // tn
    gate0 = (6 * WIDTH) // tn

    def w_blk(j, i):
        aligned = jnp.where(j < 3 * per_group, j, j + 2 * per_group)
        return jnp.where(j < n_aligned, aligned, gate0 + (j - n_aligned)), 0

    def w2_blk(j, i):
        nxt = jnp.maximum(j - n_aligned, 0) + 1
        return (gate0 + nxt) * (tn // N_HEADS), 0

    return pl.pallas_call(
        functools.partial(_proj_kernel, n_aligned=n_aligned),
        out_shape=jax.ShapeDtypeStruct((rows, (n_aligned + n_gate) * tn), BF16),
        grid=(n_aligned + n_gate, rows // tm),
        in_specs=[
            pl.BlockSpec((tm, d), lambda j, i: (i, 0)),
            pl.BlockSpec((tn, d), w_blk),
            pl.BlockSpec((N_HEADS, d), w2_blk),
        ],
        out_specs=pl.BlockSpec((tm, tn), lambda j, i: (i, j)),
        scratch_shapes=[pltpu.VMEM((tn, d), BF16)],
        compiler_params=_params(("arbitrary", "arbitrary")),
        name="proj_plain",
    )(h, wt, wt)


def _proj_rot(h, wt, cs, sn, seq):
    t = _tiles()
    tm, tn = t["proj_tm"], t["proj_tn"]
    rows, d = h.shape
    first = (3 * WIDTH) // tn
    pos_blocks = seq // tm
    return pl.pallas_call(
        _proj_rot_kernel,
        out_shape=jax.ShapeDtypeStruct((rows, 2 * WIDTH), BF16),
        grid=((2 * WIDTH) // tn, rows // tm),
        in_specs=[
            pl.BlockSpec((tm, d), lambda j, i: (i, 0)),
            pl.BlockSpec((tn, d), lambda j, i: (first + j, 0)),
            pl.BlockSpec((tm, HEAD_DIM), lambda j, i: (i % pos_blocks, 0)),
            pl.BlockSpec((tm, HEAD_DIM), lambda j, i: (i % pos_blocks, 0)),
        ],
        out_specs=pl.BlockSpec((tm, tn), lambda j, i: (i, j)),
        scratch_shapes=[pltpu.VMEM((tn, d), BF16)],
        compiler_params=_params(("arbitrary", "arbitrary")),
        name="proj_rotary",
    )(h, wt, cs, sn)


def _forget_kernel(f_ref, b_ref, tri_ref, c_ref, *, chunk):
    z = f_ref[...] + b_ref[...]
    lf = jnp.minimum(z, 0.0) - jnp.log1p(jnp.exp(-jnp.abs(z)))
    lft = lf.T[:N_HEADS, :]
    tri = tri_ref[...]
    carry = jnp.zeros((N_HEADS, 1), F32)
    for n in range(lft.shape[1] // chunk):
        xc = lft[:, n * chunk:(n + 1) * chunk]
        hi = xc.astype(BF16)
        r1 = xc - hi.astype(F32)
        mid = r1.astype(BF16)
        lo = (r1 - mid.astype(F32)).astype(BF16)
        loc = (jnp.dot(hi, tri, preferred_element_type=F32)
               + jnp.dot(mid, tri, preferred_element_type=F32)
               + jnp.dot(lo, tri, preferred_element_type=F32)) + carry
        c_ref[0, :, n * chunk:(n + 1) * chunk] = loc
        carry = loc[:, chunk - 1:chunk]


def _forget_cumsum(f_pad, b_pad, batch, seq):
    chunk = _tiles()["cum_chunk"]
    r = lax.broadcasted_iota(jnp.int32, (chunk, chunk), 0)
    c = lax.broadcasted_iota(jnp.int32, (chunk, chunk), 1)
    tri = (r <= c).astype(BF16)
    return pl.pallas_call(
        functools.partial(_forget_kernel, chunk=chunk),
        out_shape=jax.ShapeDtypeStruct((batch, N_HEADS, seq), F32),
        grid=(batch,),
        in_specs=[
            pl.BlockSpec((seq, LANES), lambda b: (b, 0)),
            pl.BlockSpec((1, LANES), lambda b: (0, 0)),
            pl.BlockSpec((chunk, chunk), lambda b: (0, 0)),
        ],
        out_specs=pl.BlockSpec((1, N_HEADS, seq), lambda b: (b, 0, 0)),
        compiler_params=_params(("parallel",)),
        name="forget_cumsum",
    )(f_pad, b_pad, tri)


def _head(ref, hh, rows=slice(None)):
    return ref[0, rows, hh * HEAD_DIM:(hh + 1) * HEAD_DIM]


def _causal_softmax_pv(s, v, tq):
    length = s.shape[1]
    row = lax.broadcasted_iota(jnp.int32, (tq, tq), 0)
    col = lax.broadcasted_iota(jnp.int32, (tq, tq), 1)
    own = jnp.where(col <= row, s[:, length - tq:], -jnp.inf)
    s = own if length == tq else jnp.concatenate([s[:, :length - tq], own], axis=1)
    m = jnp.max(s, axis=-1, keepdims=True)
    p = jnp.exp2(s - m).astype(BF16)
    v1 = jnp.concatenate([v, jnp.ones(v.shape, v.dtype)], axis=1)
    acc = jnp.dot(p, v1, preferred_element_type=F32)
    return (acc[:, :HEAD_DIM] / acc[:, HEAD_DIM:]).astype(BF16)


def _for_each_query_tile(n_tiles, body):
    i = pl.program_id(2)
    for ii in range(n_tiles):
        pl.when(i == ii)(functools.partial(body, ii))


def _fox_kernel(q_ref, k_ref, v_ref, c_ref, o_ref, *, tq, heads, n_tiles):
    def scores(ii, hh):
        length = (ii + 1) * tq
        q = (_head(q_ref, hh).astype(F32) * (QK_SCALE * LOG2E)).astype(BF16)
        c = c_ref[0, hh]
        bias = (c[:, ii * tq:ii * tq + 1] - c[:, :length]) * LOG2E
        return _dot_nt(q, _head(k_ref, hh, slice(0, length))) + bias

    def body(ii):
        length = (ii + 1) * tq
        for h0 in range(0, heads, HEAD_PAIR):
            pair = range(h0, min(h0 + HEAD_PAIR, heads))
            ss = [scores(ii, hh) for hh in pair]
            for hh, s in zip(pair, ss):
                o_ref[0, :, hh * HEAD_DIM:(hh + 1) * HEAD_DIM] = _causal_softmax_pv(
                    s, _head(v_ref, hh, slice(0, length)), tq)

    _for_each_query_tile(n_tiles, body)


def _fox(p3, c4, batch, seq):
    t = _tiles()
    tq, heads = t["attn_tq"], t["attn_heads"]
    hw = heads * HEAD_DIM
    groups = N_HEADS // heads
    return pl.pallas_call(
        functools.partial(_fox_kernel, tq=tq, heads=heads, n_tiles=seq // tq),
        out_shape=jax.ShapeDtypeStruct((batch, seq, WIDTH), BF16),
        grid=(batch, groups, seq // tq),
        in_specs=[
            pl.BlockSpec((1, tq, hw), lambda b, g, i: (b, i, g)),
            pl.BlockSpec((1, seq, hw), lambda b, g, i: (b, 0, groups + g)),
            pl.BlockSpec((1, seq, hw), lambda b, g, i: (b, 0, 2 * groups + g)),
            pl.BlockSpec((1, heads, 1, seq), lambda b, g, i: (b, g, 0, 0)),
        ],
        out_specs=pl.BlockSpec((1, tq, hw), lambda b, g, i: (b, i, g)),
        compiler_params=_params(("parallel", "parallel", "arbitrary")),
        name="fox_attention",
    )(p3, p3, p3, c4)


def _moba_select_bias(q_raw, km, ii):
    km_hi, km_lo = _split_bf16(km)
    gate = _dot_nt(km_hi, q_raw) + _dot_nt(km_lo, q_raw)
    block = lax.broadcasted_iota(jnp.int32, gate.shape, 0)
    valid = block < ii
    g = jnp.where(valid, gate, -jnp.inf)
    rank = jnp.zeros(gate.shape, jnp.int32)
    for n in range(ii):
        gn = g[n:n + 1, :]
        ahead = jnp.logical_or(gn > g, jnp.logical_and(gn == g, n < block))
        rank = rank + ahead.astype(jnp.int32)
    visible = jnp.logical_or(jnp.logical_and(valid, rank < MOBA_TOPK), block == ii)
    bias_t = jnp.where(visible, 0.0, MASK_BIAS)
    pad = jnp.zeros((LANES - bias_t.shape[0], bias_t.shape[1]), F32)
    return jnp.concatenate([bias_t, pad], axis=0).T.astype(BF16)


def _moba_kernel(q_ref, k_ref, v_ref, oh_ref, o_ref, km_scr, *, n_blocks, heads):
    blk = MOBA_BLOCK

    @pl.when(pl.program_id(2) == 0)
    def _():
        for hh in range(heads):
            for n in range(n_blocks):
                kb = _head(k_ref, hh, slice(n * blk, (n + 1) * blk)).astype(F32)
                km_scr[hh, n:n + 1, :] = jnp.mean(kb, axis=0, keepdims=True)

    def scores(ii, hh):
        length = (ii + 1) * blk
        q_raw = _head(q_ref, hh)
        bias = _moba_select_bias(q_raw, km_scr[hh], ii)
        q = (q_raw.astype(F32) * (QK_SCALE * LOG2E)).astype(BF16)
        q_aug = jnp.concatenate([q, bias], axis=1)
        k_aug = jnp.concatenate([_head(k_ref, hh, slice(0, length)), oh_ref[:length, :]], axis=1)
        return _dot_nt(q_aug, k_aug)

    def body(ii):
        length = (ii + 1) * blk
        for h0 in range(0, heads, HEAD_PAIR):
            pair = range(h0, min(h0 + HEAD_PAIR, heads))
            ss = [scores(ii, hh) for hh in pair]
            for hh, s in zip(pair, ss):
                o_ref[0, :, hh * HEAD_DIM:(hh + 1) * HEAD_DIM] = _causal_softmax_pv(
                    s, _head(v_ref, hh, slice(0, length)), blk)

    _for_each_query_tile(n_blocks, body)


def _moba(qk3, p3, batch, seq):
    n_blocks = seq // MOBA_BLOCK
    assert seq % MOBA_BLOCK == 0 and n_blocks <= LANES
    heads = _tiles()["attn_heads"]
    hw = heads * HEAD_DIM
    groups = N_HEADS // heads
    key_block = lax.broadcasted_iota(jnp.int32, (seq, LANES), 0) // MOBA_BLOCK
    onehot = (key_block == lax.broadcasted_iota(jnp.int32, (seq, LANES), 1)).astype(BF16)
    return pl.pallas_call(
        functools.partial(_moba_kernel, n_blocks=n_blocks, heads=heads),
        out_shape=jax.ShapeDtypeStruct((batch, seq, WIDTH), BF16),
        grid=(batch, groups, n_blocks),
        in_specs=[
            pl.BlockSpec((1, MOBA_BLOCK, hw), lambda b, g, i: (b, i, g)),
            pl.BlockSpec((1, seq, hw), lambda b, g, i: (b, 0, groups + g)),
            pl.BlockSpec((1, seq, hw), lambda b, g, i: (b, 0, 3 * groups + g)),
            pl.BlockSpec((seq, LANES), lambda b, g, i: (0, 0)),
        ],
        out_specs=pl.BlockSpec((1, MOBA_BLOCK, hw), lambda b, g, i: (b, i, g)),
        scratch_shapes=[pltpu.VMEM((heads, n_blocks, HEAD_DIM), F32)],
        compiler_params=_params(("parallel", "parallel", "arbitrary")),
        name="moba_attention",
    )(qk3, qk3, p3, onehot)


def _mix_kernel(oa_ref, ob_ref, wa_ref, wb_ref, ga_ref, gb_ref, y_ref):
    ya = jnp.dot(oa_ref[...], wa_ref[...], preferred_element_type=F32)
    yb = jnp.dot(ob_ref[...], wb_ref[...], preferred_element_type=F32)
    ga = jax.nn.sigmoid(ga_ref[...].astype(F32))
    gb = jax.nn.sigmoid(gb_ref[...].astype(F32))
    y_ref[...] = (ga * ya + gb * yb).astype(BF16)


def _mix(oa, ob, wa, wb, p2):
    t = _tiles()
    tm, tn = t["mix_tm"], t["mix_tn"]
    rows, kdim = oa.shape
    d = wa.shape[1]
    ga_blk = (4 * WIDTH) // tn
    gb_blk = (4 * WIDTH + d) // tn
    return pl.pallas_call(
        _mix_kernel,
        out_shape=jax.ShapeDtypeStruct((rows, d), BF16),
        grid=(rows // tm, d // tn),
        in_specs=[
            pl.BlockSpec((tm, kdim), lambda i, j: (i, 0)),
            pl.BlockSpec((tm, kdim), lambda i, j: (i, 0)),
            pl.BlockSpec((kdim, tn), lambda i, j: (0, j)),
            pl.BlockSpec((kdim, tn), lambda i, j: (0, j)),
            pl.BlockSpec((tm, tn), lambda i, j: (i, ga_blk + j)),
            pl.BlockSpec((tm, tn), lambda i, j: (i, gb_blk + j)),
        ],
        out_specs=pl.BlockSpec((tm, tn), lambda i, j: (i, j)),
        compiler_params=_params(("parallel", "parallel")),
        name="gated_mix",
    )(oa, ob, wa, wb, p2, p2)


def _resid_kernel(x_ref, y_ref, w_ref, g_ref, x1_ref, h_ref):
    x1 = x_ref[...] + jnp.dot(y_ref[...], w_ref[...], preferred_element_type=F32)
    x1_ref[...] = x1
    h_ref[...] = _rms(x1, g_ref[...]).astype(BF16)


def _resid(x2, y, w_out, g):
    tm = _tiles()["resid_tm"]
    rows, d = x2.shape
    return pl.pallas_call(
        _resid_kernel,
        out_shape=(jax.ShapeDtypeStruct((rows, d), F32),
                   jax.ShapeDtypeStruct((rows, d), BF16)),
        grid=(rows // tm,),
        in_specs=[
            pl.BlockSpec((tm, d), lambda i: (i, 0)),
            pl.BlockSpec((tm, d), lambda i: (i, 0)),
            pl.BlockSpec((d, d), lambda i: (0, 0)),
            pl.BlockSpec((1, d), lambda i: (0, 0)),
        ],
        out_specs=(pl.BlockSpec((tm, d), lambda i: (i, 0)),
                   pl.BlockSpec((tm, d), lambda i: (i, 0))),
        compiler_params=_params(("parallel",)),
        name="out_residual",
    )(x2, y, w_out, g)


def _ffn_kernel(h_ref, x1_ref, wg_ref, wu_ref, wd_ref, gf_ref, o_ref, acc_scr):
    k = pl.program_id(1)

    @pl.when(k == 0)
    def _():
        acc_scr[...] = x1_ref[...]

    h = h_ref[...]
    gate = jnp.dot(h, wg_ref[...], preferred_element_type=F32)
    up = jnp.dot(h, wu_ref[...], preferred_element_type=F32)
    a = (gate * jax.nn.sigmoid(gate) * up).astype(BF16)
    acc_scr[...] += jnp.dot(a, wd_ref[...], preferred_element_type=F32)

    @pl.when(k == pl.num_programs(1) - 1)
    def _():
        o_ref[...] = _rms(acc_scr[...], gf_ref[...])


def _ffn(h2, x1, wgu, wd, gf):
    t = _tiles()
    tm, tf = t["ffn_tm"], t["ffn_tf"]
    rows, d = h2.shape
    dff = wd.shape[0]
    up_blk = dff // tf
    return pl.pallas_call(
        _ffn_kernel,
        out_shape=jax.ShapeDtypeStruct((rows, d), F32),
        grid=(rows // tm, dff // tf),
        in_specs=[
            pl.BlockSpec((tm, d), lambda i, k: (i, 0)),
            pl.BlockSpec((tm, d), lambda i, k: (i, 0)),
            pl.BlockSpec((d, tf), lambda i, k: (0, k)),
            pl.BlockSpec((d, tf), lambda i, k: (0, up_blk + k)),
            pl.BlockSpec((tf, d), lambda i, k: (k, 0)),
            pl.BlockSpec((1, d), lambda i, k: (0, 0)),
        ],
        out_specs=pl.BlockSpec((tm, d), lambda i, k: (i, 0)),
        scratch_shapes=[pltpu.VMEM((tm, d), F32)],
        compiler_params=_params(("parallel", "arbitrary")),
        name="swiglu_ffn",
    )(h2, x1, wgu, wgu, wd, gf)


def _rope_tables(seq):
    half = HEAD_DIM // 2
    inv_freq = ROPE_THETA ** (-jnp.arange(half, dtype=F32) / half)
    ang = jnp.arange(seq, dtype=F32)[:, None] * inv_freq[None, :]
    cos, sin = jnp.cos(ang), jnp.sin(ang)
    return jnp.concatenate([cos, cos], axis=-1), jnp.concatenate([-sin, sin], axis=-1)


def kernel(x, norm_mix, w_in, b_forget, w_o_fox, w_o_moba, w_out, norm_ffn, w_gate_up, w_down,
           norm_final):
    batch, seq, d = x.shape
    assert w_in.shape[0] == 1
    cs, sn = _rope_tables(seq)
    x2 = x.reshape(batch * seq, d)
    wt = w_in[0].T
    b_pad = jnp.pad(b_forget[0][None, :], ((0, 0), (0, LANES - N_HEADS)))

    h, f_pad = _norm(x2, norm_mix[0][None, :], wt)
    p2 = _proj_plain(h, wt)
    qk = _proj_rot(h, wt, cs, sn, seq)
    c = _forget_cumsum(f_pad, b_pad, batch, seq)
    p3 = p2.reshape(batch, seq, p2.shape[1])
    o_a = _fox(p3, c.reshape(batch, N_HEADS, 1, seq), batch, seq)
    o_b = _moba(qk.reshape(batch, seq, qk.shape[1]), p3, batch, seq)
    y = _mix(o_a.reshape(batch * seq, WIDTH), o_b.reshape(batch * seq, WIDTH),
             w_o_fox[0].astype(BF16), w_o_moba[0].astype(BF16), p2)
    x1, h2 = _resid(x2, y, w_out[0].astype(BF16), norm_ffn[0][None, :])
    out = _ffn(h2, x1, w_gate_up[0].astype(BF16), w_down[0].astype(BF16), norm_final[None, :])
    return out.reshape(batch, seq, d)
```

```python
import functools
import math
from typing import NamedTuple

import jax
import jax.numpy as jnp
from jax import lax
from jax.experimental import pallas as pl
from jax.experimental.pallas import tpu as pltpu

F32 = jnp.float32
BF16 = jnp.bfloat16

HEAD_DIM = 128
N_HEADS = 8
WIDTH = N_HEADS * HEAD_DIM
MOBA_BLOCK = 256
MOBA_TOPK = 3
ROPE_THETA = 10000.0
RMS_EPS = 1e-6
LOG2E = math.log2(math.e)
QK_SCALE = HEAD_DIM ** -0.5
LANES = 128
BF16_SUBLANES = 16
MASK_BIAS = -1e30
HEAD_PAIR = 4
VMEM_LIMIT = 56 * 1024 * 1024


def _tiles():
    return dict(
        norm_tm=512, proj_tm=1024, proj_tn=1024,
        attn_tq=256, attn_heads=4,
        mix_tm=1024, mix_tn=1024,
        resid_tm=512,
        ffn_tm=512, ffn_tf=512,
        cum_chunk=256,
    )


def _params(sem):
    return pltpu.CompilerParams(dimension_semantics=sem, vmem_limit_bytes=VMEM_LIMIT)


def _rms(x, g):
    ms = jnp.mean(x * x, axis=-1, keepdims=True)
    return x * lax.rsqrt(ms + RMS_EPS) * g


def _dot_nt(a, b):
    return lax.dot_general(a, b, (((1,), (1,)), ((), ())), preferred_element_type=F32)


class _Cast(NamedTuple):
    src: jax.Array
    cols: int
    col_block: int = 0


def _call_with_casts(kernel_fn, casts, args, *, grid, in_specs, out_specs, out_shape, **kw):
    steps = math.prod(grid)
    strides = [math.prod(grid[a + 1:]) for a in range(len(grid))]

    def step_of(*g):
        return sum(gi * st for gi, st in zip(g, strides))

    n_in, n_out, n_side = len(in_specs), len(out_shape), len(casts)
    side_in, side_out, side_shape = [], [], []
    for c in casts:
        rows = c.src.shape[0] // steps
        assert rows * steps == c.src.shape[0] and rows % BF16_SUBLANES == 0
        side_in.append(pl.BlockSpec((rows, c.cols), lambda *g, c=c: (step_of(*g), c.col_block)))
        side_out.append(pl.BlockSpec((rows, c.cols), lambda *g: (step_of(*g), 0)))
        side_shape.append(jax.ShapeDtypeStruct((c.src.shape[0], c.cols), BF16))

    def body(*refs):
        src = refs[n_in:n_in + n_side]
        dst = refs[n_in + n_side + n_out:n_in + 2 * n_side + n_out]
        for s_ref, d_ref in zip(src, dst):
            d_ref[...] = s_ref[...].astype(BF16)
        kernel_fn(*refs[:n_in], *refs[n_in + n_side:n_in + n_side + n_out],
                  *refs[n_in + 2 * n_side + n_out:])

    res = pl.pallas_call(
        body, grid=grid, in_specs=[*in_specs, *side_in], out_specs=(*out_specs, *side_out),
        out_shape=(*out_shape, *side_shape), **kw)(*args, *[c.src for c in casts])
    return res[:n_out], res[n_out:]


def _split_bf16(v):
    hi = v.astype(BF16)
    return hi, (v - hi.astype(F32)).astype(BF16)


def _norm_kernel(x_ref, g_ref, wf_ref, h_ref, f_ref):
    h = _rms(x_ref[...], g_ref[...])
    h_hi, h_lo = _split_bf16(h)
    h_ref[...] = h_hi
    w_hi, w_lo = _split_bf16(wf_ref[...])
    f_ref[...] = _dot_nt(h_hi, w_hi) + _dot_nt(h_hi, w_lo) + _dot_nt(h_lo, w_hi)


def _norm(x2, g, wt):
    tm = _tiles()["norm_tm"]
    rows, d = x2.shape
    f_blk = (6 * WIDTH) // LANES
    return pl.pallas_call(
        _norm_kernel,
        out_shape=(jax.ShapeDtypeStruct((rows, d), BF16),
                   jax.ShapeDtypeStruct((rows, LANES), F32)),
        grid=(rows // tm,),
        in_specs=[
            pl.BlockSpec((tm, d), lambda i: (i, 0)),
            pl.BlockSpec((1, d), lambda i: (0, 0)),
            pl.BlockSpec((LANES, d), lambda i: (f_blk, 0)),
        ],
        out_specs=(pl.BlockSpec((tm, d), lambda i: (i, 0)),
                   pl.BlockSpec((tm, LANES), lambda i: (i, 0))),
        compiler_params=_params(("parallel",)),
        name="pre_norm",
    )(x2, g, wt)


def _cast_weight_tile(w_ref, w2_ref, w_scr, shift):
    if shift == 0:
        w_scr[...] = w_ref[...].astype(BF16)
    else:
        w_scr[...] = jnp.concatenate([w_ref[shift:, :], w2_ref[...]], axis=0).astype(BF16)


def _proj_kernel(h_ref, w_ref, w2_ref, o_ref, w_scr, *, n_aligned):
    j = pl.program_id(0)

    @pl.when(jnp.logical_and(pl.program_id(1) == 0, j < n_aligned))
    def _():
        _cast_weight_tile(w_ref, w2_ref, w_scr, 0)

    @pl.when(jnp.logical_and(pl.program_id(1) == 0, j >= n_aligned))
    def _():
        _cast_weight_tile(w_ref, w2_ref, w_scr, N_HEADS)

    o_ref[...] = _dot_nt(h_ref[...], w_scr[...]).astype(BF16)


def _proj_rot_kernel(h_ref, w_ref, cs_ref, sn_ref, o_ref, w_scr):
    @pl.when(pl.program_id(1) == 0)
    def _():
        _cast_weight_tile(w_ref, None, w_scr, 0)

    acc = _dot_nt(h_ref[...], w_scr[...])
    cs = cs_ref[...]
    sn = sn_ref[...]
    for hh in range(acc.shape[1] // HEAD_DIM):
        t = acc[:, hh * HEAD_DIM:(hh + 1) * HEAD_DIM]
        r = pltpu.roll(t, HEAD_DIM // 2, axis=1)
        o_ref[:, hh * HEAD_DIM:(hh + 1) * HEAD_DIM] = (t * cs + r * sn).astype(BF16)


def _proj_plain(h, wt, casts):
    t = _tiles()
    tm, tn = t["proj_tm"], t["proj_tn"]
    rows, d = h.shape
    per_group = WIDTH // tn
    n_aligned = 4 * per_group
    n_gate = (2 * d) // tn
    gate0 = (6 * WIDTH) // tn

    def w_blk(j, i):
        aligned = jnp.where(j < 3 * per_group, j, j + 2 * per_group)
        return jnp.where(j < n_aligned, aligned, gate0 + (j - n_aligned)), 0

    def w2_blk(j, i):
        nxt = jnp.maximum(j - n_aligned, 0) + 1
        return (gate0 + nxt) * (tn // N_HEADS), 0

    (p2,), cast = _call_with_casts(
        functools.partial(_proj_kernel, n_aligned=n_aligned), casts, (h, wt, wt),
        out_shape=(jax.ShapeDtypeStruct((rows, (n_aligned + n_gate) * tn), BF16),),
        grid=(n_aligned + n_gate, rows // tm),
        in_specs=[
            pl.BlockSpec((tm, d), lambda j, i: (i, 0)),
            pl.BlockSpec((tn, d), w_blk),
            pl.BlockSpec((N_HEADS, d), w2_blk),
        ],
        out_specs=(pl.BlockSpec((tm, tn), lambda j, i: (i, j)),),
        scratch_shapes=[pltpu.VMEM((tn, d), BF16)],
        compiler_params=_params(("arbitrary", "arbitrary")),
        name="proj_plain",
    )
    return p2, cast


def _proj_rot(h, wt, cs, sn, seq, casts):
    t = _tiles()
    tm, tn = t["proj_tm"], t["proj_tn"]
    rows, d = h.shape
    first = (3 * WIDTH) // tn
    pos_blocks = seq // tm
    (qk,), cast = _call_with_casts(
        _proj_rot_kernel, casts, (h, wt, cs, sn),
        out_shape=(jax.ShapeDtypeStruct((rows, 2 * WIDTH), BF16),),
        grid=((2 * WIDTH) // tn, rows // tm),
        in_specs=[
            pl.BlockSpec((tm, d), lambda j, i: (i, 0)),
            pl.BlockSpec((tn, d), lambda j, i: (first + j, 0)),
            pl.BlockSpec((tm, HEAD_DIM), lambda j, i: (i % pos_blocks, 0)),
            pl.BlockSpec((tm, HEAD_DIM), lambda j, i: (i % pos_blocks, 0)),
        ],
        out_specs=(pl.BlockSpec((tm, tn), lambda j, i: (i, j)),),
        scratch_shapes=[pltpu.VMEM((tn, d), BF16)],
        compiler_params=_params(("arbitrary", "arbitrary")),
        name="proj_rotary",
    )
    return qk, cast


def _forget_kernel(f_ref, b_ref, tri_ref, c_ref, *, chunk):
    z = f_ref[...] + b_ref[...]
    lf = jnp.minimum(z, 0.0) - jnp.log1p(jnp.exp(-jnp.abs(z)))
    lft = lf.T[:N_HEADS, :]
    tri = tri_ref[...]
    carry = jnp.zeros((N_HEADS, 1), F32)
    for n in range(lft.shape[1] // chunk):
        xc = lft[:, n * chunk:(n + 1) * chunk]
        hi = xc.astype(BF16)
        r1 = xc - hi.astype(F32)
        mid = r1.astype(BF16)
        lo = (r1 - mid.astype(F32)).astype(BF16)
        loc = (jnp.dot(hi, tri, preferred_element_type=F32)
               + jnp.dot(mid, tri, preferred_element_type=F32)
               + jnp.dot(lo, tri, preferred_element_type=F32)) + carry
        c_ref[0, :, n * chunk:(n + 1) * chunk] = loc
        carry = loc[:, chunk - 1:chunk]


def _forget_cumsum(f_pad, b_pad, batch, seq):
    chunk = _tiles()["cum_chunk"]
    r = lax.broadcasted_iota(jnp.int32, (chunk, chunk), 0)
    c = lax.broadcasted_iota(jnp.int32, (chunk, chunk), 1)
    tri = (r <= c).astype(BF16)
    return pl.pallas_call(
        functools.partial(_forget_kernel, chunk=chunk),
        out_shape=jax.ShapeDtypeStruct((batch, N_HEADS, seq), F32),
        grid=(batch,),
        in_specs=[
            pl.BlockSpec((seq, LANES), lambda b: (b, 0)),
            pl.BlockSpec((1, LANES), lambda b: (0, 0)),
            pl.BlockSpec((chunk, chunk), lambda b: (0, 0)),
        ],
        out_specs=pl.BlockSpec((1, N_HEADS, seq), lambda b: (b, 0, 0)),
        compiler_params=_params(("parallel",)),
        name="forget_cumsum",
    )(f_pad, b_pad, tri)


def _head(ref, hh, rows=slice(None)):
    return ref[0, rows, hh * HEAD_DIM:(hh + 1) * HEAD_DIM]


def _causal_softmax_pv(s, v, tq):
    length = s.shape[1]
    row = lax.broadcasted_iota(jnp.int32, (tq, tq), 0)
    col = lax.broadcasted_iota(jnp.int32, (tq, tq), 1)
    own = jnp.where(col <= row, s[:, length - tq:], -jnp.inf)
    s = own if length == tq else jnp.concatenate([s[:, :length - tq], own], axis=1)
    m = jnp.max(s, axis=-1, keepdims=True)
    p = jnp.exp2(s - m).astype(BF16)
    v1 = jnp.concatenate([v, jnp.ones(v.shape, v.dtype)], axis=1)
    acc = jnp.dot(p, v1, preferred_element_type=F32)
    return (acc[:, :HEAD_DIM] / acc[:, HEAD_DIM:]).astype(BF16)


def _for_each_query_tile(n_tiles, body):
    i = pl.program_id(2)
    for ii in range(n_tiles):
        pl.when(i == ii)(functools.partial(body, ii))


def _fox_kernel(q_ref, k_ref, v_ref, c_ref, o_ref, *, tq, heads, n_tiles):
    def scores(ii, hh):
        length = (ii + 1) * tq
        q = (_head(q_ref, hh).astype(F32) * (QK_SCALE * LOG2E)).astype(BF16)
        c = c_ref[0, hh]
        bias = (c[:, ii * tq:ii * tq + 1] - c[:, :length]) * LOG2E
        return _dot_nt(q, _head(k_ref, hh, slice(0, length))) + bias

    def body(ii):
        length = (ii + 1) * tq
        for h0 in range(0, heads, HEAD_PAIR):
            pair = range(h0, min(h0 + HEAD_PAIR, heads))
            ss = [scores(ii, hh) for hh in pair]
            for hh, s in zip(pair, ss):
                o_ref[0, :, hh * HEAD_DIM:(hh + 1) * HEAD_DIM] = _causal_softmax_pv(
                    s, _head(v_ref, hh, slice(0, length)), tq)

    _for_each_query_tile(n_tiles, body)


def _fox(p3, c4, batch, seq, casts):
    t = _tiles()
    tq, heads = t["attn_tq"], t["attn_heads"]
    hw = heads * HEAD_DIM
    groups = N_HEADS // heads
    (o,), cast = _call_with_casts(
        functools.partial(_fox_kernel, tq=tq, heads=heads, n_tiles=seq // tq), casts,
        (p3, p3, p3, c4),
        out_shape=(jax.ShapeDtypeStruct((batch, seq, WIDTH), BF16),),
        grid=(batch, groups, seq // tq),
        in_specs=[
            pl.BlockSpec((1, tq, hw), lambda b, g, i: (b, i, g)),
            pl.BlockSpec((1, seq, hw), lambda b, g, i: (b, 0, groups + g)),
            pl.BlockSpec((1, seq, hw), lambda b, g, i: (b, 0, 2 * groups + g)),
            pl.BlockSpec((1, heads, 1, seq), lambda b, g, i: (b, g, 0, 0)),
        ],
        out_specs=(pl.BlockSpec((1, tq, hw), lambda b, g, i: (b, i, g)),),
        compiler_params=_params(("arbitrary", "arbitrary", "arbitrary")),
        name="fox_attention",
    )
    return o, cast


def _moba_select_bias(q_raw, km, ii):
    km_hi, km_lo = _split_bf16(km)
    gate = _dot_nt(km_hi, q_raw) + _dot_nt(km_lo, q_raw)
    block = lax.broadcasted_iota(jnp.int32, gate.shape, 0)
    valid = block < ii
    g = jnp.where(valid, gate, -jnp.inf)
    rank = jnp.zeros(gate.shape, jnp.int32)
    for n in range(ii):
        gn = g[n:n + 1, :]
        ahead = jnp.logical_or(gn > g, jnp.logical_and(gn == g, n < block))
        rank = rank + ahead.astype(jnp.int32)
    visible = jnp.logical_or(jnp.logical_and(valid, rank < MOBA_TOPK), block == ii)
    bias_t = jnp.where(visible, 0.0, MASK_BIAS)
    pad = jnp.zeros((LANES - bias_t.shape[0], bias_t.shape[1]), F32)
    return jnp.concatenate([bias_t, pad], axis=0).T.astype(BF16)


def _moba_kernel(q_ref, k_ref, v_ref, oh_ref, o_ref, km_scr, *, n_blocks, heads):
    blk = MOBA_BLOCK

    @pl.when(pl.program_id(2) == 0)
    def _():
        for hh in range(heads):
            for n in range(n_blocks):
                kb = _head(k_ref, hh, slice(n * blk, (n + 1) * blk)).astype(F32)
                km_scr[hh, n:n + 1, :] = jnp.mean(kb, axis=0, keepdims=True)

    def scores(ii, hh):
        length = (ii + 1) * blk
        q_raw = _head(q_ref, hh)
        bias = _moba_select_bias(q_raw, km_scr[hh], ii)
        q = (q_raw.astype(F32) * (QK_SCALE * LOG2E)).astype(BF16)
        q_aug = jnp.concatenate([q, bias], axis=1)
        k_aug = jnp.concatenate([_head(k_ref, hh, slice(0, length)), oh_ref[:length, :]], axis=1)
        return _dot_nt(q_aug, k_aug)

    def body(ii):
        length = (ii + 1) * blk
        for h0 in range(0, heads, HEAD_PAIR):
            pair = range(h0, min(h0 + HEAD_PAIR, heads))
            ss = [scores(ii, hh) for hh in pair]
            for hh, s in zip(pair, ss):
                o_ref[0, :, hh * HEAD_DIM:(hh + 1) * HEAD_DIM] = _causal_softmax_pv(
                    s, _head(v_ref, hh, slice(0, length)), blk)

    _for_each_query_tile(n_blocks, body)


def _moba(qk3, p3, batch, seq, casts):
    n_blocks = seq // MOBA_BLOCK
    assert seq % MOBA_BLOCK == 0 and n_blocks <= LANES
    heads = _tiles()["attn_heads"]
    hw = heads * HEAD_DIM
    groups = N_HEADS // heads
    key_block = lax.broadcasted_iota(jnp.int32, (seq, LANES), 0) // MOBA_BLOCK
    onehot = (key_block == lax.broadcasted_iota(jnp.int32, (seq, LANES), 1)).astype(BF16)
    (o,), cast = _call_with_casts(
        functools.partial(_moba_kernel, n_blocks=n_blocks, heads=heads), casts,
        (qk3, qk3, p3, onehot),
        out_shape=(jax.ShapeDtypeStruct((batch, seq, WIDTH), BF16),),
        grid=(batch, groups, n_blocks),
        in_specs=[
            pl.BlockSpec((1, MOBA_BLOCK, hw), lambda b, g, i: (b, i, g)),
            pl.BlockSpec((1, seq, hw), lambda b, g, i: (b, 0, groups + g)),
            pl.BlockSpec((1, seq, hw), lambda b, g, i: (b, 0, 3 * groups + g)),
            pl.BlockSpec((seq, LANES), lambda b, g, i: (0, 0)),
        ],
        out_specs=(pl.BlockSpec((1, MOBA_BLOCK, hw), lambda b, g, i: (b, i, g)),),
        scratch_shapes=[pltpu.VMEM((heads, n_blocks, HEAD_DIM), F32)],
        compiler_params=_params(("arbitrary", "arbitrary", "arbitrary")),
        name="moba_attention",
    )
    return o, cast


def _mix_kernel(oa_ref, ob_ref, wa_ref, wb_ref, ga_ref, gb_ref, y_ref):
    ya = jnp.dot(oa_ref[...], wa_ref[...], preferred_element_type=F32)
    yb = jnp.dot(ob_ref[...], wb_ref[...], preferred_element_type=F32)
    ga = jax.nn.sigmoid(ga_ref[...].astype(F32))
    gb = jax.nn.sigmoid(gb_ref[...].astype(F32))
    y_ref[...] = (ga * ya + gb * yb).astype(BF16)


def _mix(oa, ob, wa, wb, p2):
    t = _tiles()
    tm, tn = t["mix_tm"], t["mix_tn"]
    rows, kdim = oa.shape
    d = wa.shape[1]
    ga_blk = (4 * WIDTH) // tn
    gb_blk = (4 * WIDTH + d) // tn
    return pl.pallas_call(
        _mix_kernel,
        out_shape=jax.ShapeDtypeStruct((rows, d), BF16),
        grid=(rows // tm, d // tn),
        in_specs=[
            pl.BlockSpec((tm, kdim), lambda i, j: (i, 0)),
            pl.BlockSpec((tm, kdim), lambda i, j: (i, 0)),
            pl.BlockSpec((kdim, tn), lambda i, j: (0, j)),
            pl.BlockSpec((kdim, tn), lambda i, j: (0, j)),
            pl.BlockSpec((tm, tn), lambda i, j: (i, ga_blk + j)),
            pl.BlockSpec((tm, tn), lambda i, j: (i, gb_blk + j)),
        ],
        out_specs=pl.BlockSpec((tm, tn), lambda i, j: (i, j)),
        compiler_params=_params(("parallel", "parallel")),
        name="gated_mix",
    )(oa, ob, wa, wb, p2, p2)


def _resid_kernel(x_ref, y_ref, w_ref, g_ref, x1_ref, h_ref):
    x1 = x_ref[...] + jnp.dot(y_ref[...], w_ref[...], preferred_element_type=F32)
    x1_ref[...] = x1
    h_ref[...] = _rms(x1, g_ref[...]).astype(BF16)


def _resid(x2, y, w_out, g):
    tm = _tiles()["resid_tm"]
    rows, d = x2.shape
    return pl.pallas_call(
        _resid_kernel,
        out_shape=(jax.ShapeDtypeStruct((rows, d), F32),
                   jax.ShapeDtypeStruct((rows, d), BF16)),
        grid=(rows // tm,),
        in_specs=[
            pl.BlockSpec((tm, d), lambda i: (i, 0)),
            pl.BlockSpec((tm, d), lambda i: (i, 0)),
            pl.BlockSpec((d, d), lambda i: (0, 0)),
            pl.BlockSpec((1, d), lambda i: (0, 0)),
        ],
        out_specs=(pl.BlockSpec((tm, d), lambda i: (i, 0)),
                   pl.BlockSpec((tm, d), lambda i: (i, 0))),
        compiler_params=_params(("parallel",)),
        name="out_residual",
    )(x2, y, w_out, g)


def _ffn_kernel(h_ref, x1_ref, wg_ref, wu_ref, wd_ref, gf_ref, o_ref, acc_scr):
    k = pl.program_id(1)

    @pl.when(k == 0)
    def _():
        acc_scr[...] = x1_ref[...]

    h = h_ref[...]
    gate = jnp.dot(h, wg_ref[...], preferred_element_type=F32)
    up = jnp.dot(h, wu_ref[...], preferred_element_type=F32)
    a = (gate * jax.nn.sigmoid(gate) * up).astype(BF16)
    acc_scr[...] += jnp.dot(a, wd_ref[...], preferred_element_type=F32)

    @pl.when(k == pl.num_programs(1) - 1)
    def _():
        o_ref[...] = _rms(acc_scr[...], gf_ref[...])


def _ffn(h2, x1, wg, wu, wd, gf):
    t = _tiles()
    tm, tf = t["ffn_tm"], t["ffn_tf"]
    rows, d = h2.shape
    dff = wd.shape[0]
    return pl.pallas_call(
        _ffn_kernel,
        out_shape=jax.ShapeDtypeStruct((rows, d), F32),
        grid=(rows // tm, dff // tf),
        in_specs=[
            pl.BlockSpec((tm, d), lambda i, k: (i, 0)),
            pl.BlockSpec((tm, d), lambda i, k: (i, 0)),
            pl.BlockSpec((d, tf), lambda i, k: (0, k)),
            pl.BlockSpec((d, tf), lambda i, k: (0, k)),
            pl.BlockSpec((tf, d), lambda i, k: (k, 0)),
            pl.BlockSpec((1, d), lambda i, k: (0, 0)),
        ],
        out_specs=pl.BlockSpec((tm, d), lambda i, k: (i, 0)),
        scratch_shapes=[pltpu.VMEM((tm, d), F32)],
        compiler_params=_params(("parallel", "arbitrary")),
        name="swiglu_ffn",
    )(h2, x1, wg, wu, wd, gf)


def _rope_tables(seq):
    half = HEAD_DIM // 2
    inv_freq = ROPE_THETA ** (-jnp.arange(half, dtype=F32) / half)
    ang = jnp.arange(seq, dtype=F32)[:, None] * inv_freq[None, :]
    cos, sin = jnp.cos(ang), jnp.sin(ang)
    return jnp.concatenate([cos, cos], axis=-1), jnp.concatenate([-sin, sin], axis=-1)


def kernel(x, norm_mix, w_in, b_forget, w_o_fox, w_o_moba, w_out, norm_ffn, w_gate_up, w_down,
           norm_final):
    batch, seq, d = x.shape
    assert w_in.shape[0] == 1
    cs, sn = _rope_tables(seq)
    x2 = x.reshape(batch * seq, d)
    wt = w_in[0].T
    b_pad = jnp.pad(b_forget[0][None, :], ((0, 0), (0, LANES - N_HEADS)))

    dff = w_down.shape[1]
    h, f_pad = _norm(x2, norm_mix[0][None, :], wt)
    p2, (wo_a, wo_b, w_res) = _proj_plain(
        h, wt, [_Cast(w_o_fox[0], d), _Cast(w_o_moba[0], d), _Cast(w_out[0], d)])
    qk, (wd,) = _proj_rot(h, wt, cs, sn, seq, [_Cast(w_down[0], d)])
    c = _forget_cumsum(f_pad, b_pad, batch, seq)
    p3 = p2.reshape(batch, seq, p2.shape[1])
    o_a, (wg,) = _fox(p3, c.reshape(batch, N_HEADS, 1, seq), batch, seq,
                      [_Cast(w_gate_up[0], dff, 0)])
    o_b, (wu,) = _moba(qk.reshape(batch, seq, qk.shape[1]), p3, batch, seq,
                       [_Cast(w_gate_up[0], dff, 1)])
    y = _mix(o_a.reshape(batch * seq, WIDTH), o_b.reshape(batch * seq, WIDTH), wo_a, wo_b, p2)
    x1, h2 = _resid(x2, y, w_res, norm_ffn[0][None, :])
    out = _ffn(h2, x1, wg, wu, wd, norm_final[None, :])
    return out.reshape(batch, seq, d)
```

```python
import functools
import math
from typing import NamedTuple

import jax
import jax.numpy as jnp
from jax import lax
from jax.experimental import pallas as pl
from jax.experimental.pallas import tpu as pltpu

F32 = jnp.float32
BF16 = jnp.bfloat16

HEAD_DIM = 128
N_HEADS = 8
WIDTH = N_HEADS * HEAD_DIM
MOBA_BLOCK = 256
MOBA_TOPK = 3
ROPE_THETA = 10000.0
RMS_EPS = 1e-6
LOG2E = math.log2(math.e)
QK_SCALE = HEAD_DIM ** -0.5
LANES = 128
BF16_SUBLANES = 16
MASK_BIAS = -1e30
HEAD_PAIR = 4
VMEM_LIMIT = 56 * 1024 * 1024


def _tiles():
    return dict(
        norm_tm=512, proj_tm=1024, proj_tn=1024,
        attn_tq=256, attn_heads=4,
        mix_tm=1024, mix_tn=1024,
        resid_tm=512,
        ffn_tm=1024, ffn_tf=256,
        cum_chunk=256,
    )


def _params(sem):
    return pltpu.CompilerParams(dimension_semantics=sem, vmem_limit_bytes=VMEM_LIMIT)


def _rms(x, g):
    ms = jnp.mean(x * x, axis=-1, keepdims=True)
    return x * lax.rsqrt(ms + RMS_EPS) * g


def _dot_nt(a, b):
    return lax.dot_general(a, b, (((1,), (1,)), ((), ())), preferred_element_type=F32)


class _Cast(NamedTuple):
    src: jax.Array
    cols: int
    col_block: int = 0


def _call_with_casts(kernel_fn, casts, args, *, grid, in_specs, out_specs, out_shape, **kw):
    steps = math.prod(grid)
    strides = [math.prod(grid[a + 1:]) for a in range(len(grid))]

    def step_of(*g):
        return sum(gi * st for gi, st in zip(g, strides))

    n_in, n_out, n_side = len(in_specs), len(out_shape), len(casts)
    side_in, side_out, side_shape = [], [], []
    for c in casts:
        rows = c.src.shape[0] // steps
        assert rows * steps == c.src.shape[0] and rows % BF16_SUBLANES == 0
        side_in.append(pl.BlockSpec((rows, c.cols), lambda *g, c=c: (step_of(*g), c.col_block)))
        side_out.append(pl.BlockSpec((rows, c.cols), lambda *g: (step_of(*g), 0)))
        side_shape.append(jax.ShapeDtypeStruct((c.src.shape[0], c.cols), BF16))

    def body(*refs):
        src = refs[n_in:n_in + n_side]
        dst = refs[n_in + n_side + n_out:n_in + 2 * n_side + n_out]
        for s_ref, d_ref in zip(src, dst):
            d_ref[...] = s_ref[...].astype(BF16)
        kernel_fn(*refs[:n_in], *refs[n_in + n_side:n_in + n_side + n_out],
                  *refs[n_in + 2 * n_side + n_out:])

    res = pl.pallas_call(
        body, grid=grid, in_specs=[*in_specs, *side_in], out_specs=(*out_specs, *side_out),
        out_shape=(*out_shape, *side_shape), **kw)(*args, *[c.src for c in casts])
    return res[:n_out], res[n_out:]


def _split_bf16(v):
    hi = v.astype(BF16)
    return hi, (v - hi.astype(F32)).astype(BF16)


def _norm_kernel(x_ref, g_ref, wf_ref, h_ref, f_ref):
    h = _rms(x_ref[...], g_ref[...])
    h_hi, h_lo = _split_bf16(h)
    h_ref[...] = h_hi
    w_hi, w_lo = _split_bf16(wf_ref[...])
    f_ref[...] = _dot_nt(h_hi, w_hi) + _dot_nt(h_hi, w_lo) + _dot_nt(h_lo, w_hi)


def _norm(x2, g, wt):
    tm = _tiles()["norm_tm"]
    rows, d = x2.shape
    f_blk = (6 * WIDTH) // LANES
    return pl.pallas_call(
        _norm_kernel,
        out_shape=(jax.ShapeDtypeStruct((rows, d), BF16),
                   jax.ShapeDtypeStruct((rows, LANES), F32)),
        grid=(rows // tm,),
        in_specs=[
            pl.BlockSpec((tm, d), lambda i: (i, 0)),
            pl.BlockSpec((1, d), lambda i: (0, 0)),
            pl.BlockSpec((LANES, d), lambda i: (f_blk, 0)),
        ],
        out_specs=(pl.BlockSpec((tm, d), lambda i: (i, 0)),
                   pl.BlockSpec((tm, LANES), lambda i: (i, 0))),
        compiler_params=_params(("parallel",)),
        name="pre_norm",
    )(x2, g, wt)


def _cast_weight_tile(w_ref, w2_ref, w_scr, shift):
    if shift == 0:
        w_scr[...] = w_ref[...].astype(BF16)
    else:
        w_scr[...] = jnp.concatenate([w_ref[shift:, :], w2_ref[...]], axis=0).astype(BF16)


def _proj_kernel(h_ref, w_ref, w2_ref, o_ref, w_scr, *, n_aligned):
    j = pl.program_id(0)

    @pl.when(jnp.logical_and(pl.program_id(1) == 0, j < n_aligned))
    def _():
        _cast_weight_tile(w_ref, w2_ref, w_scr, 0)

    @pl.when(jnp.logical_and(pl.program_id(1) == 0, j >= n_aligned))
    def _():
        _cast_weight_tile(w_ref, w2_ref, w_scr, N_HEADS)

    o_ref[...] = _dot_nt(h_ref[...], w_scr[...]).astype(BF16)


def _proj_rot_kernel(h_ref, w_ref, cs_ref, sn_ref, o_ref, w_scr):
    @pl.when(pl.program_id(1) == 0)
    def _():
        _cast_weight_tile(w_ref, None, w_scr, 0)

    acc = _dot_nt(h_ref[...], w_scr[...])
    cs = cs_ref[...]
    sn = sn_ref[...]
    for hh in range(acc.shape[1] // HEAD_DIM):
        t = acc[:, hh * HEAD_DIM:(hh + 1) * HEAD_DIM]
        r = pltpu.roll(t, HEAD_DIM // 2, axis=1)
        o_ref[:, hh * HEAD_DIM:(hh + 1) * HEAD_DIM] = (t * cs + r * sn).astype(BF16)


def _proj_plain(h, wt, casts):
    t = _tiles()
    tm, tn = t["proj_tm"], t["proj_tn"]
    rows, d = h.shape
    per_group = WIDTH // tn
    n_aligned = 4 * per_group
    n_gate = (2 * d) // tn
    gate0 = (6 * WIDTH) // tn

    def w_blk(j, i):
        aligned = jnp.where(j < 3 * per_group, j, j + 2 * per_group)
        return jnp.where(j < n_aligned, aligned, gate0 + (j - n_aligned)), 0

    def w2_blk(j, i):
        nxt = jnp.maximum(j - n_aligned, 0) + 1
        return (gate0 + nxt) * (tn // N_HEADS), 0

    (p2,), cast = _call_with_casts(
        functools.partial(_proj_kernel, n_aligned=n_aligned), casts, (h, wt, wt),
        out_shape=(jax.ShapeDtypeStruct((rows, (n_aligned + n_gate) * tn), BF16),),
        grid=(n_aligned + n_gate, rows // tm),
        in_specs=[
            pl.BlockSpec((tm, d), lambda j, i: (i, 0)),
            pl.BlockSpec((tn, d), w_blk),
            pl.BlockSpec((N_HEADS, d), w2_blk),
        ],
        out_specs=(pl.BlockSpec((tm, tn), lambda j, i: (i, j)),),
        scratch_shapes=[pltpu.VMEM((tn, d), BF16)],
        compiler_params=_params(("arbitrary", "arbitrary")),
        name="proj_plain",
    )
    return p2, cast


def _proj_rot(h, wt, cs, sn, seq, casts):
    t = _tiles()
    tm, tn = t["proj_tm"], t["proj_tn"]
    rows, d = h.shape
    first = (3 * WIDTH) // tn
    pos_blocks = seq // tm
    (qk,), cast = _call_with_casts(
        _proj_rot_kernel, casts, (h, wt, cs, sn),
        out_shape=(jax.ShapeDtypeStruct((rows, 2 * WIDTH), BF16),),
        grid=((2 * WIDTH) // tn, rows // tm),
        in_specs=[
            pl.BlockSpec((tm, d), lambda j, i: (i, 0)),
            pl.BlockSpec((tn, d), lambda j, i: (first + j, 0)),
            pl.BlockSpec((tm, HEAD_DIM), lambda j, i: (i % pos_blocks, 0)),
            pl.BlockSpec((tm, HEAD_DIM), lambda j, i: (i % pos_blocks, 0)),
        ],
        out_specs=(pl.BlockSpec((tm, tn), lambda j, i: (i, j)),),
        scratch_shapes=[pltpu.VMEM((tn, d), BF16)],
        compiler_params=_params(("arbitrary", "arbitrary")),
        name="proj_rotary",
    )
    return qk, cast


def _forget_kernel(f_ref, b_ref, tri_ref, c_ref, *, chunk):
    z = f_ref[...] + b_ref[...]
    lf = jnp.minimum(z, 0.0) - jnp.log1p(jnp.exp(-jnp.abs(z)))
    lft = lf.T[:N_HEADS, :]
    tri = tri_ref[...]
    carry = jnp.zeros((N_HEADS, 1), F32)
    for n in range(lft.shape[1] // chunk):
        xc = lft[:, n * chunk:(n + 1) * chunk]
        hi = xc.astype(BF16)
        r1 = xc - hi.astype(F32)
        mid = r1.astype(BF16)
        lo = (r1 - mid.astype(F32)).astype(BF16)
        loc = (jnp.dot(hi, tri, preferred_element_type=F32)
               + jnp.dot(mid, tri, preferred_element_type=F32)
               + jnp.dot(lo, tri, preferred_element_type=F32)) + carry
        c_ref[0, :, n * chunk:(n + 1) * chunk] = loc
        carry = loc[:, chunk - 1:chunk]


def _forget_cumsum(f_pad, b_pad, batch, seq):
    chunk = _tiles()["cum_chunk"]
    r = lax.broadcasted_iota(jnp.int32, (chunk, chunk), 0)
    c = lax.broadcasted_iota(jnp.int32, (chunk, chunk), 1)
    tri = (r <= c).astype(BF16)
    return pl.pallas_call(
        functools.partial(_forget_kernel, chunk=chunk),
        out_shape=jax.ShapeDtypeStruct((batch, N_HEADS, seq), F32),
        grid=(batch,),
        in_specs=[
            pl.BlockSpec((seq, LANES), lambda b: (b, 0)),
            pl.BlockSpec((1, LANES), lambda b: (0, 0)),
            pl.BlockSpec((chunk, chunk), lambda b: (0, 0)),
        ],
        out_specs=pl.BlockSpec((1, N_HEADS, seq), lambda b: (b, 0, 0)),
        compiler_params=_params(("parallel",)),
        name="forget_cumsum",
    )(f_pad, b_pad, tri)


def _head(ref, hh, rows=slice(None)):
    return ref[0, rows, hh * HEAD_DIM:(hh + 1) * HEAD_DIM]


def _causal_softmax_pv(s, v, tq):
    length = s.shape[1]
    row = lax.broadcasted_iota(jnp.int32, (tq, tq), 0)
    col = lax.broadcasted_iota(jnp.int32, (tq, tq), 1)
    own = jnp.where(col <= row, s[:, length - tq:], -jnp.inf)
    s = own if length == tq else jnp.concatenate([s[:, :length - tq], own], axis=1)
    m = jnp.max(s, axis=-1, keepdims=True)
    p = jnp.exp2(s - m).astype(BF16)
    v1 = jnp.concatenate([v, jnp.ones(v.shape, v.dtype)], axis=1)
    acc = jnp.dot(p, v1, preferred_element_type=F32)
    return (acc[:, :HEAD_DIM] / acc[:, HEAD_DIM:]).astype(BF16)


def _for_each_query_tile(n_tiles, body):
    i = pl.program_id(2)
    for ii in range(n_tiles):
        pl.when(i == ii)(functools.partial(body, ii))


def _fox_kernel(q_ref, k_ref, v_ref, c_ref, o_ref, *, tq, heads, n_tiles):
    def scores(ii, hh):
        length = (ii + 1) * tq
        q = (_head(q_ref, hh).astype(F32) * (QK_SCALE * LOG2E)).astype(BF16)
        c = c_ref[0, hh]
        bias = (c[:, ii * tq:ii * tq + 1] - c[:, :length]) * LOG2E
        return _dot_nt(q, _head(k_ref, hh, slice(0, length))) + bias

    def body(ii):
        length = (ii + 1) * tq
        for h0 in range(0, heads, HEAD_PAIR):
            pair = range(h0, min(h0 + HEAD_PAIR, heads))
            ss = [scores(ii, hh) for hh in pair]
            for hh, s in zip(pair, ss):
                o_ref[0, :, hh * HEAD_DIM:(hh + 1) * HEAD_DIM] = _causal_softmax_pv(
                    s, _head(v_ref, hh, slice(0, length)), tq)

    _for_each_query_tile(n_tiles, body)


def _fox(p3, c4, batch, seq, casts):
    t = _tiles()
    tq, heads = t["attn_tq"], t["attn_heads"]
    hw = heads * HEAD_DIM
    groups = N_HEADS // heads
    (o,), cast = _call_with_casts(
        functools.partial(_fox_kernel, tq=tq, heads=heads, n_tiles=seq // tq), casts,
        (p3, p3, p3, c4),
        out_shape=(jax.ShapeDtypeStruct((batch, seq, WIDTH), BF16),),
        grid=(batch, groups, seq // tq),
        in_specs=[
            pl.BlockSpec((1, tq, hw), lambda b, g, i: (b, i, g)),
            pl.BlockSpec((1, seq, hw), lambda b, g, i: (b, 0, groups + g)),
            pl.BlockSpec((1, seq, hw), lambda b, g, i: (b, 0, 2 * groups + g)),
            pl.BlockSpec((1, heads, 1, seq), lambda b, g, i: (b, g, 0, 0)),
        ],
        out_specs=(pl.BlockSpec((1, tq, hw), lambda b, g, i: (b, i, g)),),
        compiler_params=_params(("arbitrary", "arbitrary", "arbitrary")),
        name="fox_attention",
    )
    return o, cast


def _moba_select_bias(q_raw, km, ii):
    km_hi, km_lo = _split_bf16(km)
    gate = _dot_nt(km_hi, q_raw) + _dot_nt(km_lo, q_raw)
    block = lax.broadcasted_iota(jnp.int32, gate.shape, 0)
    valid = block < ii
    g = jnp.where(valid, gate, -jnp.inf)
    rank = jnp.zeros(gate.shape, jnp.int32)
    for n in range(ii):
        gn = g[n:n + 1, :]
        ahead = jnp.logical_or(gn > g, jnp.logical_and(gn == g, n < block))
        rank = rank + ahead.astype(jnp.int32)
    visible = jnp.logical_or(jnp.logical_and(valid, rank < MOBA_TOPK), block == ii)
    bias_t = jnp.where(visible, 0.0, MASK_BIAS)
    pad = jnp.zeros((LANES - bias_t.shape[0], bias_t.shape[1]), F32)
    return jnp.concatenate([bias_t, pad], axis=0).T.astype(BF16)


def _moba_kernel(q_ref, k_ref, v_ref, oh_ref, o_ref, km_scr, *, n_blocks, heads):
    blk = MOBA_BLOCK

    @pl.when(pl.program_id(2) == 0)
    def _():
        for hh in range(heads):
            for n in range(n_blocks):
                kb = _head(k_ref, hh, slice(n * blk, (n + 1) * blk)).astype(F32)
                km_scr[hh, n:n + 1, :] = jnp.mean(kb, axis=0, keepdims=True)

    def scores(ii, hh):
        length = (ii + 1) * blk
        q_raw = _head(q_ref, hh)
        bias = _moba_select_bias(q_raw, km_scr[hh], ii)
        q = (q_raw.astype(F32) * (QK_SCALE * LOG2E)).astype(BF16)
        q_aug = jnp.concatenate([q, bias], axis=1)
        k_aug = jnp.concatenate([_head(k_ref, hh, slice(0, length)), oh_ref[:length, :]], axis=1)
        return _dot_nt(q_aug, k_aug)

    def body(ii):
        length = (ii + 1) * blk
        for h0 in range(0, heads, HEAD_PAIR):
            pair = range(h0, min(h0 + HEAD_PAIR, heads))
            ss = [scores(ii, hh) for hh in pair]
            for hh, s in zip(pair, ss):
                o_ref[0, :, hh * HEAD_DIM:(hh + 1) * HEAD_DIM] = _causal_softmax_pv(
                    s, _head(v_ref, hh, slice(0, length)), blk)

    _for_each_query_tile(n_blocks, body)


def _moba(qk3, p3, batch, seq, casts):
    n_blocks = seq // MOBA_BLOCK
    assert seq % MOBA_BLOCK == 0 and n_blocks <= LANES
    heads = _tiles()["attn_heads"]
    hw = heads * HEAD_DIM
    groups = N_HEADS // heads
    key_block = lax.broadcasted_iota(jnp.int32, (seq, LANES), 0) // MOBA_BLOCK
    onehot = (key_block == lax.broadcasted_iota(jnp.int32, (seq, LANES), 1)).astype(BF16)
    (o,), cast = _call_with_casts(
        functools.partial(_moba_kernel, n_blocks=n_blocks, heads=heads), casts,
        (qk3, qk3, p3, onehot),
        out_shape=(jax.ShapeDtypeStruct((batch, seq, WIDTH), BF16),),
        grid=(batch, groups, n_blocks),
        in_specs=[
            pl.BlockSpec((1, MOBA_BLOCK, hw), lambda b, g, i: (b, i, g)),
            pl.BlockSpec((1, seq, hw), lambda b, g, i: (b, 0, groups + g)),
            pl.BlockSpec((1, seq, hw), lambda b, g, i: (b, 0, 3 * groups + g)),
            pl.BlockSpec((seq, LANES), lambda b, g, i: (0, 0)),
        ],
        out_specs=(pl.BlockSpec((1, MOBA_BLOCK, hw), lambda b, g, i: (b, i, g)),),
        scratch_shapes=[pltpu.VMEM((heads, n_blocks, HEAD_DIM), F32)],
        compiler_params=_params(("arbitrary", "arbitrary", "arbitrary")),
        name="moba_attention",
    )
    return o, cast


def _mix_kernel(oa_ref, ob_ref, wa_ref, wb_ref, ga_ref, gb_ref, y_ref):
    ya = jnp.dot(oa_ref[...], wa_ref[...], preferred_element_type=F32)
    yb = jnp.dot(ob_ref[...], wb_ref[...], preferred_element_type=F32)
    ga = jax.nn.sigmoid(ga_ref[...].astype(F32))
    gb = jax.nn.sigmoid(gb_ref[...].astype(F32))
    y_ref[...] = (ga * ya + gb * yb).astype(BF16)


def _mix(oa, ob, wa, wb, p2):
    t = _tiles()
    tm, tn = t["mix_tm"], t["mix_tn"]
    rows, kdim = oa.shape
    d = wa.shape[1]
    ga_blk = (4 * WIDTH) // tn
    gb_blk = (4 * WIDTH + d) // tn
    return pl.pallas_call(
        _mix_kernel,
        out_shape=jax.ShapeDtypeStruct((rows, d), BF16),
        grid=(rows // tm, d // tn),
        in_specs=[
            pl.BlockSpec((tm, kdim), lambda i, j: (i, 0)),
            pl.BlockSpec((tm, kdim), lambda i, j: (i, 0)),
            pl.BlockSpec((kdim, tn), lambda i, j: (0, j)),
            pl.BlockSpec((kdim, tn), lambda i, j: (0, j)),
            pl.BlockSpec((tm, tn), lambda i, j: (i, ga_blk + j)),
            pl.BlockSpec((tm, tn), lambda i, j: (i, gb_blk + j)),
        ],
        out_specs=pl.BlockSpec((tm, tn), lambda i, j: (i, j)),
        compiler_params=_params(("parallel", "parallel")),
        name="gated_mix",
    )(oa, ob, wa, wb, p2, p2)


def _resid_kernel(x_ref, y_ref, w_ref, g_ref, x1_ref, h_ref):
    x1 = x_ref[...] + jnp.dot(y_ref[...], w_ref[...], preferred_element_type=F32)
    x1_ref[...] = x1
    h_ref[...] = _rms(x1, g_ref[...]).astype(BF16)


def _resid(x2, y, w_out, g):
    tm = _tiles()["resid_tm"]
    rows, d = x2.shape
    return pl.pallas_call(
        _resid_kernel,
        out_shape=(jax.ShapeDtypeStruct((rows, d), F32),
                   jax.ShapeDtypeStruct((rows, d), BF16)),
        grid=(rows // tm,),
        in_specs=[
            pl.BlockSpec((tm, d), lambda i: (i, 0)),
            pl.BlockSpec((tm, d), lambda i: (i, 0)),
            pl.BlockSpec((d, d), lambda i: (0, 0)),
            pl.BlockSpec((1, d), lambda i: (0, 0)),
        ],
        out_specs=(pl.BlockSpec((tm, d), lambda i: (i, 0)),
                   pl.BlockSpec((tm, d), lambda i: (i, 0))),
        compiler_params=_params(("parallel",)),
        name="out_residual",
    )(x2, y, w_out, g)


def _ffn_kernel(h_ref, x1_ref, wg_ref, wu_ref, wd_ref, gf_ref, o_ref):
    k = pl.program_id(1)

    @pl.when(k == 0)
    def _():
        o_ref[...] = x1_ref[...]

    h = h_ref[...]
    gate = jnp.dot(h, wg_ref[...], preferred_element_type=F32)
    up = jnp.dot(h, wu_ref[...], preferred_element_type=F32)
    a = (gate * jax.nn.sigmoid(gate) * up).astype(BF16)
    o_ref[...] += jnp.dot(a, wd_ref[...], preferred_element_type=F32)

    @pl.when(k == pl.num_programs(1) - 1)
    def _():
        o_ref[...] = _rms(o_ref[...], gf_ref[...])


def _ffn(h2, x1, wg, wu, wd, gf):
    t = _tiles()
    tm, tf = t["ffn_tm"], t["ffn_tf"]
    rows, d = h2.shape
    dff = wd.shape[0]
    return pl.pallas_call(
        _ffn_kernel,
        out_shape=jax.ShapeDtypeStruct((rows, d), F32),
        grid=(rows // tm, dff // tf),
        in_specs=[
            pl.BlockSpec((tm, d), lambda i, k: (i, 0)),
            pl.BlockSpec((tm, d), lambda i, k: (i, 0)),
            pl.BlockSpec((d, tf), lambda i, k: (0, k)),
            pl.BlockSpec((d, tf), lambda i, k: (0, k)),
            pl.BlockSpec((tf, d), lambda i, k: (k, 0)),
            pl.BlockSpec((1, d), lambda i, k: (0, 0)),
        ],
        out_specs=pl.BlockSpec((tm, d), lambda i, k: (i, 0)),
        compiler_params=_params(("parallel", "arbitrary")),
        name="swiglu_ffn",
    )(h2, x1, wg, wu, wd, gf)


def _rope_tables(seq):
    half = HEAD_DIM // 2
    inv_freq = ROPE_THETA ** (-jnp.arange(half, dtype=F32) / half)
    ang = jnp.arange(seq, dtype=F32)[:, None] * inv_freq[None, :]
    cos, sin = jnp.cos(ang), jnp.sin(ang)
    return jnp.concatenate([cos, cos], axis=-1), jnp.concatenate([-sin, sin], axis=-1)


def kernel(x, norm_mix, w_in, b_forget, w_o_fox, w_o_moba, w_out, norm_ffn, w_gate_up, w_down,
           norm_final):
    batch, seq, d = x.shape
    assert w_in.shape[0] == 1
    cs, sn = _rope_tables(seq)
    x2 = x.reshape(batch * seq, d)
    wt = w_in[0].T
    b_pad = jnp.pad(b_forget[0][None, :], ((0, 0), (0, LANES - N_HEADS)))

    dff = w_down.shape[1]
    h, f_pad = _norm(x2, norm_mix[0][None, :], wt)
    p2, (wo_a, wo_b, w_res) = _proj_plain(
        h, wt, [_Cast(w_o_fox[0], d), _Cast(w_o_moba[0], d), _Cast(w_out[0], d)])
    qk, (wd,) = _proj_rot(h, wt, cs, sn, seq, [_Cast(w_down[0], d)])
    c = _forget_cumsum(f_pad, b_pad, batch, seq)
    p3 = p2.reshape(batch, seq, p2.shape[1])
    o_a, (wg,) = _fox(p3, c.reshape(batch, N_HEADS, 1, seq), batch, seq,
                      [_Cast(w_gate_up[0], dff, 0)])
    o_b, (wu,) = _moba(qk.reshape(batch, seq, qk.shape[1]), p3, batch, seq,
                       [_Cast(w_gate_up[0], dff, 1)])
    y = _mix(o_a.reshape(batch * seq, WIDTH), o_b.reshape(batch * seq, WIDTH), wo_a, wo_b, p2)
    x1, h2 = _resid(x2, y, w_res, norm_ffn[0][None, :])
    out = _ffn(h2, x1, wg, wu, wd, norm_final[None, :])
    return out.reshape(batch, seq, d)
```

```python
import functools
import math
from typing import NamedTuple

import jax
import jax.numpy as jnp
from jax import lax
from jax.experimental import pallas as pl
from jax.experimental.pallas import tpu as pltpu

F32 = jnp.float32
BF16 = jnp.bfloat16

HEAD_DIM = 128
N_HEADS = 8
WIDTH = N_HEADS * HEAD_DIM
MOBA_BLOCK = 256
MOBA_TOPK = 3
ROPE_THETA = 10000.0
RMS_EPS = 1e-6
LOG2E = math.log2(math.e)
QK_SCALE = HEAD_DIM ** -0.5
LANES = 128
BF16_SUBLANES = 16
MASK_BIAS = -1e30
SCORE_GROUP = 8
VMEM_LIMIT = 56 * 1024 * 1024


def _tiles():
    return dict(
        norm_tm=512, proj_tm=1024, proj_tn=1024,
        attn_tq=256, attn_heads=4, attn_tiles_per_step=2,
        mix_tm=1024, mix_tn=1024,
        resid_tm=512,
        ffn_tm=1024, ffn_tf=256,
        cum_chunk=256,
    )


def _params(sem):
    return pltpu.CompilerParams(dimension_semantics=sem, vmem_limit_bytes=VMEM_LIMIT)


def _rms(x, g):
    ms = jnp.mean(x * x, axis=-1, keepdims=True)
    return x * lax.rsqrt(ms + RMS_EPS) * g


def _dot_nt(a, b):
    return lax.dot_general(a, b, (((1,), (1,)), ((), ())), preferred_element_type=F32)


class _Cast(NamedTuple):
    src: jax.Array
    cols: int
    col_block: int = 0


def _call_with_casts(kernel_fn, casts, args, *, grid, in_specs, out_specs, out_shape, **kw):
    steps = math.prod(grid)
    strides = [math.prod(grid[a + 1:]) for a in range(len(grid))]

    def step_of(*g):
        return sum(gi * st for gi, st in zip(g, strides))

    n_in, n_out, n_side = len(in_specs), len(out_shape), len(casts)
    side_in, side_out, side_shape = [], [], []
    for c in casts:
        rows = c.src.shape[0] // steps
        assert rows * steps == c.src.shape[0] and rows % BF16_SUBLANES == 0
        side_in.append(pl.BlockSpec((rows, c.cols), lambda *g, c=c: (step_of(*g), c.col_block)))
        side_out.append(pl.BlockSpec((rows, c.cols), lambda *g: (step_of(*g), 0)))
        side_shape.append(jax.ShapeDtypeStruct((c.src.shape[0], c.cols), BF16))

    def body(*refs):
        src = refs[n_in:n_in + n_side]
        dst = refs[n_in + n_side + n_out:n_in + 2 * n_side + n_out]
        for s_ref, d_ref in zip(src, dst):
            d_ref[...] = s_ref[...].astype(BF16)
        kernel_fn(*refs[:n_in], *refs[n_in + n_side:n_in + n_side + n_out],
                  *refs[n_in + 2 * n_side + n_out:])

    res = pl.pallas_call(
        body, grid=grid, in_specs=[*in_specs, *side_in], out_specs=(*out_specs, *side_out),
        out_shape=(*out_shape, *side_shape), **kw)(*args, *[c.src for c in casts])
    return res[:n_out], res[n_out:]


def _split_bf16(v):
    hi = v.astype(BF16)
    return hi, (v - hi.astype(F32)).astype(BF16)


def _norm_kernel(x_ref, g_ref, wf_ref, h_ref, f_ref):
    h = _rms(x_ref[...], g_ref[...]).astype(BF16)
    h_ref[...] = h
    f_ref[...] = _dot_nt(h, wf_ref[...].astype(BF16))


def _norm(x2, g, wt):
    tm = _tiles()["norm_tm"]
    rows, d = x2.shape
    f_blk = (6 * WIDTH) // LANES
    return pl.pallas_call(
        _norm_kernel,
        out_shape=(jax.ShapeDtypeStruct((rows, d), BF16),
                   jax.ShapeDtypeStruct((rows, LANES), F32)),
        grid=(rows // tm,),
        in_specs=[
            pl.BlockSpec((tm, d), lambda i: (i, 0)),
            pl.BlockSpec((1, d), lambda i: (0, 0)),
            pl.BlockSpec((LANES, d), lambda i: (f_blk, 0)),
        ],
        out_specs=(pl.BlockSpec((tm, d), lambda i: (i, 0)),
                   pl.BlockSpec((tm, LANES), lambda i: (i, 0))),
        compiler_params=_params(("parallel",)),
        name="pre_norm",
    )(x2, g, wt)


def _cast_weight_tile(w_ref, w2_ref, w_scr, shift):
    if shift == 0:
        w_scr[...] = w_ref[...].astype(BF16)
    else:
        w_scr[...] = jnp.concatenate([w_ref[shift:, :], w2_ref[...]], axis=0).astype(BF16)


def _proj_kernel(h_ref, w_ref, w2_ref, o_ref, w_scr, *, n_aligned):
    j = pl.program_id(0)

    @pl.when(jnp.logical_and(pl.program_id(1) == 0, j < n_aligned))
    def _():
        _cast_weight_tile(w_ref, w2_ref, w_scr, 0)

    @pl.when(jnp.logical_and(pl.program_id(1) == 0, j >= n_aligned))
    def _():
        _cast_weight_tile(w_ref, w2_ref, w_scr, N_HEADS)

    o_ref[...] = _dot_nt(h_ref[...], w_scr[...]).astype(BF16)


def _proj_rot_kernel(h_ref, w_ref, cs_ref, sn_ref, o_ref, w_scr):
    @pl.when(pl.program_id(1) == 0)
    def _():
        _cast_weight_tile(w_ref, None, w_scr, 0)

    acc = _dot_nt(h_ref[...], w_scr[...])
    cs = cs_ref[...]
    sn = sn_ref[...]
    for hh in range(acc.shape[1] // HEAD_DIM):
        t = acc[:, hh * HEAD_DIM:(hh + 1) * HEAD_DIM]
        r = pltpu.roll(t, HEAD_DIM // 2, axis=1)
        o_ref[:, hh * HEAD_DIM:(hh + 1) * HEAD_DIM] = (t * cs + r * sn).astype(BF16)


def _proj_plain(h, wt, casts):
    t = _tiles()
    tm, tn = t["proj_tm"], t["proj_tn"]
    rows, d = h.shape
    per_group = WIDTH // tn
    n_aligned = 4 * per_group
    n_gate = (2 * d) // tn
    gate0 = (6 * WIDTH) // tn

    def w_blk(j, i):
        aligned = jnp.where(j < 3 * per_group, j, j + 2 * per_group)
        return jnp.where(j < n_aligned, aligned, gate0 + (j - n_aligned)), 0

    def w2_blk(j, i):
        nxt = jnp.maximum(j - n_aligned, 0) + 1
        return (gate0 + nxt) * (tn // N_HEADS), 0

    (p2,), cast = _call_with_casts(
        functools.partial(_proj_kernel, n_aligned=n_aligned), casts, (h, wt, wt),
        out_shape=(jax.ShapeDtypeStruct((rows, (n_aligned + n_gate) * tn), BF16),),
        grid=(n_aligned + n_gate, rows // tm),
        in_specs=[
            pl.BlockSpec((tm, d), lambda j, i: (i, 0)),
            pl.BlockSpec((tn, d), w_blk),
            pl.BlockSpec((N_HEADS, d), w2_blk),
        ],
        out_specs=(pl.BlockSpec((tm, tn), lambda j, i: (i, j)),),
        scratch_shapes=[pltpu.VMEM((tn, d), BF16)],
        compiler_params=_params(("arbitrary", "arbitrary")),
        name="proj_plain",
    )
    return p2, cast


def _proj_rot(h, wt, cs, sn, seq, casts):
    t = _tiles()
    tm, tn = t["proj_tm"], t["proj_tn"]
    rows, d = h.shape
    first = (3 * WIDTH) // tn
    pos_blocks = seq // tm
    (qk,), cast = _call_with_casts(
        _proj_rot_kernel, casts, (h, wt, cs, sn),
        out_shape=(jax.ShapeDtypeStruct((rows, 2 * WIDTH), BF16),),
        grid=((2 * WIDTH) // tn, rows // tm),
        in_specs=[
            pl.BlockSpec((tm, d), lambda j, i: (i, 0)),
            pl.BlockSpec((tn, d), lambda j, i: (first + j, 0)),
            pl.BlockSpec((tm, HEAD_DIM), lambda j, i: (i % pos_blocks, 0)),
            pl.BlockSpec((tm, HEAD_DIM), lambda j, i: (i % pos_blocks, 0)),
        ],
        out_specs=(pl.BlockSpec((tm, tn), lambda j, i: (i, j)),),
        scratch_shapes=[pltpu.VMEM((tn, d), BF16)],
        compiler_params=_params(("arbitrary", "arbitrary")),
        name="proj_rotary",
    )
    return qk, cast


def _forget_kernel(f_ref, b_ref, tri_ref, c_ref, *, chunk):
    z = f_ref[...] + b_ref[...]
    lf = jnp.minimum(z, 0.0) - jnp.log1p(jnp.exp(-jnp.abs(z)))
    lft = lf.T[:N_HEADS, :]
    tri = tri_ref[...]
    carry = jnp.zeros((N_HEADS, 1), F32)
    for n in range(lft.shape[1] // chunk):
        xc = lft[:, n * chunk:(n + 1) * chunk]
        hi = xc.astype(BF16)
        r1 = xc - hi.astype(F32)
        mid = r1.astype(BF16)
        lo = (r1 - mid.astype(F32)).astype(BF16)
        loc = (jnp.dot(hi, tri, preferred_element_type=F32)
               + jnp.dot(mid, tri, preferred_element_type=F32)
               + jnp.dot(lo, tri, preferred_element_type=F32)) + carry
        c_ref[0, :, n * chunk:(n + 1) * chunk] = loc
        carry = loc[:, chunk - 1:chunk]


def _forget_cumsum(f_pad, b_pad, batch, seq):
    chunk = _tiles()["cum_chunk"]
    r = lax.broadcasted_iota(jnp.int32, (chunk, chunk), 0)
    c = lax.broadcasted_iota(jnp.int32, (chunk, chunk), 1)
    tri = (r <= c).astype(BF16)
    return pl.pallas_call(
        functools.partial(_forget_kernel, chunk=chunk),
        out_shape=jax.ShapeDtypeStruct((batch, N_HEADS, seq), F32),
        grid=(batch,),
        in_specs=[
            pl.BlockSpec((seq, LANES), lambda b: (b, 0)),
            pl.BlockSpec((1, LANES), lambda b: (0, 0)),
            pl.BlockSpec((chunk, chunk), lambda b: (0, 0)),
        ],
        out_specs=pl.BlockSpec((1, N_HEADS, seq), lambda b: (b, 0, 0)),
        compiler_params=_params(("parallel",)),
        name="forget_cumsum",
    )(f_pad, b_pad, tri)


def _head(ref, hh, rows=slice(None)):
    return ref[0, rows, hh * HEAD_DIM:(hh + 1) * HEAD_DIM]


def _causal_softmax_pv(s, v, tq):
    length = s.shape[1]
    row = lax.broadcasted_iota(jnp.int32, (tq, tq), 0)
    col = lax.broadcasted_iota(jnp.int32, (tq, tq), 1)
    own = jnp.where(col <= row, s[:, length - tq:], -jnp.inf)
    s = own if length == tq else jnp.concatenate([s[:, :length - tq], own], axis=1)
    m = jnp.max(s, axis=-1, keepdims=True)
    p = jnp.exp2(s - m).astype(BF16)
    v1 = jnp.concatenate([v, jnp.ones(v.shape, v.dtype)], axis=1)
    acc = jnp.dot(p, v1, preferred_element_type=F32)
    return (acc[:, :HEAD_DIM] / acc[:, HEAD_DIM:]).astype(BF16)


def _tile_rows(ii, tq):
    return slice(ii * tq, (ii + 1) * tq)


def _attend(scores, v_ref, o_ref, *, n_tiles, per_step, tq, heads):
    def run(tiles):
        jobs = [(ii, hh) for ii in tiles for hh in range(heads)]
        for j0 in range(0, len(jobs), SCORE_GROUP):
            group = jobs[j0:j0 + SCORE_GROUP]
            ss = [scores(ii, hh) for ii, hh in group]
            for (ii, hh), s in zip(group, ss):
                o_ref[0, _tile_rows(ii, tq), hh * HEAD_DIM:(hh + 1) * HEAD_DIM] = (
                    _causal_softmax_pv(s, _head(v_ref, hh, slice(0, (ii + 1) * tq)), tq))

    n_steps = n_tiles // per_step
    assert per_step in (1, 2) and n_steps * per_step == n_tiles
    step = pl.program_id(2)
    for pp in range(n_steps):
        tiles = [pp] if per_step == 1 else [pp, n_tiles - 1 - pp]
        pl.when(step == pp)(functools.partial(run, tiles))


def _fox_kernel(q_ref, k_ref, v_ref, c_ref, o_ref, *, tq, heads, n_tiles, per_step):
    def scores(ii, hh):
        length = (ii + 1) * tq
        q = (_head(q_ref, hh, _tile_rows(ii, tq)).astype(F32) * (QK_SCALE * LOG2E)).astype(BF16)
        c = c_ref[0, hh]
        bias = (c[:, ii * tq:ii * tq + 1] - c[:, :length]) * LOG2E
        return _dot_nt(q, _head(k_ref, hh, slice(0, length))) + bias

    _attend(scores, v_ref, o_ref, n_tiles=n_tiles, per_step=per_step, tq=tq, heads=heads)


def _fox(p3, c4, batch, seq, casts):
    t = _tiles()
    tq, heads, per_step = t["attn_tq"], t["attn_heads"], t["attn_tiles_per_step"]
    hw = heads * HEAD_DIM
    groups = N_HEADS // heads
    n_tiles = seq // tq
    (o,), cast = _call_with_casts(
        functools.partial(_fox_kernel, tq=tq, heads=heads, n_tiles=n_tiles, per_step=per_step),
        casts, (p3, p3, p3, c4),
        out_shape=(jax.ShapeDtypeStruct((batch, seq, WIDTH), BF16),),
        grid=(batch, groups, n_tiles // per_step),
        in_specs=[
            pl.BlockSpec((1, seq, hw), lambda b, g, i: (b, 0, g)),
            pl.BlockSpec((1, seq, hw), lambda b, g, i: (b, 0, groups + g)),
            pl.BlockSpec((1, seq, hw), lambda b, g, i: (b, 0, 2 * groups + g)),
            pl.BlockSpec((1, heads, 1, seq), lambda b, g, i: (b, g, 0, 0)),
        ],
        out_specs=(pl.BlockSpec((1, seq, hw), lambda b, g, i: (b, 0, g)),),
        compiler_params=_params(("arbitrary", "arbitrary", "arbitrary")),
        name="fox_attention",
    )
    return o, cast


def _moba_block_bias(q_raw, km, ii):
    km_hi, km_lo = _split_bf16(km)
    gate = _dot_nt(km_hi, q_raw) + _dot_nt(km_lo, q_raw)
    block = lax.broadcasted_iota(jnp.int32, gate.shape, 0)
    g = jnp.where(block < ii, gate, -jnp.inf)
    rank = jnp.zeros(gate.shape, jnp.int32)
    for n in range(ii):
        gn = g[n:n + 1, :]
        ahead = jnp.logical_or(gn > g, jnp.logical_and(gn == g, n < block))
        rank = rank + ahead.astype(jnp.int32)
    bias_t = jnp.where(rank < MOBA_TOPK, 0.0, MASK_BIAS)
    pad = jnp.zeros((LANES - bias_t.shape[0], bias_t.shape[1]), F32)
    return jnp.concatenate([bias_t, pad], axis=0).T


def _moba_kernel(q_ref, k_ref, v_ref, o_ref, km_scr, *, n_blocks, heads, per_step):
    blk = MOBA_BLOCK

    @pl.when(pl.program_id(2) == 0)
    def _():
        for hh in range(heads):
            for n in range(n_blocks):
                kb = _head(k_ref, hh, _tile_rows(n, blk)).astype(F32)
                km_scr[hh, n:n + 1, :] = jnp.mean(kb, axis=0, keepdims=True)

    def scores(ii, hh):
        length = (ii + 1) * blk
        q_raw = _head(q_ref, hh, _tile_rows(ii, blk))
        q = (q_raw.astype(F32) * (QK_SCALE * LOG2E)).astype(BF16)
        s = _dot_nt(q, _head(k_ref, hh, slice(0, length)))
        if ii <= MOBA_TOPK:
            return s
        bias = _moba_block_bias(q_raw, km_scr[hh], ii)
        past = [s[:, _tile_rows(j, blk)] + bias[:, j:j + 1] for j in range(ii)]
        return jnp.concatenate(past + [s[:, _tile_rows(ii, blk)]], axis=1)

    _attend(scores, v_ref, o_ref, n_tiles=n_blocks, per_step=per_step, tq=blk, heads=heads)


def _moba(qk3, p3, batch, seq, casts):
    n_blocks = seq // MOBA_BLOCK
    assert seq % MOBA_BLOCK == 0 and n_blocks <= LANES
    t = _tiles()
    heads, per_step = t["attn_heads"], t["attn_tiles_per_step"]
    hw = heads * HEAD_DIM
    groups = N_HEADS // heads
    (o,), cast = _call_with_casts(
        functools.partial(_moba_kernel, n_blocks=n_blocks, heads=heads, per_step=per_step), casts,
        (qk3, qk3, p3),
        out_shape=(jax.ShapeDtypeStruct((batch, seq, WIDTH), BF16),),
        grid=(batch, groups, n_blocks // per_step),
        in_specs=[
            pl.BlockSpec((1, seq, hw), lambda b, g, i: (b, 0, g)),
            pl.BlockSpec((1, seq, hw), lambda b, g, i: (b, 0, groups + g)),
            pl.BlockSpec((1, seq, hw), lambda b, g, i: (b, 0, 3 * groups + g)),
        ],
        out_specs=(pl.BlockSpec((1, seq, hw), lambda b, g, i: (b, 0, g)),),
        scratch_shapes=[pltpu.VMEM((heads, n_blocks, HEAD_DIM), F32)],
        compiler_params=_params(("arbitrary", "arbitrary", "arbitrary")),
        name="moba_attention",
    )
    return o, cast


def _mix_kernel(oa_ref, ob_ref, wa_ref, wb_ref, ga_ref, gb_ref, y_ref):
    ya = jnp.dot(oa_ref[...], wa_ref[...], preferred_element_type=F32)
    yb = jnp.dot(ob_ref[...], wb_ref[...], preferred_element_type=F32)
    ga = jax.nn.sigmoid(ga_ref[...].astype(F32))
    gb = jax.nn.sigmoid(gb_ref[...].astype(F32))
    y_ref[...] = (ga * ya + gb * yb).astype(BF16)


def _mix(oa, ob, wa, wb, p2):
    t = _tiles()
    tm, tn = t["mix_tm"], t["mix_tn"]
    rows, kdim = oa.shape
    d = wa.shape[1]
    ga_blk = (4 * WIDTH) // tn
    gb_blk = (4 * WIDTH + d) // tn
    return pl.pallas_call(
        _mix_kernel,
        out_shape=jax.ShapeDtypeStruct((rows, d), BF16),
        grid=(rows // tm, d // tn),
        in_specs=[
            pl.BlockSpec((tm, kdim), lambda i, j: (i, 0)),
            pl.BlockSpec((tm, kdim), lambda i, j: (i, 0)),
            pl.BlockSpec((kdim, tn), lambda i, j: (0, j)),
            pl.BlockSpec((kdim, tn), lambda i, j: (0, j)),
            pl.BlockSpec((tm, tn), lambda i, j: (i, ga_blk + j)),
            pl.BlockSpec((tm, tn), lambda i, j: (i, gb_blk + j)),
        ],
        out_specs=pl.BlockSpec((tm, tn), lambda i, j: (i, j)),
        compiler_params=_params(("parallel", "parallel")),
        name="gated_mix",
    )(oa, ob, wa, wb, p2, p2)


def _resid_kernel(x_ref, y_ref, w_ref, g_ref, x1_ref, h_ref):
    x1 = x_ref[...] + jnp.dot(y_ref[...], w_ref[...], preferred_element_type=F32)
    x1_ref[...] = x1
    h_ref[...] = _rms(x1, g_ref[...]).astype(BF16)


def _resid(x2, y, w_out, g):
    tm = _tiles()["resid_tm"]
    rows, d = x2.shape
    return pl.pallas_call(
        _resid_kernel,
        out_shape=(jax.ShapeDtypeStruct((rows, d), F32),
                   jax.ShapeDtypeStruct((rows, d), BF16)),
        grid=(rows // tm,),
        in_specs=[
            pl.BlockSpec((tm, d), lambda i: (i, 0)),
            pl.BlockSpec((tm, d), lambda i: (i, 0)),
            pl.BlockSpec((d, d), lambda i: (0, 0)),
            pl.BlockSpec((1, d), lambda i: (0, 0)),
        ],
        out_specs=(pl.BlockSpec((tm, d), lambda i: (i, 0)),
                   pl.BlockSpec((tm, d), lambda i: (i, 0))),
        compiler_params=_params(("parallel",)),
        name="out_residual",
    )(x2, y, w_out, g)


def _ffn_kernel(h_ref, x1_ref, wg_ref, wu_ref, wd_ref, gf_ref, o_ref):
    k = pl.program_id(1)

    @pl.when(k == 0)
    def _():
        o_ref[...] = x1_ref[...]

    h = h_ref[...]
    gate = jnp.dot(h, wg_ref[...], preferred_element_type=F32)
    up = jnp.dot(h, wu_ref[...], preferred_element_type=F32)
    a = (gate * jax.nn.sigmoid(gate) * up).astype(BF16)
    o_ref[...] += jnp.dot(a, wd_ref[...], preferred_element_type=F32)

    @pl.when(k == pl.num_programs(1) - 1)
    def _():
        o_ref[...] = _rms(o_ref[...], gf_ref[...])


def _ffn(h2, x1, wg, wu, wd, gf):
    t = _tiles()
    tm, tf = t["ffn_tm"], t["ffn_tf"]
    rows, d = h2.shape
    dff = wd.shape[0]
    return pl.pallas_call(
        _ffn_kernel,
        out_shape=jax.ShapeDtypeStruct((rows, d), F32),
        grid=(rows // tm, dff // tf),
        in_specs=[
            pl.BlockSpec((tm, d), lambda i, k: (i, 0)),
            pl.BlockSpec((tm, d), lambda i, k: (i, 0)),
            pl.BlockSpec((d, tf), lambda i, k: (0, k)),
            pl.BlockSpec((d, tf), lambda i, k: (0, k)),
            pl.BlockSpec((tf, d), lambda i, k: (k, 0)),
            pl.BlockSpec((1, d), lambda i, k: (0, 0)),
        ],
        out_specs=pl.BlockSpec((tm, d), lambda i, k: (i, 0)),
        compiler_params=_params(("parallel", "arbitrary")),
        name="swiglu_ffn",
    )(h2, x1, wg, wu, wd, gf)


def _rope_tables(seq):
    half = HEAD_DIM // 2
    inv_freq = ROPE_THETA ** (-jnp.arange(half, dtype=F32) / half)
    ang = jnp.arange(seq, dtype=F32)[:, None] * inv_freq[None, :]
    cos, sin = jnp.cos(ang), jnp.sin(ang)
    return jnp.concatenate([cos, cos], axis=-1), jnp.concatenate([-sin, sin], axis=-1)


def kernel(x, norm_mix, w_in, b_forget, w_o_fox, w_o_moba, w_out, norm_ffn, w_gate_up, w_down,
           norm_final):
    batch, seq, d = x.shape
    assert w_in.shape[0] == 1
    cs, sn = _rope_tables(seq)
    x2 = x.reshape(batch * seq, d)
    wt = w_in[0].T
    b_pad = jnp.pad(b_forget[0][None, :], ((0, 0), (0, LANES - N_HEADS)))

    dff = w_down.shape[1]
    h, f_pad = _norm(x2, norm_mix[0][None, :], wt)
    p2, (wo_a, wo_b, w_res) = _proj_plain(
        h, wt, [_Cast(w_o_fox[0], d), _Cast(w_o_moba[0], d), _Cast(w_out[0], d)])
    qk, (wd,) = _proj_rot(h, wt, cs, sn, seq, [_Cast(w_down[0], d)])
    c = _forget_cumsum(f_pad, b_pad, batch, seq)
    p3 = p2.reshape(batch, seq, p2.shape[1])
    o_a, (wg,) = _fox(p3, c.reshape(batch, N_HEADS, 1, seq), batch, seq,
                      [_Cast(w_gate_up[0], dff, 0)])
    o_b, (wu,) = _moba(qk.reshape(batch, seq, qk.shape[1]), p3, batch, seq,
                       [_Cast(w_gate_up[0], dff, 1)])
    y = _mix(o_a.reshape(batch * seq, WIDTH), o_b.reshape(batch * seq, WIDTH), wo_a, wo_b, p2)
    x1, h2 = _resid(x2, y, w_res, norm_ffn[0][None, :])
    out = _ffn(h2, x1, wg, wu, wd, norm_final[None, :])
    return out.reshape(batch, seq, d)
```

```python
import functools
import math
from typing import NamedTuple

import jax
import jax.numpy as jnp
from jax import lax
from jax.experimental import pallas as pl
from jax.experimental.pallas import tpu as pltpu

F32 = jnp.float32
BF16 = jnp.bfloat16

HEAD_DIM = 128
N_HEADS = 8
WIDTH = N_HEADS * HEAD_DIM
MOBA_BLOCK = 256
MOBA_TOPK = 3
ROPE_THETA = 10000.0
RMS_EPS = 1e-6
LOG2E = math.log2(math.e)
QK_SCALE = HEAD_DIM ** -0.5
LANES = 128
BF16_SUBLANES = 16
MASK_BIAS = -1e30
SCORE_GROUP = 8
VMEM_LIMIT = 60 * 1024 * 1024


def _tiles():
    return dict(
        norm_tm=512, proj_tm=2048, rot_tm=1024, proj_tn=1024,
        attn_tq=256, attn_heads=4, attn_tiles_per_step=2,
        mix_tm=1024, mix_tn=1024,
        resid_tm=512,
        ffn_tm=1024, ffn_tf=512,
        cum_chunk=256,
    )


def _params(sem):
    return pltpu.CompilerParams(dimension_semantics=sem, vmem_limit_bytes=VMEM_LIMIT)


def _rms(x, g):
    ms = jnp.mean(x * x, axis=-1, keepdims=True)
    return x * lax.rsqrt(ms + RMS_EPS) * g


def _dot_nt(a, b):
    return lax.dot_general(a, b, (((1,), (1,)), ((), ())), preferred_element_type=F32)


class _Cast(NamedTuple):
    src: jax.Array
    cols: int
    col_block: int = 0


def _call_with_casts(kernel_fn, casts, args, *, grid, in_specs, out_specs, out_shape, **kw):
    steps = math.prod(grid)
    strides = [math.prod(grid[a + 1:]) for a in range(len(grid))]

    def step_of(*g):
        return sum(gi * st for gi, st in zip(g, strides))

    n_in, n_out, n_side = len(in_specs), len(out_shape), len(casts)
    side_in, side_out, side_shape = [], [], []
    for c in casts:
        rows = c.src.shape[0] // steps
        assert rows * steps == c.src.shape[0] and rows % BF16_SUBLANES == 0
        side_in.append(pl.BlockSpec((rows, c.cols), lambda *g, c=c: (step_of(*g), c.col_block)))
        side_out.append(pl.BlockSpec((rows, c.cols), lambda *g: (step_of(*g), 0)))
        side_shape.append(jax.ShapeDtypeStruct((c.src.shape[0], c.cols), BF16))

    def body(*refs):
        src = refs[n_in:n_in + n_side]
        dst = refs[n_in + n_side + n_out:n_in + 2 * n_side + n_out]
        for s_ref, d_ref in zip(src, dst):
            d_ref[...] = s_ref[...].astype(BF16)
        kernel_fn(*refs[:n_in], *refs[n_in + n_side:n_in + n_side + n_out],
                  *refs[n_in + 2 * n_side + n_out:])

    res = pl.pallas_call(
        body, grid=grid, in_specs=[*in_specs, *side_in], out_specs=(*out_specs, *side_out),
        out_shape=(*out_shape, *side_shape), **kw)(*args, *[c.src for c in casts])
    return res[:n_out], res[n_out:]


def _split_bf16(v):
    hi = v.astype(BF16)
    return hi, (v - hi.astype(F32)).astype(BF16)


def _norm_kernel(x_ref, g_ref, wf_ref, h_ref, f_ref):
    h = _rms(x_ref[...], g_ref[...]).astype(BF16)
    h_ref[...] = h
    f_ref[...] = _dot_nt(h, wf_ref[...].astype(BF16))


def _norm(x2, g, wt):
    tm = _tiles()["norm_tm"]
    rows, d = x2.shape
    f_blk = (6 * WIDTH) // LANES
    return pl.pallas_call(
        _norm_kernel,
        out_shape=(jax.ShapeDtypeStruct((rows, d), BF16),
                   jax.ShapeDtypeStruct((rows, LANES), F32)),
        grid=(rows // tm,),
        in_specs=[
            pl.BlockSpec((tm, d), lambda i: (i, 0)),
            pl.BlockSpec((1, d), lambda i: (0, 0)),
            pl.BlockSpec((LANES, d), lambda i: (f_blk, 0)),
        ],
        out_specs=(pl.BlockSpec((tm, d), lambda i: (i, 0)),
                   pl.BlockSpec((tm, LANES), lambda i: (i, 0))),
        compiler_params=_params(("parallel",)),
        name="pre_norm",
    )(x2, g, wt)


def _cast_weight_tile(w_ref, w2_ref, w_scr, shift):
    if shift == 0:
        w_scr[...] = w_ref[...].astype(BF16)
    else:
        w_scr[...] = jnp.concatenate([w_ref[shift:, :], w2_ref[...]], axis=0).astype(BF16)


def _proj_kernel(h_ref, w_ref, w2_ref, o_ref, w_scr, *, n_aligned):
    j = pl.program_id(0)

    @pl.when(jnp.logical_and(pl.program_id(1) == 0, j < n_aligned))
    def _():
        _cast_weight_tile(w_ref, w2_ref, w_scr, 0)

    @pl.when(jnp.logical_and(pl.program_id(1) == 0, j >= n_aligned))
    def _():
        _cast_weight_tile(w_ref, w2_ref, w_scr, N_HEADS)

    o_ref[...] = _dot_nt(h_ref[...], w_scr[...]).astype(BF16)


def _proj_rot_kernel(h_ref, w_ref, cs_ref, sn_ref, o_ref, w_scr):
    @pl.when(pl.program_id(1) == 0)
    def _():
        _cast_weight_tile(w_ref, None, w_scr, 0)

    acc = _dot_nt(h_ref[...], w_scr[...])
    cs = cs_ref[...]
    sn = sn_ref[...]
    for hh in range(acc.shape[1] // HEAD_DIM):
        t = acc[:, hh * HEAD_DIM:(hh + 1) * HEAD_DIM]
        r = pltpu.roll(t, HEAD_DIM // 2, axis=1)
        o_ref[:, hh * HEAD_DIM:(hh + 1) * HEAD_DIM] = (t * cs + r * sn).astype(BF16)


def _proj_plain(h, wt, casts):
    t = _tiles()
    tm, tn = t["proj_tm"], t["proj_tn"]
    rows, d = h.shape
    per_group = WIDTH // tn
    n_aligned = 4 * per_group
    n_gate = (2 * d) // tn
    gate0 = (6 * WIDTH) // tn

    def w_blk(j, i):
        aligned = jnp.where(j < 3 * per_group, j, j + 2 * per_group)
        return jnp.where(j < n_aligned, aligned, gate0 + (j - n_aligned)), 0

    def w2_blk(j, i):
        nxt = jnp.maximum(j - n_aligned, 0) + 1
        return (gate0 + nxt) * (tn // N_HEADS), 0

    (p2,), cast = _call_with_casts(
        functools.partial(_proj_kernel, n_aligned=n_aligned), casts, (h, wt, wt),
        out_shape=(jax.ShapeDtypeStruct((rows, (n_aligned + n_gate) * tn), BF16),),
        grid=(n_aligned + n_gate, rows // tm),
        in_specs=[
            pl.BlockSpec((tm, d), lambda j, i: (i, 0)),
            pl.BlockSpec((tn, d), w_blk),
            pl.BlockSpec((N_HEADS, d), w2_blk),
        ],
        out_specs=(pl.BlockSpec((tm, tn), lambda j, i: (i, j)),),
        scratch_shapes=[pltpu.VMEM((tn, d), BF16)],
        compiler_params=_params(("arbitrary", "arbitrary")),
        name="proj_plain",
    )
    return p2, cast


def _proj_rot(h, wt, cs, sn, seq, casts):
    t = _tiles()
    tm, tn = t["rot_tm"], t["proj_tn"]
    rows, d = h.shape
    first = (3 * WIDTH) // tn
    pos_blocks = seq // tm
    (qk,), cast = _call_with_casts(
        _proj_rot_kernel, casts, (h, wt, cs, sn),
        out_shape=(jax.ShapeDtypeStruct((rows, 2 * WIDTH), BF16),),
        grid=((2 * WIDTH) // tn, rows // tm),
        in_specs=[
            pl.BlockSpec((tm, d), lambda j, i: (i, 0)),
            pl.BlockSpec((tn, d), lambda j, i: (first + j, 0)),
            pl.BlockSpec((tm, HEAD_DIM), lambda j, i: (i % pos_blocks, 0)),
            pl.BlockSpec((tm, HEAD_DIM), lambda j, i: (i % pos_blocks, 0)),
        ],
        out_specs=(pl.BlockSpec((tm, tn), lambda j, i: (i, j)),),
        scratch_shapes=[pltpu.VMEM((tn, d), BF16)],
        compiler_params=_params(("arbitrary", "arbitrary")),
        name="proj_rotary",
    )
    return qk, cast


def _forget_kernel(f_ref, b_ref, tri_ref, c_ref, *, chunk):
    z = f_ref[...] + b_ref[...]
    lf = jnp.minimum(z, 0.0) - jnp.log1p(jnp.exp(-jnp.abs(z)))
    lft = lf.T[:N_HEADS, :]
    tri = tri_ref[...]
    carry = jnp.zeros((N_HEADS, 1), F32)
    for n in range(lft.shape[1] // chunk):
        xc = lft[:, n * chunk:(n + 1) * chunk]
        hi = xc.astype(BF16)
        r1 = xc - hi.astype(F32)
        mid = r1.astype(BF16)
        lo = (r1 - mid.astype(F32)).astype(BF16)
        loc = (jnp.dot(hi, tri, preferred_element_type=F32)
               + jnp.dot(mid, tri, preferred_element_type=F32)
               + jnp.dot(lo, tri, preferred_element_type=F32)) + carry
        c_ref[0, :, n * chunk:(n + 1) * chunk] = loc
        carry = loc[:, chunk - 1:chunk]


def _forget_cumsum(f_pad, b_pad, batch, seq):
    chunk = _tiles()["cum_chunk"]
    r = lax.broadcasted_iota(jnp.int32, (chunk, chunk), 0)
    c = lax.broadcasted_iota(jnp.int32, (chunk, chunk), 1)
    tri = (r <= c).astype(BF16)
    return pl.pallas_call(
        functools.partial(_forget_kernel, chunk=chunk),
        out_shape=jax.ShapeDtypeStruct((batch, N_HEADS, seq), F32),
        grid=(batch,),
        in_specs=[
            pl.BlockSpec((seq, LANES), lambda b: (b, 0)),
            pl.BlockSpec((1, LANES), lambda b: (0, 0)),
            pl.BlockSpec((chunk, chunk), lambda b: (0, 0)),
        ],
        out_specs=pl.BlockSpec((1, N_HEADS, seq), lambda b: (b, 0, 0)),
        compiler_params=_params(("parallel",)),
        name="forget_cumsum",
    )(f_pad, b_pad, tri)


def _head(ref, hh, rows=slice(None)):
    return ref[0, rows, hh * HEAD_DIM:(hh + 1) * HEAD_DIM]


def _causal_softmax_pv(s, v, tq):
    length = s.shape[1]
    row = lax.broadcasted_iota(jnp.int32, (tq, tq), 0)
    col = lax.broadcasted_iota(jnp.int32, (tq, tq), 1)
    own = jnp.where(col <= row, s[:, length - tq:], -jnp.inf)
    s = own if length == tq else jnp.concatenate([s[:, :length - tq], own], axis=1)
    m = jnp.max(s, axis=-1, keepdims=True)
    p = jnp.exp2(s - m).astype(BF16)
    v1 = jnp.concatenate([v, jnp.ones(v.shape, v.dtype)], axis=1)
    acc = jnp.dot(p, v1, preferred_element_type=F32)
    return (acc[:, :HEAD_DIM] / acc[:, HEAD_DIM:]).astype(BF16)


def _tile_rows(ii, tq):
    return slice(ii * tq, (ii + 1) * tq)


def _attend(scores, v_ref, o_ref, *, n_tiles, per_step, tq, heads):
    def run(tiles):
        jobs = [(ii, hh) for ii in tiles for hh in range(heads)]
        for j0 in range(0, len(jobs), SCORE_GROUP):
            group = jobs[j0:j0 + SCORE_GROUP]
            ss = [scores(ii, hh) for ii, hh in group]
            for (ii, hh), s in zip(group, ss):
                o_ref[0, _tile_rows(ii, tq), hh * HEAD_DIM:(hh + 1) * HEAD_DIM] = (
                    _causal_softmax_pv(s, _head(v_ref, hh, slice(0, (ii + 1) * tq)), tq))

    n_steps = n_tiles // per_step
    assert per_step in (1, 2) and n_steps * per_step == n_tiles
    step = pl.program_id(2)
    for pp in range(n_steps):
        tiles = [pp] if per_step == 1 else [pp, n_tiles - 1 - pp]
        pl.when(step == pp)(functools.partial(run, tiles))


def _fox_kernel(q_ref, k_ref, v_ref, c_ref, o_ref, *, tq, heads, n_tiles, per_step):
    def scores(ii, hh):
        length = (ii + 1) * tq
        q = (_head(q_ref, hh, _tile_rows(ii, tq)).astype(F32) * (QK_SCALE * LOG2E)).astype(BF16)
        c = c_ref[0, hh]
        bias = (c[:, ii * tq:ii * tq + 1] - c[:, :length]) * LOG2E
        return _dot_nt(q, _head(k_ref, hh, slice(0, length))) + bias

    _attend(scores, v_ref, o_ref, n_tiles=n_tiles, per_step=per_step, tq=tq, heads=heads)


def _fox(p3, c4, batch, seq, casts):
    t = _tiles()
    tq, heads, per_step = t["attn_tq"], t["attn_heads"], t["attn_tiles_per_step"]
    hw = heads * HEAD_DIM
    groups = N_HEADS // heads
    n_tiles = seq // tq
    (o,), cast = _call_with_casts(
        functools.partial(_fox_kernel, tq=tq, heads=heads, n_tiles=n_tiles, per_step=per_step),
        casts, (p3, p3, p3, c4),
        out_shape=(jax.ShapeDtypeStruct((batch, seq, WIDTH), BF16),),
        grid=(batch, groups, n_tiles // per_step),
        in_specs=[
            pl.BlockSpec((1, seq, hw), lambda b, g, i: (b, 0, g)),
            pl.BlockSpec((1, seq, hw), lambda b, g, i: (b, 0, groups + g)),
            pl.BlockSpec((1, seq, hw), lambda b, g, i: (b, 0, 2 * groups + g)),
            pl.BlockSpec((1, heads, 1, seq), lambda b, g, i: (b, g, 0, 0)),
        ],
        out_specs=(pl.BlockSpec((1, seq, hw), lambda b, g, i: (b, 0, g)),),
        compiler_params=_params(("arbitrary", "arbitrary", "arbitrary")),
        name="fox_attention",
    )
    return o, cast


def _moba_block_bias(q_raw, km, ii):
    km_hi, km_lo = _split_bf16(km)
    gate = _dot_nt(km_hi, q_raw) + _dot_nt(km_lo, q_raw)
    block = lax.broadcasted_iota(jnp.int32, gate.shape, 0)
    g = jnp.where(block < ii, gate, -jnp.inf)
    rank = jnp.zeros(gate.shape, jnp.int32)
    for n in range(ii):
        gn = g[n:n + 1, :]
        ahead = jnp.logical_or(gn > g, jnp.logical_and(gn == g, n < block))
        rank = rank + ahead.astype(jnp.int32)
    bias_t = jnp.where(rank < MOBA_TOPK, 0.0, MASK_BIAS)
    pad = jnp.zeros((LANES - bias_t.shape[0], bias_t.shape[1]), F32)
    return jnp.concatenate([bias_t, pad], axis=0).T


def _moba_kernel(q_ref, k_ref, v_ref, o_ref, km_scr, *, n_blocks, heads, per_step):
    blk = MOBA_BLOCK

    @pl.when(pl.program_id(2) == 0)
    def _():
        for hh in range(heads):
            for n in range(n_blocks):
                kb = _head(k_ref, hh, _tile_rows(n, blk)).astype(F32)
                km_scr[hh, n:n + 1, :] = jnp.mean(kb, axis=0, keepdims=True)

    def scores(ii, hh):
        length = (ii + 1) * blk
        q_raw = _head(q_ref, hh, _tile_rows(ii, blk))
        q = (q_raw.astype(F32) * (QK_SCALE * LOG2E)).astype(BF16)
        s = _dot_nt(q, _head(k_ref, hh, slice(0, length)))
        if ii <= MOBA_TOPK:
            return s
        bias = _moba_block_bias(q_raw, km_scr[hh], ii)
        past = [s[:, _tile_rows(j, blk)] + bias[:, j:j + 1] for j in range(ii)]
        return jnp.concatenate(past + [s[:, _tile_rows(ii, blk)]], axis=1)

    _attend(scores, v_ref, o_ref, n_tiles=n_blocks, per_step=per_step, tq=blk, heads=heads)


def _moba(qk3, p3, batch, seq, casts):
    n_blocks = seq // MOBA_BLOCK
    assert seq % MOBA_BLOCK == 0 and n_blocks <= LANES
    t = _tiles()
    heads, per_step = t["attn_heads"], t["attn_tiles_per_step"]
    hw = heads * HEAD_DIM
    groups = N_HEADS // heads
    (o,), cast = _call_with_casts(
        functools.partial(_moba_kernel, n_blocks=n_blocks, heads=heads, per_step=per_step), casts,
        (qk3, qk3, p3),
        out_shape=(jax.ShapeDtypeStruct((batch, seq, WIDTH), BF16),),
        grid=(batch, groups, n_blocks // per_step),
        in_specs=[
            pl.BlockSpec((1, seq, hw), lambda b, g, i: (b, 0, g)),
            pl.BlockSpec((1, seq, hw), lambda b, g, i: (b, 0, groups + g)),
            pl.BlockSpec((1, seq, hw), lambda b, g, i: (b, 0, 3 * groups + g)),
        ],
        out_specs=(pl.BlockSpec((1, seq, hw), lambda b, g, i: (b, 0, g)),),
        scratch_shapes=[pltpu.VMEM((heads, n_blocks, HEAD_DIM), F32)],
        compiler_params=_params(("arbitrary", "arbitrary", "arbitrary")),
        name="moba_attention",
    )
    return o, cast


def _mix_kernel(oa_ref, ob_ref, wa_ref, wb_ref, ga_ref, gb_ref, y_ref):
    ya = jnp.dot(oa_ref[...], wa_ref[...], preferred_element_type=F32)
    yb = jnp.dot(ob_ref[...], wb_ref[...], preferred_element_type=F32)
    ga = jax.nn.sigmoid(ga_ref[...].astype(F32))
    gb = jax.nn.sigmoid(gb_ref[...].astype(F32))
    y_ref[...] = (ga * ya + gb * yb).astype(BF16)


def _mix(oa, ob, wa, wb, p2):
    t = _tiles()
    tm, tn = t["mix_tm"], t["mix_tn"]
    rows, kdim = oa.shape
    d = wa.shape[1]
    ga_blk = (4 * WIDTH) // tn
    gb_blk = (4 * WIDTH + d) // tn
    return pl.pallas_call(
        _mix_kernel,
        out_shape=jax.ShapeDtypeStruct((rows, d), BF16),
        grid=(rows // tm, d // tn),
        in_specs=[
            pl.BlockSpec((tm, kdim), lambda i, j: (i, 0)),
            pl.BlockSpec((tm, kdim), lambda i, j: (i, 0)),
            pl.BlockSpec((kdim, tn), lambda i, j: (0, j)),
            pl.BlockSpec((kdim, tn), lambda i, j: (0, j)),
            pl.BlockSpec((tm, tn), lambda i, j: (i, ga_blk + j)),
            pl.BlockSpec((tm, tn), lambda i, j: (i, gb_blk + j)),
        ],
        out_specs=pl.BlockSpec((tm, tn), lambda i, j: (i, j)),
        compiler_params=_params(("parallel", "parallel")),
        name="gated_mix",
    )(oa, ob, wa, wb, p2, p2)


def _resid_kernel(x_ref, y_ref, w_ref, g_ref, x1_ref, h_ref):
    x1 = x_ref[...] + jnp.dot(y_ref[...], w_ref[...], preferred_element_type=F32)
    x1_ref[...] = x1
    h_ref[...] = _rms(x1, g_ref[...]).astype(BF16)


def _resid(x2, y, w_out, g):
    tm = _tiles()["resid_tm"]
    rows, d = x2.shape
    return pl.pallas_call(
        _resid_kernel,
        out_shape=(jax.ShapeDtypeStruct((rows, d), F32),
                   jax.ShapeDtypeStruct((rows, d), BF16)),
        grid=(rows // tm,),
        in_specs=[
            pl.BlockSpec((tm, d), lambda i: (i, 0)),
            pl.BlockSpec((tm, d), lambda i: (i, 0)),
            pl.BlockSpec((d, d), lambda i: (0, 0)),
            pl.BlockSpec((1, d), lambda i: (0, 0)),
        ],
        out_specs=(pl.BlockSpec((tm, d), lambda i: (i, 0)),
                   pl.BlockSpec((tm, d), lambda i: (i, 0))),
        compiler_params=_params(("parallel",)),
        name="out_residual",
    )(x2, y, w_out, g)


def _ffn_kernel(h_ref, x1_ref, wg_ref, wu_ref, wd_ref, gf_ref, o_ref):
    k = pl.program_id(1)

    @pl.when(k == 0)
    def _():
        o_ref[...] = x1_ref[...]

    h = h_ref[...]
    gate = jnp.dot(h, wg_ref[...], preferred_element_type=F32)
    up = jnp.dot(h, wu_ref[...], preferred_element_type=F32)
    a = (gate * jax.nn.sigmoid(gate) * up).astype(BF16)
    o_ref[...] += jnp.dot(a, wd_ref[...], preferred_element_type=F32)

    @pl.when(k == pl.num_programs(1) - 1)
    def _():
        o_ref[...] = _rms(o_ref[...], gf_ref[...])


def _ffn(h2, x1, wg, wu, wd, gf):
    t = _tiles()
    tm, tf = t["ffn_tm"], t["ffn_tf"]
    rows, d = h2.shape
    dff = wd.shape[0]
    return pl.pallas_call(
        _ffn_kernel,
        out_shape=jax.ShapeDtypeStruct((rows, d), F32),
        grid=(rows // tm, dff // tf),
        in_specs=[
            pl.BlockSpec((tm, d), lambda i, k: (i, 0)),
            pl.BlockSpec((tm, d), lambda i, k: (i, 0)),
            pl.BlockSpec((d, tf), lambda i, k: (0, k)),
            pl.BlockSpec((d, tf), lambda i, k: (0, k)),
            pl.BlockSpec((tf, d), lambda i, k: (k, 0)),
            pl.BlockSpec((1, d), lambda i, k: (0, 0)),
        ],
        out_specs=pl.BlockSpec((tm, d), lambda i, k: (i, 0)),
        compiler_params=_params(("parallel", "arbitrary")),
        name="swiglu_ffn",
    )(h2, x1, wg, wu, wd, gf)


def _rope_tables(seq):
    half = HEAD_DIM // 2
    inv_freq = ROPE_THETA ** (-jnp.arange(half, dtype=F32) / half)
    ang = jnp.arange(seq, dtype=F32)[:, None] * inv_freq[None, :]
    cos, sin = jnp.cos(ang), jnp.sin(ang)
    return jnp.concatenate([cos, cos], axis=-1), jnp.concatenate([-sin, sin], axis=-1)


def kernel(x, norm_mix, w_in, b_forget, w_o_fox, w_o_moba, w_out, norm_ffn, w_gate_up, w_down,
           norm_final):
    batch, seq, d = x.shape
    assert w_in.shape[0] == 1
    cs, sn = _rope_tables(seq)
    x2 = x.reshape(batch * seq, d)
    wt = w_in[0].T
    b_pad = jnp.pad(b_forget[0][None, :], ((0, 0), (0, LANES - N_HEADS)))

    dff = w_down.shape[1]
    h, f_pad = _norm(x2, norm_mix[0][None, :], wt)
    p2, (wo_a, wo_b, w_res) = _proj_plain(
        h, wt, [_Cast(w_o_fox[0], d), _Cast(w_o_moba[0], d), _Cast(w_out[0], d)])
    qk, (wd,) = _proj_rot(h, wt, cs, sn, seq, [_Cast(w_down[0], d)])
    c = _forget_cumsum(f_pad, b_pad, batch, seq)
    p3 = p2.reshape(batch, seq, p2.shape[1])
    o_a, (wg,) = _fox(p3, c.reshape(batch, N_HEADS, 1, seq), batch, seq,
                      [_Cast(w_gate_up[0], dff, 0)])
    o_b, (wu,) = _moba(qk.reshape(batch, seq, qk.shape[1]), p3, batch, seq,
                       [_Cast(w_gate_up[0], dff, 1)])
    y = _mix(o_a.reshape(batch * seq, WIDTH), o_b.reshape(batch * seq, WIDTH), wo_a, wo_b, p2)
    x1, h2 = _resid(x2, y, w_res, norm_ffn[0][None, :])
    out = _ffn(h2, x1, wg, wu, wd, norm_final[None, :])
    return out.reshape(batch, seq, d)
```

```python
import functools
import math
from typing import NamedTuple

import jax
import jax.numpy as jnp
from jax import lax
from jax.experimental import pallas as pl
from jax.experimental.pallas import tpu as pltpu

F32 = jnp.float32
BF16 = jnp.bfloat16

HEAD_DIM = 128
N_HEADS = 8
WIDTH = N_HEADS * HEAD_DIM
MOBA_BLOCK = 256
MOBA_TOPK = 3
ROPE_THETA = 10000.0
RMS_EPS = 1e-6
LOG2E = math.log2(math.e)
QK_SCALE = HEAD_DIM ** -0.5
LANES = 128
BF16_SUBLANES = 16
MASK_BIAS = -1e30
SCORE_GROUP = 8
VMEM_LIMIT = 60 * 1024 * 1024


def _tiles():
    return dict(
        norm_tm=512, proj_tm=2048, rot_tm=2048, proj_tn=1024,
        attn_tq=256, attn_heads=4, attn_tiles_per_step=2,
        mix_tm=512,
        ffn_tm=1024, ffn_tf=512,
        cum_chunk=256,
    )


def _params(sem):
    return pltpu.CompilerParams(dimension_semantics=sem, vmem_limit_bytes=VMEM_LIMIT)


def _rms(x, g):
    ms = jnp.mean(x * x, axis=-1, keepdims=True)
    return x * lax.rsqrt(ms + RMS_EPS) * g


def _dot_nt(a, b):
    return lax.dot_general(a, b, (((1,), (1,)), ((), ())), preferred_element_type=F32)


class _Cast(NamedTuple):
    src: jax.Array
    cols: int
    col_block: int = 0


def _call_with_casts(kernel_fn, casts, args, *, grid, in_specs, out_specs, out_shape, **kw):
    steps = math.prod(grid)
    strides = [math.prod(grid[a + 1:]) for a in range(len(grid))]

    def step_of(*g):
        return sum(gi * st for gi, st in zip(g, strides))

    n_in, n_out, n_side = len(in_specs), len(out_shape), len(casts)
    side_in, side_out, side_shape = [], [], []
    for c in casts:
        rows = c.src.shape[0] // steps
        assert rows * steps == c.src.shape[0] and rows % BF16_SUBLANES == 0
        side_in.append(pl.BlockSpec((rows, c.cols), lambda *g, c=c: (step_of(*g), c.col_block)))
        side_out.append(pl.BlockSpec((rows, c.cols), lambda *g: (step_of(*g), 0)))
        side_shape.append(jax.ShapeDtypeStruct((c.src.shape[0], c.cols), BF16))

    def body(*refs):
        src = refs[n_in:n_in + n_side]
        dst = refs[n_in + n_side + n_out:n_in + 2 * n_side + n_out]
        for s_ref, d_ref in zip(src, dst):
            d_ref[...] = s_ref[...].astype(BF16)
        kernel_fn(*refs[:n_in], *refs[n_in + n_side:n_in + n_side + n_out],
                  *refs[n_in + 2 * n_side + n_out:])

    res = pl.pallas_call(
        body, grid=grid, in_specs=[*in_specs, *side_in], out_specs=(*out_specs, *side_out),
        out_shape=(*out_shape, *side_shape), **kw)(*args, *[c.src for c in casts])
    return res[:n_out], res[n_out:]


def _split_bf16(v):
    hi = v.astype(BF16)
    return hi, (v - hi.astype(F32)).astype(BF16)


def _norm_kernel(x_ref, g_ref, wf_ref, h_ref, f_ref):
    h = _rms(x_ref[...], g_ref[...]).astype(BF16)
    h_ref[...] = h
    f_ref[...] = _dot_nt(h, wf_ref[...].astype(BF16))


def _norm(x2, g, wt):
    tm = _tiles()["norm_tm"]
    rows, d = x2.shape
    f_blk = (6 * WIDTH) // LANES
    return pl.pallas_call(
        _norm_kernel,
        out_shape=(jax.ShapeDtypeStruct((rows, d), BF16),
                   jax.ShapeDtypeStruct((rows, LANES), F32)),
        grid=(rows // tm,),
        in_specs=[
            pl.BlockSpec((tm, d), lambda i: (i, 0)),
            pl.BlockSpec((1, d), lambda i: (0, 0)),
            pl.BlockSpec((LANES, d), lambda i: (f_blk, 0)),
        ],
        out_specs=(pl.BlockSpec((tm, d), lambda i: (i, 0)),
                   pl.BlockSpec((tm, LANES), lambda i: (i, 0))),
        compiler_params=_params(("parallel",)),
        name="pre_norm",
    )(x2, g, wt)


def _cast_weight_tile(w_ref, w2_ref, w_scr, shift):
    if shift == 0:
        w_scr[...] = w_ref[...].astype(BF16)
    else:
        w_scr[...] = jnp.concatenate([w_ref[shift:, :], w2_ref[...]], axis=0).astype(BF16)


def _proj_kernel(h_ref, w_ref, w2_ref, o_ref, w_scr, *, n_aligned):
    j = pl.program_id(0)

    @pl.when(jnp.logical_and(pl.program_id(1) == 0, j < n_aligned))
    def _():
        _cast_weight_tile(w_ref, w2_ref, w_scr, 0)

    @pl.when(jnp.logical_and(pl.program_id(1) == 0, j >= n_aligned))
    def _():
        _cast_weight_tile(w_ref, w2_ref, w_scr, N_HEADS)

    o_ref[...] = _dot_nt(h_ref[...], w_scr[...]).astype(BF16)


def _proj_rot_kernel(h_ref, w_ref, cs_ref, sn_ref, o_ref, w_scr):
    @pl.when(pl.program_id(1) == 0)
    def _():
        _cast_weight_tile(w_ref, None, w_scr, 0)

    acc = _dot_nt(h_ref[...], w_scr[...])
    cs = cs_ref[...]
    sn = sn_ref[...]
    for hh in range(acc.shape[1] // HEAD_DIM):
        t = acc[:, hh * HEAD_DIM:(hh + 1) * HEAD_DIM]
        r = pltpu.roll(t, HEAD_DIM // 2, axis=1)
        o_ref[:, hh * HEAD_DIM:(hh + 1) * HEAD_DIM] = (t * cs + r * sn).astype(BF16)


def _proj_plain(h, wt, casts):
    t = _tiles()
    tm, tn = t["proj_tm"], t["proj_tn"]
    rows, d = h.shape
    per_group = WIDTH // tn
    n_aligned = 4 * per_group
    n_gate = (2 * d) // tn
    gate0 = (6 * WIDTH) // tn

    def w_blk(j, i):
        aligned = jnp.where(j < 3 * per_group, j, j + 2 * per_group)
        return jnp.where(j < n_aligned, aligned, gate0 + (j - n_aligned)), 0

    def w2_blk(j, i):
        nxt = jnp.maximum(j - n_aligned, 0) + 1
        return (gate0 + nxt) * (tn // N_HEADS), 0

    (p2,), cast = _call_with_casts(
        functools.partial(_proj_kernel, n_aligned=n_aligned), casts, (h, wt, wt),
        out_shape=(jax.ShapeDtypeStruct((rows, (n_aligned + n_gate) * tn), BF16),),
        grid=(n_aligned + n_gate, rows // tm),
        in_specs=[
            pl.BlockSpec((tm, d), lambda j, i: (i, 0)),
            pl.BlockSpec((tn, d), w_blk),
            pl.BlockSpec((N_HEADS, d), w2_blk),
        ],
        out_specs=(pl.BlockSpec((tm, tn), lambda j, i: (i, j)),),
        scratch_shapes=[pltpu.VMEM((tn, d), BF16)],
        compiler_params=_params(("arbitrary", "arbitrary")),
        name="proj_plain",
    )
    return p2, cast


def _proj_rot(h, wt, cs, sn, seq, casts):
    t = _tiles()
    tm, tn = t["rot_tm"], t["proj_tn"]
    rows, d = h.shape
    first = (3 * WIDTH) // tn
    pos_blocks = seq // tm
    (qk,), cast = _call_with_casts(
        _proj_rot_kernel, casts, (h, wt, cs, sn),
        out_shape=(jax.ShapeDtypeStruct((rows, 2 * WIDTH), BF16),),
        grid=((2 * WIDTH) // tn, rows // tm),
        in_specs=[
            pl.BlockSpec((tm, d), lambda j, i: (i, 0)),
            pl.BlockSpec((tn, d), lambda j, i: (first + j, 0)),
            pl.BlockSpec((tm, HEAD_DIM), lambda j, i: (i % pos_blocks, 0)),
            pl.BlockSpec((tm, HEAD_DIM), lambda j, i: (i % pos_blocks, 0)),
        ],
        out_specs=(pl.BlockSpec((tm, tn), lambda j, i: (i, j)),),
        scratch_shapes=[pltpu.VMEM((tn, d), BF16)],
        compiler_params=_params(("arbitrary", "arbitrary")),
        name="proj_rotary",
    )
    return qk, cast


def _forget_kernel(f_ref, b_ref, tri_ref, c_ref, *, chunk):
    z = f_ref[...] + b_ref[...]
    lf = jnp.minimum(z, 0.0) - jnp.log1p(jnp.exp(-jnp.abs(z)))
    lft = lf.T[:N_HEADS, :]
    tri = tri_ref[...]
    carry = jnp.zeros((N_HEADS, 1), F32)
    for n in range(lft.shape[1] // chunk):
        xc = lft[:, n * chunk:(n + 1) * chunk]
        hi = xc.astype(BF16)
        r1 = xc - hi.astype(F32)
        mid = r1.astype(BF16)
        lo = (r1 - mid.astype(F32)).astype(BF16)
        loc = (jnp.dot(hi, tri, preferred_element_type=F32)
               + jnp.dot(mid, tri, preferred_element_type=F32)
               + jnp.dot(lo, tri, preferred_element_type=F32)) + carry
        c_ref[0, :, n * chunk:(n + 1) * chunk] = loc
        carry = loc[:, chunk - 1:chunk]


def _forget_cumsum(f_pad, b_pad, batch, seq):
    chunk = _tiles()["cum_chunk"]
    r = lax.broadcasted_iota(jnp.int32, (chunk, chunk), 0)
    c = lax.broadcasted_iota(jnp.int32, (chunk, chunk), 1)
    tri = (r <= c).astype(BF16)
    return pl.pallas_call(
        functools.partial(_forget_kernel, chunk=chunk),
        out_shape=jax.ShapeDtypeStruct((batch, N_HEADS, seq), F32),
        grid=(batch,),
        in_specs=[
            pl.BlockSpec((seq, LANES), lambda b: (b, 0)),
            pl.BlockSpec((1, LANES), lambda b: (0, 0)),
            pl.BlockSpec((chunk, chunk), lambda b: (0, 0)),
        ],
        out_specs=pl.BlockSpec((1, N_HEADS, seq), lambda b: (b, 0, 0)),
        compiler_params=_params(("parallel",)),
        name="forget_cumsum",
    )(f_pad, b_pad, tri)


def _head(ref, hh, rows=slice(None)):
    return ref[0, rows, hh * HEAD_DIM:(hh + 1) * HEAD_DIM]


def _causal_softmax_pv(s, v, tq):
    length = s.shape[1]
    row = lax.broadcasted_iota(jnp.int32, (tq, tq), 0)
    col = lax.broadcasted_iota(jnp.int32, (tq, tq), 1)
    own = jnp.where(col <= row, s[:, length - tq:], -jnp.inf)
    s = own if length == tq else jnp.concatenate([s[:, :length - tq], own], axis=1)
    m = jnp.max(s, axis=-1, keepdims=True)
    p = jnp.exp2(s - m).astype(BF16)
    v1 = jnp.concatenate([v, jnp.ones(v.shape, v.dtype)], axis=1)
    acc = jnp.dot(p, v1, preferred_element_type=F32)
    return (acc[:, :HEAD_DIM] / acc[:, HEAD_DIM:]).astype(BF16)


def _tile_rows(ii, tq):
    return slice(ii * tq, (ii + 1) * tq)


def _attend(scores, v_ref, o_ref, *, n_tiles, per_step, tq, heads):
    def run(tiles):
        jobs = [(ii, hh) for ii in tiles for hh in range(heads)]
        for j0 in range(0, len(jobs), SCORE_GROUP):
            group = jobs[j0:j0 + SCORE_GROUP]
            ss = [scores(ii, hh) for ii, hh in group]
            for (ii, hh), s in zip(group, ss):
                o_ref[0, _tile_rows(ii, tq), hh * HEAD_DIM:(hh + 1) * HEAD_DIM] = (
                    _causal_softmax_pv(s, _head(v_ref, hh, slice(0, (ii + 1) * tq)), tq))

    n_steps = n_tiles // per_step
    assert per_step in (1, 2) and n_steps * per_step == n_tiles
    step = pl.program_id(2)
    for pp in range(n_steps):
        tiles = [pp] if per_step == 1 else [pp, n_tiles - 1 - pp]
        pl.when(step == pp)(functools.partial(run, tiles))


def _fox_kernel(q_ref, k_ref, v_ref, c_ref, o_ref, *, tq, heads, n_tiles, per_step):
    def scores(ii, hh):
        length = (ii + 1) * tq
        q = (_head(q_ref, hh, _tile_rows(ii, tq)).astype(F32) * (QK_SCALE * LOG2E)).astype(BF16)
        c = c_ref[0, hh]
        bias = (c[:, ii * tq:ii * tq + 1] - c[:, :length]) * LOG2E
        return _dot_nt(q, _head(k_ref, hh, slice(0, length))) + bias

    _attend(scores, v_ref, o_ref, n_tiles=n_tiles, per_step=per_step, tq=tq, heads=heads)


def _fox(p3, c4, batch, seq, casts):
    t = _tiles()
    tq, heads, per_step = t["attn_tq"], t["attn_heads"], t["attn_tiles_per_step"]
    hw = heads * HEAD_DIM
    groups = N_HEADS // heads
    n_tiles = seq // tq
    (o,), cast = _call_with_casts(
        functools.partial(_fox_kernel, tq=tq, heads=heads, n_tiles=n_tiles, per_step=per_step),
        casts, (p3, p3, p3, c4),
        out_shape=(jax.ShapeDtypeStruct((batch, seq, WIDTH), BF16),),
        grid=(batch, groups, n_tiles // per_step),
        in_specs=[
            pl.BlockSpec((1, seq, hw), lambda b, g, i: (b, 0, g)),
            pl.BlockSpec((1, seq, hw), lambda b, g, i: (b, 0, groups + g)),
            pl.BlockSpec((1, seq, hw), lambda b, g, i: (b, 0, 2 * groups + g)),
            pl.BlockSpec((1, heads, 1, seq), lambda b, g, i: (b, g, 0, 0)),
        ],
        out_specs=(pl.BlockSpec((1, seq, hw), lambda b, g, i: (b, 0, g)),),
        compiler_params=_params(("arbitrary", "arbitrary", "arbitrary")),
        name="fox_attention",
    )
    return o, cast


def _moba_block_bias(q_raw, km, ii):
    km_hi, km_lo = _split_bf16(km)
    gate = _dot_nt(km_hi, q_raw) + _dot_nt(km_lo, q_raw)
    block = lax.broadcasted_iota(jnp.int32, gate.shape, 0)
    g = jnp.where(block < ii, gate, -jnp.inf)
    rank = jnp.zeros(gate.shape, jnp.int32)
    for n in range(ii):
        gn = g[n:n + 1, :]
        ahead = jnp.logical_or(gn > g, jnp.logical_and(gn == g, n < block))
        rank = rank + ahead.astype(jnp.int32)
    bias_t = jnp.where(rank < MOBA_TOPK, 0.0, MASK_BIAS)
    pad = jnp.zeros((LANES - bias_t.shape[0], bias_t.shape[1]), F32)
    return jnp.concatenate([bias_t, pad], axis=0).T


def _moba_kernel(q_ref, k_ref, v_ref, o_ref, km_scr, *, n_blocks, heads, per_step):
    blk = MOBA_BLOCK

    @pl.when(pl.program_id(2) == 0)
    def _():
        for hh in range(heads):
            for n in range(n_blocks):
                kb = _head(k_ref, hh, _tile_rows(n, blk)).astype(F32)
                km_scr[hh, n:n + 1, :] = jnp.mean(kb, axis=0, keepdims=True)

    def scores(ii, hh):
        length = (ii + 1) * blk
        q_raw = _head(q_ref, hh, _tile_rows(ii, blk))
        q = (q_raw.astype(F32) * (QK_SCALE * LOG2E)).astype(BF16)
        s = _dot_nt(q, _head(k_ref, hh, slice(0, length)))
        if ii <= MOBA_TOPK:
            return s
        bias = _moba_block_bias(q_raw, km_scr[hh], ii)
        past = [s[:, _tile_rows(j, blk)] + bias[:, j:j + 1] for j in range(ii)]
        return jnp.concatenate(past + [s[:, _tile_rows(ii, blk)]], axis=1)

    _attend(scores, v_ref, o_ref, n_tiles=n_blocks, per_step=per_step, tq=blk, heads=heads)


def _moba(qk3, p3, batch, seq, casts):
    n_blocks = seq // MOBA_BLOCK
    assert seq % MOBA_BLOCK == 0 and n_blocks <= LANES
    t = _tiles()
    heads, per_step = t["attn_heads"], t["attn_tiles_per_step"]
    hw = heads * HEAD_DIM
    groups = N_HEADS // heads
    (o,), cast = _call_with_casts(
        functools.partial(_moba_kernel, n_blocks=n_blocks, heads=heads, per_step=per_step), casts,
        (qk3, qk3, p3),
        out_shape=(jax.ShapeDtypeStruct((batch, seq, WIDTH), BF16),),
        grid=(batch, groups, n_blocks // per_step),
        in_specs=[
            pl.BlockSpec((1, seq, hw), lambda b, g, i: (b, 0, g)),
            pl.BlockSpec((1, seq, hw), lambda b, g, i: (b, 0, groups + g)),
            pl.BlockSpec((1, seq, hw), lambda b, g, i: (b, 0, 3 * groups + g)),
        ],
        out_specs=(pl.BlockSpec((1, seq, hw), lambda b, g, i: (b, 0, g)),),
        scratch_shapes=[pltpu.VMEM((heads, n_blocks, HEAD_DIM), F32)],
        compiler_params=_params(("arbitrary", "arbitrary", "arbitrary")),
        name="moba_attention",
    )
    return o, cast


def _mix_resid_kernel(oa_ref, ob_ref, ga_ref, gb_ref, x_ref, wa_ref, wb_ref, wo_ref, g_ref,
                      x1_ref, h_ref):
    ya = jnp.dot(oa_ref[...], wa_ref[...], preferred_element_type=F32)
    yb = jnp.dot(ob_ref[...], wb_ref[...], preferred_element_type=F32)
    ga = jax.nn.sigmoid(ga_ref[...].astype(F32))
    gb = jax.nn.sigmoid(gb_ref[...].astype(F32))
    y = (ga * ya + gb * yb).astype(BF16)
    x1 = x_ref[...] + jnp.dot(y, wo_ref[...], preferred_element_type=F32)
    x1_ref[...] = x1
    h_ref[...] = _rms(x1, g_ref[...]).astype(BF16)


def _mix_resid(oa, ob, p2, x2, wa, wb, wo, g):
    tm = _tiles()["mix_tm"]
    rows, kdim = oa.shape
    d = wa.shape[1]
    ga_blk = (4 * WIDTH) // d
    resident = pl.Buffered(1)
    return pl.pallas_call(
        _mix_resid_kernel,
        out_shape=(jax.ShapeDtypeStruct((rows, d), F32),
                   jax.ShapeDtypeStruct((rows, d), BF16)),
        grid=(rows // tm,),
        in_specs=[
            pl.BlockSpec((tm, kdim), lambda i: (i, 0)),
            pl.BlockSpec((tm, kdim), lambda i: (i, 0)),
            pl.BlockSpec((tm, d), lambda i: (i, ga_blk)),
            pl.BlockSpec((tm, d), lambda i: (i, ga_blk + 1)),
            pl.BlockSpec((tm, d), lambda i: (i, 0)),
            pl.BlockSpec((kdim, d), lambda i: (0, 0), pipeline_mode=resident),
            pl.BlockSpec((kdim, d), lambda i: (0, 0), pipeline_mode=resident),
            pl.BlockSpec((d, d), lambda i: (0, 0), pipeline_mode=resident),
            pl.BlockSpec((1, d), lambda i: (0, 0)),
        ],
        out_specs=(pl.BlockSpec((tm, d), lambda i: (i, 0)),
                   pl.BlockSpec((tm, d), lambda i: (i, 0))),
        compiler_params=_params(("parallel",)),
        name="mix_residual",
    )(oa, ob, p2, p2, x2, wa, wb, wo, g)


def _ffn_kernel(h_ref, x1_ref, wg_ref, wu_ref, wd_ref, gf_ref, o_ref):
    k = pl.program_id(1)

    @pl.when(k == 0)
    def _():
        o_ref[...] = x1_ref[...]

    h = h_ref[...]
    gate = jnp.dot(h, wg_ref[...], preferred_element_type=F32)
    up = jnp.dot(h, wu_ref[...], preferred_element_type=F32)
    a = (gate * jax.nn.sigmoid(gate) * up).astype(BF16)
    o_ref[...] += jnp.dot(a, wd_ref[...], preferred_element_type=F32)

    @pl.when(k == pl.num_programs(1) - 1)
    def _():
        o_ref[...] = _rms(o_ref[...], gf_ref[...])


def _ffn(h2, x1, wg, wu, wd, gf):
    t = _tiles()
    tm, tf = t["ffn_tm"], t["ffn_tf"]
    rows, d = h2.shape
    dff = wd.shape[0]
    return pl.pallas_call(
        _ffn_kernel,
        out_shape=jax.ShapeDtypeStruct((rows, d), F32),
        grid=(rows // tm, dff // tf),
        in_specs=[
            pl.BlockSpec((tm, d), lambda i, k: (i, 0)),
            pl.BlockSpec((tm, d), lambda i, k: (i, 0)),
            pl.BlockSpec((d, tf), lambda i, k: (0, k)),
            pl.BlockSpec((d, tf), lambda i, k: (0, k)),
            pl.BlockSpec((tf, d), lambda i, k: (k, 0)),
            pl.BlockSpec((1, d), lambda i, k: (0, 0)),
        ],
        out_specs=pl.BlockSpec((tm, d), lambda i, k: (i, 0)),
        compiler_params=_params(("parallel", "arbitrary")),
        name="swiglu_ffn",
    )(h2, x1, wg, wu, wd, gf)


def _rope_tables(seq):
    half = HEAD_DIM // 2
    inv_freq = ROPE_THETA ** (-jnp.arange(half, dtype=F32) / half)
    ang = jnp.arange(seq, dtype=F32)[:, None] * inv_freq[None, :]
    cos, sin = jnp.cos(ang), jnp.sin(ang)
    return jnp.concatenate([cos, cos], axis=-1), jnp.concatenate([-sin, sin], axis=-1)


def kernel(x, norm_mix, w_in, b_forget, w_o_fox, w_o_moba, w_out, norm_ffn, w_gate_up, w_down,
           norm_final):
    batch, seq, d = x.shape
    assert w_in.shape[0] == 1
    cs, sn = _rope_tables(seq)
    x2 = x.reshape(batch * seq, d)
    wt = w_in[0].T
    b_pad = jnp.pad(b_forget[0][None, :], ((0, 0), (0, LANES - N_HEADS)))

    dff = w_down.shape[1]
    h, f_pad = _norm(x2, norm_mix[0][None, :], wt)
    p2, (wo_a, wo_b, w_res) = _proj_plain(
        h, wt, [_Cast(w_o_fox[0], d), _Cast(w_o_moba[0], d), _Cast(w_out[0], d)])
    qk, _ = _proj_rot(h, wt, cs, sn, seq, [])
    c = _forget_cumsum(f_pad, b_pad, batch, seq)
    p3 = p2.reshape(batch, seq, p2.shape[1])
    o_a, (wg, wd) = _fox(p3, c.reshape(batch, N_HEADS, 1, seq), batch, seq,
                         [_Cast(w_gate_up[0], dff, 0), _Cast(w_down[0], d)])
    o_b, (wu,) = _moba(qk.reshape(batch, seq, qk.shape[1]), p3, batch, seq,
                       [_Cast(w_gate_up[0], dff, 1)])
    x1, h2 = _mix_resid(o_a.reshape(batch * seq, WIDTH), o_b.reshape(batch * seq, WIDTH), p2, x2,
                        wo_a, wo_b, w_res, norm_ffn[0][None, :])
    out = _ffn(h2, x1, wg, wu, wd, norm_final[None, :])
    return out.reshape(batch, seq, d)
```

```python
import functools
import math
from typing import NamedTuple

import jax
import jax.numpy as jnp
from jax import lax
from jax.experimental import pallas as pl
from jax.experimental.pallas import tpu as pltpu

F32 = jnp.float32
BF16 = jnp.bfloat16

HEAD_DIM = 128
N_HEADS = 8
WIDTH = N_HEADS * HEAD_DIM
MOBA_BLOCK = 256
MOBA_TOPK = 3
ROPE_THETA = 10000.0
RMS_EPS = 1e-6
LOG2E = math.log2(math.e)
QK_SCALE = HEAD_DIM ** -0.5
LANES = 128
BF16_SUBLANES = 16
MASK_BIAS = -1e30
SCORE_GROUP = 8
VMEM_LIMIT = 60 * 1024 * 1024


def _tiles():
    return dict(
        norm_tm=512, proj_tm=2048, rot_tm=2048, proj_tn=1024,
        attn_tq=256, attn_heads=4, attn_tiles_per_step=4,
        mix_tm=512,
        ffn_tm=1024, ffn_tf=512,
        cum_chunk=256,
    )


def _params(sem):
    return pltpu.CompilerParams(dimension_semantics=sem, vmem_limit_bytes=VMEM_LIMIT)


def _rms(x, g):
    ms = jnp.mean(x * x, axis=-1, keepdims=True)
    return x * lax.rsqrt(ms + RMS_EPS) * g


def _dot_nt(a, b):
    return lax.dot_general(a, b, (((1,), (1,)), ((), ())), preferred_element_type=F32)


class _Cast(NamedTuple):
    src: jax.Array
    cols: int
    col_block: int = 0


def _call_with_casts(kernel_fn, casts, args, *, grid, in_specs, out_specs, out_shape, **kw):
    steps = math.prod(grid)
    strides = [math.prod(grid[a + 1:]) for a in range(len(grid))]

    def step_of(*g):
        return sum(gi * st for gi, st in zip(g, strides))

    n_in, n_out, n_side = len(in_specs), len(out_shape), len(casts)
    side_in, side_out, side_shape = [], [], []
    for c in casts:
        rows = c.src.shape[0] // steps
        assert rows * steps == c.src.shape[0] and rows % BF16_SUBLANES == 0
        side_in.append(pl.BlockSpec((rows, c.cols), lambda *g, c=c: (step_of(*g), c.col_block)))
        side_out.append(pl.BlockSpec((rows, c.cols), lambda *g: (step_of(*g), 0)))
        side_shape.append(jax.ShapeDtypeStruct((c.src.shape[0], c.cols), BF16))

    def body(*refs):
        src = refs[n_in:n_in + n_side]
        dst = refs[n_in + n_side + n_out:n_in + 2 * n_side + n_out]
        for s_ref, d_ref in zip(src, dst):
            d_ref[...] = s_ref[...].astype(BF16)
        kernel_fn(*refs[:n_in], *refs[n_in + n_side:n_in + n_side + n_out],
                  *refs[n_in + 2 * n_side + n_out:])

    res = pl.pallas_call(
        body, grid=grid, in_specs=[*in_specs, *side_in], out_specs=(*out_specs, *side_out),
        out_shape=(*out_shape, *side_shape), **kw)(*args, *[c.src for c in casts])
    return res[:n_out], res[n_out:]


def _split_bf16(v):
    hi = v.astype(BF16)
    return hi, (v - hi.astype(F32)).astype(BF16)


def _norm_kernel(x_ref, g_ref, wf_ref, h_ref, f_ref):
    h = _rms(x_ref[...], g_ref[...]).astype(BF16)
    h_ref[...] = h
    f_ref[...] = _dot_nt(h, wf_ref[...].astype(BF16))


def _norm(x2, g, wt):
    tm = _tiles()["norm_tm"]
    rows, d = x2.shape
    f_blk = (6 * WIDTH) // LANES
    return pl.pallas_call(
        _norm_kernel,
        out_shape=(jax.ShapeDtypeStruct((rows, d), BF16),
                   jax.ShapeDtypeStruct((rows, LANES), F32)),
        grid=(rows // tm,),
        in_specs=[
            pl.BlockSpec((tm, d), lambda i: (i, 0)),
            pl.BlockSpec((1, d), lambda i: (0, 0)),
            pl.BlockSpec((LANES, d), lambda i: (f_blk, 0)),
        ],
        out_specs=(pl.BlockSpec((tm, d), lambda i: (i, 0)),
                   pl.BlockSpec((tm, LANES), lambda i: (i, 0))),
        compiler_params=_params(("parallel",)),
        name="pre_norm",
    )(x2, g, wt)


def _cast_weight_tile(w_ref, w2_ref, w_scr, shift):
    if shift == 0:
        w_scr[...] = w_ref[...].astype(BF16)
    else:
        w_scr[...] = jnp.concatenate([w_ref[shift:, :], w2_ref[...]], axis=0).astype(BF16)


def _proj_kernel(h_ref, w_ref, w2_ref, o_ref, w_scr, *, n_aligned):
    j = pl.program_id(0)

    @pl.when(jnp.logical_and(pl.program_id(1) == 0, j < n_aligned))
    def _():
        _cast_weight_tile(w_ref, w2_ref, w_scr, 0)

    @pl.when(jnp.logical_and(pl.program_id(1) == 0, j >= n_aligned))
    def _():
        _cast_weight_tile(w_ref, w2_ref, w_scr, N_HEADS)

    o_ref[...] = _dot_nt(h_ref[...], w_scr[...]).astype(BF16)


def _proj_rot_kernel(h_ref, w_ref, cs_ref, sn_ref, o_ref, w_scr):
    @pl.when(pl.program_id(1) == 0)
    def _():
        _cast_weight_tile(w_ref, None, w_scr, 0)

    acc = _dot_nt(h_ref[...], w_scr[...])
    cs = cs_ref[...]
    sn = sn_ref[...]
    for hh in range(acc.shape[1] // HEAD_DIM):
        t = acc[:, hh * HEAD_DIM:(hh + 1) * HEAD_DIM]
        r = pltpu.roll(t, HEAD_DIM // 2, axis=1)
        o_ref[:, hh * HEAD_DIM:(hh + 1) * HEAD_DIM] = (t * cs + r * sn).astype(BF16)


def _proj_plain(h, wt, casts):
    t = _tiles()
    tm, tn = t["proj_tm"], t["proj_tn"]
    rows, d = h.shape
    per_group = WIDTH // tn
    n_aligned = 4 * per_group
    n_gate = (2 * d) // tn
    gate0 = (6 * WIDTH) // tn

    def w_blk(j, i):
        aligned = jnp.where(j < 3 * per_group, j, j + 2 * per_group)
        return jnp.where(j < n_aligned, aligned, gate0 + (j - n_aligned)), 0

    def w2_blk(j, i):
        nxt = jnp.maximum(j - n_aligned, 0) + 1
        return (gate0 + nxt) * (tn // N_HEADS), 0

    (p2,), cast = _call_with_casts(
        functools.partial(_proj_kernel, n_aligned=n_aligned), casts, (h, wt, wt),
        out_shape=(jax.ShapeDtypeStruct((rows, (n_aligned + n_gate) * tn), BF16),),
        grid=(n_aligned + n_gate, rows // tm),
        in_specs=[
            pl.BlockSpec((tm, d), lambda j, i: (i, 0)),
            pl.BlockSpec((tn, d), w_blk),
            pl.BlockSpec((N_HEADS, d), w2_blk),
        ],
        out_specs=(pl.BlockSpec((tm, tn), lambda j, i: (i, j)),),
        scratch_shapes=[pltpu.VMEM((tn, d), BF16)],
        compiler_params=_params(("arbitrary", "arbitrary")),
        name="proj_plain",
    )
    return p2, cast


def _proj_rot(h, wt, cs, sn, seq, casts):
    t = _tiles()
    tm, tn = t["rot_tm"], t["proj_tn"]
    rows, d = h.shape
    first = (3 * WIDTH) // tn
    pos_blocks = seq // tm
    (qk,), cast = _call_with_casts(
        _proj_rot_kernel, casts, (h, wt, cs, sn),
        out_shape=(jax.ShapeDtypeStruct((rows, 2 * WIDTH), BF16),),
        grid=((2 * WIDTH) // tn, rows // tm),
        in_specs=[
            pl.BlockSpec((tm, d), lambda j, i: (i, 0)),
            pl.BlockSpec((tn, d), lambda j, i: (first + j, 0)),
            pl.BlockSpec((tm, HEAD_DIM), lambda j, i: (i % pos_blocks, 0)),
            pl.BlockSpec((tm, HEAD_DIM), lambda j, i: (i % pos_blocks, 0)),
        ],
        out_specs=(pl.BlockSpec((tm, tn), lambda j, i: (i, j)),),
        scratch_shapes=[pltpu.VMEM((tn, d), BF16)],
        compiler_params=_params(("arbitrary", "arbitrary")),
        name="proj_rotary",
    )
    return qk, cast


def _forget_kernel(f_ref, b_ref, tri_ref, c_ref, *, chunk):
    z = f_ref[...] + b_ref[...]
    lf = jnp.minimum(z, 0.0) - jnp.log1p(jnp.exp(-jnp.abs(z)))
    lft = lf.T[:N_HEADS, :]
    tri = tri_ref[...]
    carry = jnp.zeros((N_HEADS, 1), F32)
    for n in range(lft.shape[1] // chunk):
        xc = lft[:, n * chunk:(n + 1) * chunk]
        hi = xc.astype(BF16)
        r1 = xc - hi.astype(F32)
        mid = r1.astype(BF16)
        lo = (r1 - mid.astype(F32)).astype(BF16)
        loc = (jnp.dot(hi, tri, preferred_element_type=F32)
               + jnp.dot(mid, tri, preferred_element_type=F32)
               + jnp.dot(lo, tri, preferred_element_type=F32)) + carry
        c_ref[0, :, n * chunk:(n + 1) * chunk] = loc
        carry = loc[:, chunk - 1:chunk]


def _forget_cumsum(f_pad, b_pad, batch, seq):
    chunk = _tiles()["cum_chunk"]
    r = lax.broadcasted_iota(jnp.int32, (chunk, chunk), 0)
    c = lax.broadcasted_iota(jnp.int32, (chunk, chunk), 1)
    tri = (r <= c).astype(BF16)
    return pl.pallas_call(
        functools.partial(_forget_kernel, chunk=chunk),
        out_shape=jax.ShapeDtypeStruct((batch, N_HEADS, seq), F32),
        grid=(batch,),
        in_specs=[
            pl.BlockSpec((seq, LANES), lambda b: (b, 0)),
            pl.BlockSpec((1, LANES), lambda b: (0, 0)),
            pl.BlockSpec((chunk, chunk), lambda b: (0, 0)),
        ],
        out_specs=pl.BlockSpec((1, N_HEADS, seq), lambda b: (b, 0, 0)),
        compiler_params=_params(("parallel",)),
        name="forget_cumsum",
    )(f_pad, b_pad, tri)


def _head(ref, hh, rows=slice(None)):
    return ref[0, rows, hh * HEAD_DIM:(hh + 1) * HEAD_DIM]


def _causal_softmax_pv(s, v, tq):
    length = s.shape[1]
    row = lax.broadcasted_iota(jnp.int32, (tq, tq), 0)
    col = lax.broadcasted_iota(jnp.int32, (tq, tq), 1)
    own = jnp.where(col <= row, s[:, length - tq:], -jnp.inf)
    s = own if length == tq else jnp.concatenate([s[:, :length - tq], own], axis=1)
    m = jnp.max(s, axis=-1, keepdims=True)
    p = jnp.exp2(s - m).astype(BF16)
    v1 = jnp.concatenate([v, jnp.ones(v.shape, v.dtype)], axis=1)
    acc = jnp.dot(p, v1, preferred_element_type=F32)
    return (acc[:, :HEAD_DIM] / acc[:, HEAD_DIM:]).astype(BF16)


def _tile_rows(ii, tq):
    return slice(ii * tq, (ii + 1) * tq)


def _attend(scores, v_ref, o_ref, *, n_tiles, per_step, tq, heads):
    def run(tiles):
        jobs = [(ii, hh) for ii in tiles for hh in range(heads)]
        for j0 in range(0, len(jobs), SCORE_GROUP):
            group = jobs[j0:j0 + SCORE_GROUP]
            ss = [scores(ii, hh) for ii, hh in group]
            for (ii, hh), s in zip(group, ss):
                o_ref[0, _tile_rows(ii, tq), hh * HEAD_DIM:(hh + 1) * HEAD_DIM] = (
                    _causal_softmax_pv(s, _head(v_ref, hh, slice(0, (ii + 1) * tq)), tq))

    n_steps = n_tiles // per_step
    assert per_step % 2 == 0 and n_steps * per_step == n_tiles
    step = pl.program_id(2)
    for pp in range(n_steps):
        lows = [pp + j * n_steps for j in range(per_step // 2)]
        tiles = [t for lo in lows for t in (lo, n_tiles - 1 - lo)]
        pl.when(step == pp)(functools.partial(run, tiles))


def _fox_kernel(q_ref, k_ref, v_ref, c_ref, o_ref, *, tq, heads, n_tiles, per_step):
    def scores(ii, hh):
        length = (ii + 1) * tq
        q = (_head(q_ref, hh, _tile_rows(ii, tq)).astype(F32) * (QK_SCALE * LOG2E)).astype(BF16)
        c = c_ref[0, hh]
        bias = (c[:, ii * tq:ii * tq + 1] - c[:, :length]) * LOG2E
        return _dot_nt(q, _head(k_ref, hh, slice(0, length))) + bias

    _attend(scores, v_ref, o_ref, n_tiles=n_tiles, per_step=per_step, tq=tq, heads=heads)


def _fox(p3, c4, batch, seq, casts):
    t = _tiles()
    tq, heads, per_step = t["attn_tq"], t["attn_heads"], t["attn_tiles_per_step"]
    hw = heads * HEAD_DIM
    groups = N_HEADS // heads
    n_tiles = seq // tq
    (o,), cast = _call_with_casts(
        functools.partial(_fox_kernel, tq=tq, heads=heads, n_tiles=n_tiles, per_step=per_step),
        casts, (p3, p3, p3, c4),
        out_shape=(jax.ShapeDtypeStruct((batch, seq, WIDTH), BF16),),
        grid=(batch, groups, n_tiles // per_step),
        in_specs=[
            pl.BlockSpec((1, seq, hw), lambda b, g, i: (b, 0, g)),
            pl.BlockSpec((1, seq, hw), lambda b, g, i: (b, 0, groups + g)),
            pl.BlockSpec((1, seq, hw), lambda b, g, i: (b, 0, 2 * groups + g)),
            pl.BlockSpec((1, heads, 1, seq), lambda b, g, i: (b, g, 0, 0)),
        ],
        out_specs=(pl.BlockSpec((1, seq, hw), lambda b, g, i: (b, 0, g)),),
        compiler_params=_params(("arbitrary", "arbitrary", "arbitrary")),
        name="fox_attention",
    )
    return o, cast


def _moba_block_bias(q_raw, km, ii):
    km_hi, km_lo = _split_bf16(km)
    gate = _dot_nt(km_hi, q_raw) + _dot_nt(km_lo, q_raw)
    block = lax.broadcasted_iota(jnp.int32, gate.shape, 0)
    g = jnp.where(block < ii, gate, -jnp.inf)
    rank = jnp.zeros(gate.shape, jnp.int32)
    for n in range(ii):
        gn = g[n:n + 1, :]
        ahead = jnp.logical_or(gn > g, jnp.logical_and(gn == g, n < block))
        rank = rank + ahead.astype(jnp.int32)
    bias_t = jnp.where(rank < MOBA_TOPK, 0.0, MASK_BIAS)
    pad = jnp.zeros((LANES - bias_t.shape[0], bias_t.shape[1]), F32)
    return jnp.concatenate([bias_t, pad], axis=0).T


def _moba_kernel(q_ref, k_ref, v_ref, o_ref, km_scr, *, n_blocks, heads, per_step):
    blk = MOBA_BLOCK

    @pl.when(pl.program_id(2) == 0)
    def _():
        for hh in range(heads):
            for n in range(n_blocks):
                kb = _head(k_ref, hh, _tile_rows(n, blk)).astype(F32)
                km_scr[hh, n:n + 1, :] = jnp.mean(kb, axis=0, keepdims=True)

    def scores(ii, hh):
        length = (ii + 1) * blk
        q_raw = _head(q_ref, hh, _tile_rows(ii, blk))
        q = (q_raw.astype(F32) * (QK_SCALE * LOG2E)).astype(BF16)
        s = _dot_nt(q, _head(k_ref, hh, slice(0, length)))
        if ii <= MOBA_TOPK:
            return s
        bias = _moba_block_bias(q_raw, km_scr[hh], ii)
        past = [s[:, _tile_rows(j, blk)] + bias[:, j:j + 1] for j in range(ii)]
        return jnp.concatenate(past + [s[:, _tile_rows(ii, blk)]], axis=1)

    _attend(scores, v_ref, o_ref, n_tiles=n_blocks, per_step=per_step, tq=blk, heads=heads)


def _moba(qk3, p3, batch, seq, casts):
    n_blocks = seq // MOBA_BLOCK
    assert seq % MOBA_BLOCK == 0 and n_blocks <= LANES
    t = _tiles()
    heads, per_step = t["attn_heads"], t["attn_tiles_per_step"]
    hw = heads * HEAD_DIM
    groups = N_HEADS // heads
    (o,), cast = _call_with_casts(
        functools.partial(_moba_kernel, n_blocks=n_blocks, heads=heads, per_step=per_step), casts,
        (qk3, qk3, p3),
        out_shape=(jax.ShapeDtypeStruct((batch, seq, WIDTH), BF16),),
        grid=(batch, groups, n_blocks // per_step),
        in_specs=[
            pl.BlockSpec((1, seq, hw), lambda b, g, i: (b, 0, g)),
            pl.BlockSpec((1, seq, hw), lambda b, g, i: (b, 0, groups + g)),
            pl.BlockSpec((1, seq, hw), lambda b, g, i: (b, 0, 3 * groups + g)),
        ],
        out_specs=(pl.BlockSpec((1, seq, hw), lambda b, g, i: (b, 0, g)),),
        scratch_shapes=[pltpu.VMEM((heads, n_blocks, HEAD_DIM), F32)],
        compiler_params=_params(("arbitrary", "arbitrary", "arbitrary")),
        name="moba_attention",
    )
    return o, cast


def _mix_resid_kernel(oa_ref, ob_ref, ga_ref, gb_ref, x_ref, wa_ref, wb_ref, wo_ref, g_ref,
                      x1_ref, h_ref):
    ya = jnp.dot(oa_ref[...], wa_ref[...], preferred_element_type=F32)
    yb = jnp.dot(ob_ref[...], wb_ref[...], preferred_element_type=F32)
    ga = jax.nn.sigmoid(ga_ref[...].astype(F32))
    gb = jax.nn.sigmoid(gb_ref[...].astype(F32))
    y = (ga * ya + gb * yb).astype(BF16)
    x1 = x_ref[...] + jnp.dot(y, wo_ref[...], preferred_element_type=F32)
    x1_ref[...] = x1
    h_ref[...] = _rms(x1, g_ref[...]).astype(BF16)


def _mix_resid(oa, ob, p2, x2, wa, wb, wo, g):
    tm = _tiles()["mix_tm"]
    rows, kdim = oa.shape
    d = wa.shape[1]
    ga_blk = (4 * WIDTH) // d
    resident = pl.Buffered(1)
    return pl.pallas_call(
        _mix_resid_kernel,
        out_shape=(jax.ShapeDtypeStruct((rows, d), F32),
                   jax.ShapeDtypeStruct((rows, d), BF16)),
        grid=(rows // tm,),
        in_specs=[
            pl.BlockSpec((tm, kdim), lambda i: (i, 0)),
            pl.BlockSpec((tm, kdim), lambda i: (i, 0)),
            pl.BlockSpec((tm, d), lambda i: (i, ga_blk)),
            pl.BlockSpec((tm, d), lambda i: (i, ga_blk + 1)),
            pl.BlockSpec((tm, d), lambda i: (i, 0)),
            pl.BlockSpec((kdim, d), lambda i: (0, 0), pipeline_mode=resident),
            pl.BlockSpec((kdim, d), lambda i: (0, 0), pipeline_mode=resident),
            pl.BlockSpec((d, d), lambda i: (0, 0), pipeline_mode=resident),
            pl.BlockSpec((1, d), lambda i: (0, 0)),
        ],
        out_specs=(pl.BlockSpec((tm, d), lambda i: (i, 0)),
                   pl.BlockSpec((tm, d), lambda i: (i, 0))),
        compiler_params=_params(("parallel",)),
        name="mix_residual",
    )(oa, ob, p2, p2, x2, wa, wb, wo, g)


def _ffn_kernel(h_ref, x1_ref, wg_ref, wu_ref, wd_ref, gf_ref, o_ref):
    k = pl.program_id(1)

    @pl.when(k == 0)
    def _():
        o_ref[...] = x1_ref[...]

    h = h_ref[...]
    gate = jnp.dot(h, wg_ref[...], preferred_element_type=F32)
    up = jnp.dot(h, wu_ref[...], preferred_element_type=F32)
    a = (gate * jax.nn.sigmoid(gate) * up).astype(BF16)
    o_ref[...] += jnp.dot(a, wd_ref[...], preferred_element_type=F32)

    @pl.when(k == pl.num_programs(1) - 1)
    def _():
        o_ref[...] = _rms(o_ref[...], gf_ref[...])


def _ffn(h2, x1, wg, wu, wd, gf):
    t = _tiles()
    tm, tf = t["ffn_tm"], t["ffn_tf"]
    rows, d = h2.shape
    dff = wd.shape[0]
    return pl.pallas_call(
        _ffn_kernel,
        out_shape=jax.ShapeDtypeStruct((rows, d), F32),
        grid=(rows // tm, dff // tf),
        in_specs=[
            pl.BlockSpec((tm, d), lambda i, k: (i, 0)),
            pl.BlockSpec((tm, d), lambda i, k: (i, 0)),
            pl.BlockSpec((d, tf), lambda i, k: (0, k)),
            pl.BlockSpec((d, tf), lambda i, k: (0, k)),
            pl.BlockSpec((tf, d), lambda i, k: (k, 0)),
            pl.BlockSpec((1, d), lambda i, k: (0, 0)),
        ],
        out_specs=pl.BlockSpec((tm, d), lambda i, k: (i, 0)),
        compiler_params=_params(("parallel", "arbitrary")),
        name="swiglu_ffn",
    )(h2, x1, wg, wu, wd, gf)


def _rope_tables(seq):
    half = HEAD_DIM // 2
    inv_freq = ROPE_THETA ** (-jnp.arange(half, dtype=F32) / half)
    ang = jnp.arange(seq, dtype=F32)[:, None] * inv_freq[None, :]
    cos, sin = jnp.cos(ang), jnp.sin(ang)
    return jnp.concatenate([cos, cos], axis=-1), jnp.concatenate([-sin, sin], axis=-1)


def kernel(x, norm_mix, w_in, b_forget, w_o_fox, w_o_moba, w_out, norm_ffn, w_gate_up, w_down,
           norm_final):
    batch, seq, d = x.shape
    assert w_in.shape[0] == 1
    cs, sn = _rope_tables(seq)
    x2 = x.reshape(batch * seq, d)
    wt = w_in[0].T
    b_pad = jnp.pad(b_forget[0][None, :], ((0, 0), (0, LANES - N_HEADS)))

    dff = w_down.shape[1]
    h, f_pad = _norm(x2, norm_mix[0][None, :], wt)
    p2, (wo_a, wo_b, w_res) = _proj_plain(
        h, wt, [_Cast(w_o_fox[0], d), _Cast(w_o_moba[0], d), _Cast(w_out[0], d)])
    qk, _ = _proj_rot(h, wt, cs, sn, seq, [])
    c = _forget_cumsum(f_pad, b_pad, batch, seq)
    p3 = p2.reshape(batch, seq, p2.shape[1])
    o_a, (wg, wd) = _fox(p3, c.reshape(batch, N_HEADS, 1, seq), batch, seq,
                         [_Cast(w_gate_up[0], dff, 0), _Cast(w_down[0], d)])
    o_b, (wu,) = _moba(qk.reshape(batch, seq, qk.shape[1]), p3, batch, seq,
                       [_Cast(w_gate_up[0], dff, 1)])
    x1, h2 = _mix_resid(o_a.reshape(batch * seq, WIDTH), o_b.reshape(batch * seq, WIDTH), p2, x2,
                        wo_a, wo_b, w_res, norm_ffn[0][None, :])
    out = _ffn(h2, x1, wg, wu, wd, norm_final[None, :])
    return out.reshape(batch, seq, d)
```

```python
import functools
import math
from typing import NamedTuple

import jax
import jax.numpy as jnp
from jax import lax
from jax.experimental import pallas as pl
from jax.experimental.pallas import tpu as pltpu

F32 = jnp.float32
BF16 = jnp.bfloat16

HEAD_DIM = 128
N_HEADS = 8
WIDTH = N_HEADS * HEAD_DIM
MOBA_BLOCK = 256
MOBA_TOPK = 3
ROPE_THETA = 10000.0
RMS_EPS = 1e-6
LOG2E = math.log2(math.e)
QK_SCALE = HEAD_DIM ** -0.5
LANES = 128
BF16_SUBLANES = 16
MASK_BIAS = -1e30
SCORE_GROUP = 8
VMEM_LIMIT = 60 * 1024 * 1024


def _tiles():
    return dict(
        norm_tm=1024, proj_tm=2048, rot_tm=2048, proj_tn=1024,
        attn_tq=256, attn_heads=4, attn_tiles_per_step=4,
        mix_tm=512,
        ffn_tm=1024, ffn_tf=512,
        cum_chunk=256,
    )


def _params(sem):
    return pltpu.CompilerParams(dimension_semantics=sem, vmem_limit_bytes=VMEM_LIMIT)


def _rms(x, g):
    ms = jnp.mean(x * x, axis=-1, keepdims=True)
    return x * lax.rsqrt(ms + RMS_EPS) * g


def _dot_nt(a, b):
    return lax.dot_general(a, b, (((1,), (1,)), ((), ())), preferred_element_type=F32)


class _Cast(NamedTuple):
    src: jax.Array
    cols: int
    col_block: int = 0


def _call_with_casts(kernel_fn, casts, args, *, grid, in_specs, out_specs, out_shape, **kw):
    steps = math.prod(grid)
    strides = [math.prod(grid[a + 1:]) for a in range(len(grid))]

    def step_of(*g):
        return sum(gi * st for gi, st in zip(g, strides))

    n_in, n_out, n_side = len(in_specs), len(out_shape), len(casts)
    side_in, side_out, side_shape = [], [], []
    for c in casts:
        rows = c.src.shape[0] // steps
        assert rows * steps == c.src.shape[0] and rows % BF16_SUBLANES == 0
        side_in.append(pl.BlockSpec((rows, c.cols), lambda *g, c=c: (step_of(*g), c.col_block)))
        side_out.append(pl.BlockSpec((rows, c.cols), lambda *g: (step_of(*g), 0)))
        side_shape.append(jax.ShapeDtypeStruct((c.src.shape[0], c.cols), BF16))

    def body(*refs):
        src = refs[n_in:n_in + n_side]
        dst = refs[n_in + n_side + n_out:n_in + 2 * n_side + n_out]
        for s_ref, d_ref in zip(src, dst):
            d_ref[...] = s_ref[...].astype(BF16)
        kernel_fn(*refs[:n_in], *refs[n_in + n_side:n_in + n_side + n_out],
                  *refs[n_in + 2 * n_side + n_out:])

    res = pl.pallas_call(
        body, grid=grid, in_specs=[*in_specs, *side_in], out_specs=(*out_specs, *side_out),
        out_shape=(*out_shape, *side_shape), **kw)(*args, *[c.src for c in casts])
    return res[:n_out], res[n_out:]


def _split_bf16(v):
    hi = v.astype(BF16)
    return hi, (v - hi.astype(F32)).astype(BF16)


def _norm_kernel(x_ref, g_ref, wf_ref, h_ref, f_ref):
    h = _rms(x_ref[...], g_ref[...]).astype(BF16)
    h_ref[...] = h
    f_ref[...] = _dot_nt(h, wf_ref[...].astype(BF16))


def _norm(x2, g, wt):
    tm = _tiles()["norm_tm"]
    rows, d = x2.shape
    f_blk = (6 * WIDTH) // LANES
    return pl.pallas_call(
        _norm_kernel,
        out_shape=(jax.ShapeDtypeStruct((rows, d), BF16),
                   jax.ShapeDtypeStruct((rows, LANES), F32)),
        grid=(rows // tm,),
        in_specs=[
            pl.BlockSpec((tm, d), lambda i: (i, 0)),
            pl.BlockSpec((1, d), lambda i: (0, 0)),
            pl.BlockSpec((LANES, d), lambda i: (f_blk, 0)),
        ],
        out_specs=(pl.BlockSpec((tm, d), lambda i: (i, 0)),
                   pl.BlockSpec((tm, LANES), lambda i: (i, 0))),
        compiler_params=_params(("parallel",)),
        name="pre_norm",
    )(x2, g, wt)


def _cast_weight_tile(w_ref, w2_ref, w_scr, shift):
    if shift == 0:
        w_scr[...] = w_ref[...].astype(BF16)
    else:
        w_scr[...] = jnp.concatenate([w_ref[shift:, :], w2_ref[...]], axis=0).astype(BF16)


def _proj_kernel(h_ref, w_ref, w2_ref, o_ref, w_scr, *, n_aligned):
    j = pl.program_id(0)

    @pl.when(jnp.logical_and(pl.program_id(1) == 0, j < n_aligned))
    def _():
        _cast_weight_tile(w_ref, w2_ref, w_scr, 0)

    @pl.when(jnp.logical_and(pl.program_id(1) == 0, j >= n_aligned))
    def _():
        _cast_weight_tile(w_ref, w2_ref, w_scr, N_HEADS)

    o_ref[...] = _dot_nt(h_ref[...], w_scr[...]).astype(BF16)


def _proj_rot_kernel(h_ref, w_ref, cs_ref, sn_ref, o_ref, w_scr):
    @pl.when(pl.program_id(1) == 0)
    def _():
        _cast_weight_tile(w_ref, None, w_scr, 0)

    acc = _dot_nt(h_ref[...], w_scr[...])
    cs = cs_ref[...]
    sn = sn_ref[...]
    for hh in range(acc.shape[1] // HEAD_DIM):
        t = acc[:, hh * HEAD_DIM:(hh + 1) * HEAD_DIM]
        r = pltpu.roll(t, HEAD_DIM // 2, axis=1)
        o_ref[:, hh * HEAD_DIM:(hh + 1) * HEAD_DIM] = (t * cs + r * sn).astype(BF16)


def _proj_plain(h, wt, casts):
    t = _tiles()
    tm, tn = t["proj_tm"], t["proj_tn"]
    rows, d = h.shape
    per_group = WIDTH // tn
    n_aligned = 4 * per_group
    n_gate = (2 * d) // tn
    gate0 = (6 * WIDTH) // tn

    def w_blk(j, i):
        aligned = jnp.where(j < 3 * per_group, j, j + 2 * per_group)
        return jnp.where(j < n_aligned, aligned, gate0 + (j - n_aligned)), 0

    def w2_blk(j, i):
        nxt = jnp.maximum(j - n_aligned, 0) + 1
        return (gate0 + nxt) * (tn // N_HEADS), 0

    (p2,), cast = _call_with_casts(
        functools.partial(_proj_kernel, n_aligned=n_aligned), casts, (h, wt, wt),
        out_shape=(jax.ShapeDtypeStruct((rows, (n_aligned + n_gate) * tn), BF16),),
        grid=(n_aligned + n_gate, rows // tm),
        in_specs=[
            pl.BlockSpec((tm, d), lambda j, i: (i, 0)),
            pl.BlockSpec((tn, d), w_blk),
            pl.BlockSpec((N_HEADS, d), w2_blk),
        ],
        out_specs=(pl.BlockSpec((tm, tn), lambda j, i: (i, j)),),
        scratch_shapes=[pltpu.VMEM((tn, d), BF16)],
        compiler_params=_params(("arbitrary", "arbitrary")),
        name="proj_plain",
    )
    return p2, cast


def _proj_rot(h, wt, cs, sn, seq, casts):
    t = _tiles()
    tm, tn = t["rot_tm"], t["proj_tn"]
    rows, d = h.shape
    first = (3 * WIDTH) // tn
    pos_blocks = seq // tm
    (qk,), cast = _call_with_casts(
        _proj_rot_kernel, casts, (h, wt, cs, sn),
        out_shape=(jax.ShapeDtypeStruct((rows, 2 * WIDTH), BF16),),
        grid=((2 * WIDTH) // tn, rows // tm),
        in_specs=[
            pl.BlockSpec((tm, d), lambda j, i: (i, 0)),
            pl.BlockSpec((tn, d), lambda j, i: (first + j, 0)),
            pl.BlockSpec((tm, HEAD_DIM), lambda j, i: (i % pos_blocks, 0)),
            pl.BlockSpec((tm, HEAD_DIM), lambda j, i: (i % pos_blocks, 0)),
        ],
        out_specs=(pl.BlockSpec((tm, tn), lambda j, i: (i, j)),),
        scratch_shapes=[pltpu.VMEM((tn, d), BF16)],
        compiler_params=_params(("arbitrary", "arbitrary")),
        name="proj_rotary",
    )
    return qk, cast


def _forget_kernel(f_ref, b_ref, tri_ref, c_ref, *, chunk):
    z = f_ref[...] + b_ref[...]
    lf = jnp.minimum(z, 0.0) - jnp.log1p(jnp.exp(-jnp.abs(z)))
    lft = lf.T[:N_HEADS, :]
    tri = tri_ref[...]
    carry = jnp.zeros((N_HEADS, 1), F32)
    for n in range(lft.shape[1] // chunk):
        xc = lft[:, n * chunk:(n + 1) * chunk]
        hi = xc.astype(BF16)
        r1 = xc - hi.astype(F32)
        mid = r1.astype(BF16)
        lo = (r1 - mid.astype(F32)).astype(BF16)
        loc = (jnp.dot(hi, tri, preferred_element_type=F32)
               + jnp.dot(mid, tri, preferred_element_type=F32)
               + jnp.dot(lo, tri, preferred_element_type=F32)) + carry
        c_ref[0, :, n * chunk:(n + 1) * chunk] = loc
        carry = loc[:, chunk - 1:chunk]


def _forget_cumsum(f_pad, b_pad, batch, seq):
    chunk = _tiles()["cum_chunk"]
    r = lax.broadcasted_iota(jnp.int32, (chunk, chunk), 0)
    c = lax.broadcasted_iota(jnp.int32, (chunk, chunk), 1)
    tri = (r <= c).astype(BF16)
    return pl.pallas_call(
        functools.partial(_forget_kernel, chunk=chunk),
        out_shape=jax.ShapeDtypeStruct((batch, N_HEADS, seq), F32),
        grid=(batch,),
        in_specs=[
            pl.BlockSpec((seq, LANES), lambda b: (b, 0)),
            pl.BlockSpec((1, LANES), lambda b: (0, 0)),
            pl.BlockSpec((chunk, chunk), lambda b: (0, 0)),
        ],
        out_specs=pl.BlockSpec((1, N_HEADS, seq), lambda b: (b, 0, 0)),
        compiler_params=_params(("parallel",)),
        name="forget_cumsum",
    )(f_pad, b_pad, tri)


def _head(ref, hh, rows=slice(None)):
    return ref[0, rows, hh * HEAD_DIM:(hh + 1) * HEAD_DIM]


def _causal_softmax_pv(s, v, tq):
    length = s.shape[1]
    row = lax.broadcasted_iota(jnp.int32, (tq, tq), 0)
    col = lax.broadcasted_iota(jnp.int32, (tq, tq), 1)
    own = jnp.where(col <= row, s[:, length - tq:], -jnp.inf)
    s = own if length == tq else jnp.concatenate([s[:, :length - tq], own], axis=1)
    m = jnp.max(s, axis=-1, keepdims=True)
    p = jnp.exp2(s - m).astype(BF16)
    v1 = jnp.concatenate([v, jnp.ones(v.shape, v.dtype)], axis=1)
    acc = jnp.dot(p, v1, preferred_element_type=F32)
    return (acc[:, :HEAD_DIM] / acc[:, HEAD_DIM:]).astype(BF16)


def _tile_rows(ii, tq):
    return slice(ii * tq, (ii + 1) * tq)


def _attend(scores, v_ref, o_ref, *, n_tiles, per_step, tq, heads):
    def run(tiles):
        jobs = [(ii, hh) for ii in tiles for hh in range(heads)]
        for j0 in range(0, len(jobs), SCORE_GROUP):
            group = jobs[j0:j0 + SCORE_GROUP]
            ss = [scores(ii, hh) for ii, hh in group]
            for (ii, hh), s in zip(group, ss):
                o_ref[0, _tile_rows(ii, tq), hh * HEAD_DIM:(hh + 1) * HEAD_DIM] = (
                    _causal_softmax_pv(s, _head(v_ref, hh, slice(0, (ii + 1) * tq)), tq))

    n_steps = n_tiles // per_step
    assert per_step % 2 == 0 and n_steps * per_step == n_tiles
    step = pl.program_id(2)
    for pp in range(n_steps):
        lows = [pp + j * n_steps for j in range(per_step // 2)]
        tiles = [t for lo in lows for t in (lo, n_tiles - 1 - lo)]
        pl.when(step == pp)(functools.partial(run, tiles))


def _fox_kernel(q_ref, k_ref, v_ref, c_ref, o_ref, *, tq, heads, n_tiles, per_step):
    def scores(ii, hh):
        length = (ii + 1) * tq
        q = (_head(q_ref, hh, _tile_rows(ii, tq)).astype(F32) * (QK_SCALE * LOG2E)).astype(BF16)
        c = c_ref[0, hh]
        bias = (c[:, ii * tq:ii * tq + 1] - c[:, :length]) * LOG2E
        return _dot_nt(q, _head(k_ref, hh, slice(0, length))) + bias

    _attend(scores, v_ref, o_ref, n_tiles=n_tiles, per_step=per_step, tq=tq, heads=heads)


def _fox(p3, c4, batch, seq, casts):
    t = _tiles()
    tq, heads, per_step = t["attn_tq"], t["attn_heads"], t["attn_tiles_per_step"]
    hw = heads * HEAD_DIM
    groups = N_HEADS // heads
    n_tiles = seq // tq
    (o,), cast = _call_with_casts(
        functools.partial(_fox_kernel, tq=tq, heads=heads, n_tiles=n_tiles, per_step=per_step),
        casts, (p3, p3, p3, c4),
        out_shape=(jax.ShapeDtypeStruct((batch, seq, WIDTH), BF16),),
        grid=(batch, groups, n_tiles // per_step),
        in_specs=[
            pl.BlockSpec((1, seq, hw), lambda b, g, i: (b, 0, g)),
            pl.BlockSpec((1, seq, hw), lambda b, g, i: (b, 0, groups + g)),
            pl.BlockSpec((1, seq, hw), lambda b, g, i: (b, 0, 2 * groups + g)),
            pl.BlockSpec((1, heads, 1, seq), lambda b, g, i: (b, g, 0, 0)),
        ],
        out_specs=(pl.BlockSpec((1, seq, hw), lambda b, g, i: (b, 0, g)),),
        compiler_params=_params(("arbitrary", "arbitrary", "arbitrary")),
        name="fox_attention",
    )
    return o, cast


def _moba_block_bias(q_raw, km, ii):
    km_hi, km_lo = _split_bf16(km)
    gate = _dot_nt(km_hi, q_raw) + _dot_nt(km_lo, q_raw)
    block = lax.broadcasted_iota(jnp.int32, gate.shape, 0)
    g = jnp.where(block < ii, gate, -jnp.inf)
    rank = jnp.zeros(gate.shape, jnp.int32)
    for n in range(ii):
        gn = g[n:n + 1, :]
        ahead = jnp.logical_or(gn > g, jnp.logical_and(gn == g, n < block))
        rank = rank + ahead.astype(jnp.int32)
    bias_t = jnp.where(rank < MOBA_TOPK, 0.0, MASK_BIAS)
    pad = jnp.zeros((LANES - bias_t.shape[0], bias_t.shape[1]), F32)
    return jnp.concatenate([bias_t, pad], axis=0).T


def _moba_kernel(q_ref, k_ref, v_ref, o_ref, km_scr, *, n_blocks, heads, per_step):
    blk = MOBA_BLOCK

    @pl.when(pl.program_id(2) == 0)
    def _():
        for hh in range(heads):
            for n in range(n_blocks):
                kb = _head(k_ref, hh, _tile_rows(n, blk)).astype(F32)
                km_scr[hh, n:n + 1, :] = jnp.mean(kb, axis=0, keepdims=True)

    def scores(ii, hh):
        length = (ii + 1) * blk
        q_raw = _head(q_ref, hh, _tile_rows(ii, blk))
        q = (q_raw.astype(F32) * (QK_SCALE * LOG2E)).astype(BF16)
        s = _dot_nt(q, _head(k_ref, hh, slice(0, length)))
        if ii <= MOBA_TOPK:
            return s
        bias = _moba_block_bias(q_raw, km_scr[hh], ii)
        past = [s[:, _tile_rows(j, blk)] + bias[:, j:j + 1] for j in range(ii)]
        return jnp.concatenate(past + [s[:, _tile_rows(ii, blk)]], axis=1)

    _attend(scores, v_ref, o_ref, n_tiles=n_blocks, per_step=per_step, tq=blk, heads=heads)


def _moba(qk3, p3, batch, seq, casts):
    n_blocks = seq // MOBA_BLOCK
    assert seq % MOBA_BLOCK == 0 and n_blocks <= LANES
    t = _tiles()
    heads, per_step = t["attn_heads"], t["attn_tiles_per_step"]
    hw = heads * HEAD_DIM
    groups = N_HEADS // heads
    (o,), cast = _call_with_casts(
        functools.partial(_moba_kernel, n_blocks=n_blocks, heads=heads, per_step=per_step), casts,
        (qk3, qk3, p3),
        out_shape=(jax.ShapeDtypeStruct((batch, seq, WIDTH), BF16),),
        grid=(batch, groups, n_blocks // per_step),
        in_specs=[
            pl.BlockSpec((1, seq, hw), lambda b, g, i: (b, 0, g)),
            pl.BlockSpec((1, seq, hw), lambda b, g, i: (b, 0, groups + g)),
            pl.BlockSpec((1, seq, hw), lambda b, g, i: (b, 0, 3 * groups + g)),
        ],
        out_specs=(pl.BlockSpec((1, seq, hw), lambda b, g, i: (b, 0, g)),),
        scratch_shapes=[pltpu.VMEM((heads, n_blocks, HEAD_DIM), F32)],
        compiler_params=_params(("arbitrary", "arbitrary", "arbitrary")),
        name="moba_attention",
    )
    return o, cast


def _mix_resid_kernel(oa_ref, ob_ref, ga_ref, gb_ref, x_ref, wa_ref, wb_ref, wo_ref, g_ref,
                      x1_ref, h_ref):
    ya = jnp.dot(oa_ref[...], wa_ref[...], preferred_element_type=F32)
    yb = jnp.dot(ob_ref[...], wb_ref[...], preferred_element_type=F32)
    ga = jax.nn.sigmoid(ga_ref[...].astype(F32))
    gb = jax.nn.sigmoid(gb_ref[...].astype(F32))
    y = (ga * ya + gb * yb).astype(BF16)
    x1 = x_ref[...] + jnp.dot(y, wo_ref[...], preferred_element_type=F32)
    x1_ref[...] = x1
    h_ref[...] = _rms(x1, g_ref[...]).astype(BF16)


def _mix_resid(oa, ob, p2, x2, wa, wb, wo, g):
    tm = _tiles()["mix_tm"]
    rows, kdim = oa.shape
    d = wa.shape[1]
    ga_blk = (4 * WIDTH) // d
    resident = pl.Buffered(1)
    return pl.pallas_call(
        _mix_resid_kernel,
        out_shape=(jax.ShapeDtypeStruct((rows, d), F32),
                   jax.ShapeDtypeStruct((rows, d), BF16)),
        grid=(rows // tm,),
        in_specs=[
            pl.BlockSpec((tm, kdim), lambda i: (i, 0)),
            pl.BlockSpec((tm, kdim), lambda i: (i, 0)),
            pl.BlockSpec((tm, d), lambda i: (i, ga_blk)),
            pl.BlockSpec((tm, d), lambda i: (i, ga_blk + 1)),
            pl.BlockSpec((tm, d), lambda i: (i, 0)),
            pl.BlockSpec((kdim, d), lambda i: (0, 0), pipeline_mode=resident),
            pl.BlockSpec((kdim, d), lambda i: (0, 0), pipeline_mode=resident),
            pl.BlockSpec((d, d), lambda i: (0, 0), pipeline_mode=resident),
            pl.BlockSpec((1, d), lambda i: (0, 0)),
        ],
        out_specs=(pl.BlockSpec((tm, d), lambda i: (i, 0)),
                   pl.BlockSpec((tm, d), lambda i: (i, 0))),
        compiler_params=_params(("parallel",)),
        name="mix_residual",
    )(oa, ob, p2, p2, x2, wa, wb, wo, g)


def _ffn_kernel(h_ref, x1_ref, wg_ref, wu_ref, wd_ref, gf_ref, o_ref):
    k = pl.program_id(1)

    @pl.when(k == 0)
    def _():
        o_ref[...] = x1_ref[...]

    h = h_ref[...]
    gate = jnp.dot(h, wg_ref[...], preferred_element_type=F32)
    up = jnp.dot(h, wu_ref[...], preferred_element_type=F32)
    a = (gate * jax.nn.sigmoid(gate) * up).astype(BF16)
    o_ref[...] += jnp.dot(a, wd_ref[...], preferred_element_type=F32)

    @pl.when(k == pl.num_programs(1) - 1)
    def _():
        o_ref[...] = _rms(o_ref[...], gf_ref[...])


def _ffn(h2, x1, wg, wu, wd, gf):
    t = _tiles()
    tm, tf = t["ffn_tm"], t["ffn_tf"]
    rows, d = h2.shape
    dff = wd.shape[0]
    return pl.pallas_call(
        _ffn_kernel,
        out_shape=jax.ShapeDtypeStruct((rows, d), F32),
        grid=(rows // tm, dff // tf),
        in_specs=[
            pl.BlockSpec((tm, d), lambda i, k: (i, 0)),
            pl.BlockSpec((tm, d), lambda i, k: (i, 0)),
            pl.BlockSpec((d, tf), lambda i, k: (0, k)),
            pl.BlockSpec((d, tf), lambda i, k: (0, k)),
            pl.BlockSpec((tf, d), lambda i, k: (k, 0)),
            pl.BlockSpec((1, d), lambda i, k: (0, 0)),
        ],
        out_specs=pl.BlockSpec((tm, d), lambda i, k: (i, 0)),
        compiler_params=_params(("parallel", "arbitrary")),
        name="swiglu_ffn",
    )(h2, x1, wg, wu, wd, gf)


def _rope_tables(seq):
    half = HEAD_DIM // 2
    inv_freq = ROPE_THETA ** (-jnp.arange(half, dtype=F32) / half)
    ang = jnp.arange(seq, dtype=F32)[:, None] * inv_freq[None, :]
    cos, sin = jnp.cos(ang), jnp.sin(ang)
    return jnp.concatenate([cos, cos], axis=-1), jnp.concatenate([-sin, sin], axis=-1)


def kernel(x, norm_mix, w_in, b_forget, w_o_fox, w_o_moba, w_out, norm_ffn, w_gate_up, w_down,
           norm_final):
    batch, seq, d = x.shape
    assert w_in.shape[0] == 1
    cs, sn = _rope_tables(seq)
    x2 = x.reshape(batch * seq, d)
    wt = w_in[0].T
    b_pad = jnp.pad(b_forget[0][None, :], ((0, 0), (0, LANES - N_HEADS)))

    dff = w_down.shape[1]
    h, f_pad = _norm(x2, norm_mix[0][None, :], wt)
    p2, (wo_a, wo_b, w_res) = _proj_plain(
        h, wt, [_Cast(w_o_fox[0], d), _Cast(w_o_moba[0], d), _Cast(w_out[0], d)])
    qk, _ = _proj_rot(h, wt, cs, sn, seq, [])
    c = _forget_cumsum(f_pad, b_pad, batch, seq)
    p3 = p2.reshape(batch, seq, p2.shape[1])
    o_a, (wg, wd) = _fox(p3, c.reshape(batch, N_HEADS, 1, seq), batch, seq,
                         [_Cast(w_gate_up[0], dff, 0), _Cast(w_down[0], d)])
    o_b, (wu,) = _moba(qk.reshape(batch, seq, qk.shape[1]), p3, batch, seq,
                       [_Cast(w_gate_up[0], dff, 1)])
    x1, h2 = _mix_resid(o_a.reshape(batch * seq, WIDTH), o_b.reshape(batch * seq, WIDTH), p2, x2,
                        wo_a, wo_b, w_res, norm_ffn[0][None, :])
    out = _ffn(h2, x1, wg, wu, wd, norm_final[None, :])
    return out.reshape(batch, seq, d)
```

```python
import functools
import math
from typing import NamedTuple

import jax
import jax.numpy as jnp
from jax import lax
from jax.experimental import pallas as pl
from jax.experimental.pallas import tpu as pltpu

F32 = jnp.float32
BF16 = jnp.bfloat16

HEAD_DIM = 128
N_HEADS = 8
WIDTH = N_HEADS * HEAD_DIM
MOBA_BLOCK = 256
MOBA_TOPK = 3
ROPE_THETA = 10000.0
RMS_EPS = 1e-6
LOG2E = math.log2(math.e)
QK_SCALE = HEAD_DIM ** -0.5
LANES = 128
BF16_SUBLANES = 16
MASK_BIAS = -1e30
SCORE_GROUP = 16
VMEM_LIMIT = 60 * 1024 * 1024


def _tiles():
    return dict(
        norm_tm=1024, proj_tm=2048, rot_tm=2048, proj_tn=1024,
        attn_tq=256, attn_heads=4, attn_tiles_per_step=4,
        mix_tm=512,
        ffn_tm=1024, ffn_tf=512,
        cum_chunk=256,
    )


def _params(sem):
    return pltpu.CompilerParams(dimension_semantics=sem, vmem_limit_bytes=VMEM_LIMIT)


def _rms(x, g):
    ms = jnp.mean(x * x, axis=-1, keepdims=True)
    return x * lax.rsqrt(ms + RMS_EPS) * g


def _dot_nt(a, b):
    return lax.dot_general(a, b, (((1,), (1,)), ((), ())), preferred_element_type=F32)


class _Cast(NamedTuple):
    src: jax.Array
    cols: int
    col_block: int = 0


def _call_with_casts(kernel_fn, casts, args, *, grid, in_specs, out_specs, out_shape, **kw):
    steps = math.prod(grid)
    strides = [math.prod(grid[a + 1:]) for a in range(len(grid))]

    def step_of(*g):
        return sum(gi * st for gi, st in zip(g, strides))

    n_in, n_out, n_side = len(in_specs), len(out_shape), len(casts)
    side_in, side_out, side_shape = [], [], []
    for c in casts:
        rows = c.src.shape[0] // steps
        assert rows * steps == c.src.shape[0] and rows % BF16_SUBLANES == 0
        side_in.append(pl.BlockSpec((rows, c.cols), lambda *g, c=c: (step_of(*g), c.col_block)))
        side_out.append(pl.BlockSpec((rows, c.cols), lambda *g: (step_of(*g), 0)))
        side_shape.append(jax.ShapeDtypeStruct((c.src.shape[0], c.cols), BF16))

    def body(*refs):
        src = refs[n_in:n_in + n_side]
        dst = refs[n_in + n_side + n_out:n_in + 2 * n_side + n_out]
        for s_ref, d_ref in zip(src, dst):
            d_ref[...] = s_ref[...].astype(BF16)
        kernel_fn(*refs[:n_in], *refs[n_in + n_side:n_in + n_side + n_out],
                  *refs[n_in + 2 * n_side + n_out:])

    res = pl.pallas_call(
        body, grid=grid, in_specs=[*in_specs, *side_in], out_specs=(*out_specs, *side_out),
        out_shape=(*out_shape, *side_shape), **kw)(*args, *[c.src for c in casts])
    return res[:n_out], res[n_out:]


def _split_bf16(v):
    hi = v.astype(BF16)
    return hi, (v - hi.astype(F32)).astype(BF16)


def _norm_kernel(x_ref, g_ref, wf_ref, h_ref, f_ref):
    h = _rms(x_ref[...], g_ref[...]).astype(BF16)
    h_ref[...] = h
    f_ref[...] = _dot_nt(h, wf_ref[...].astype(BF16))


def _norm(x2, g, wt):
    tm = _tiles()["norm_tm"]
    rows, d = x2.shape
    f_blk = (6 * WIDTH) // LANES
    return pl.pallas_call(
        _norm_kernel,
        out_shape=(jax.ShapeDtypeStruct((rows, d), BF16),
                   jax.ShapeDtypeStruct((rows, LANES), F32)),
        grid=(rows // tm,),
        in_specs=[
            pl.BlockSpec((tm, d), lambda i: (i, 0)),
            pl.BlockSpec((1, d), lambda i: (0, 0)),
            pl.BlockSpec((LANES, d), lambda i: (f_blk, 0)),
        ],
        out_specs=(pl.BlockSpec((tm, d), lambda i: (i, 0)),
                   pl.BlockSpec((tm, LANES), lambda i: (i, 0))),
        compiler_params=_params(("parallel",)),
        name="pre_norm",
    )(x2, g, wt)


def _cast_weight_tile(w_ref, w2_ref, w_scr, shift):
    if shift == 0:
        w_scr[...] = w_ref[...].astype(BF16)
    else:
        w_scr[...] = jnp.concatenate([w_ref[shift:, :], w2_ref[...]], axis=0).astype(BF16)


def _proj_kernel(h_ref, w_ref, w2_ref, o_ref, w_scr, *, n_aligned):
    j = pl.program_id(0)

    @pl.when(jnp.logical_and(pl.program_id(1) == 0, j < n_aligned))
    def _():
        _cast_weight_tile(w_ref, w2_ref, w_scr, 0)

    @pl.when(jnp.logical_and(pl.program_id(1) == 0, j >= n_aligned))
    def _():
        _cast_weight_tile(w_ref, w2_ref, w_scr, N_HEADS)

    o_ref[...] = _dot_nt(h_ref[...], w_scr[...]).astype(BF16)


def _proj_rot_kernel(h_ref, w_ref, cs_ref, sn_ref, o_ref, w_scr):
    @pl.when(pl.program_id(1) == 0)
    def _():
        _cast_weight_tile(w_ref, None, w_scr, 0)

    acc = _dot_nt(h_ref[...], w_scr[...])
    cs = cs_ref[...]
    sn = sn_ref[...]
    for hh in range(acc.shape[1] // HEAD_DIM):
        t = acc[:, hh * HEAD_DIM:(hh + 1) * HEAD_DIM]
        r = pltpu.roll(t, HEAD_DIM // 2, axis=1)
        o_ref[:, hh * HEAD_DIM:(hh + 1) * HEAD_DIM] = (t * cs + r * sn).astype(BF16)


def _proj_plain(h, wt, casts):
    t = _tiles()
    tm, tn = t["proj_tm"], t["proj_tn"]
    rows, d = h.shape
    per_group = WIDTH // tn
    n_aligned = 4 * per_group
    n_gate = (2 * d) // tn
    gate0 = (6 * WIDTH) // tn

    def w_blk(j, i):
        aligned = jnp.where(j < 3 * per_group, j, j + 2 * per_group)
        return jnp.where(j < n_aligned, aligned, gate0 + (j - n_aligned)), 0

    def w2_blk(j, i):
        nxt = jnp.maximum(j - n_aligned, 0) + 1
        return (gate0 + nxt) * (tn // N_HEADS), 0

    (p2,), cast = _call_with_casts(
        functools.partial(_proj_kernel, n_aligned=n_aligned), casts, (h, wt, wt),
        out_shape=(jax.ShapeDtypeStruct((rows, (n_aligned + n_gate) * tn), BF16),),
        grid=(n_aligned + n_gate, rows // tm),
        in_specs=[
            pl.BlockSpec((tm, d), lambda j, i: (i, 0)),
            pl.BlockSpec((tn, d), w_blk),
            pl.BlockSpec((N_HEADS, d), w2_blk),
        ],
        out_specs=(pl.BlockSpec((tm, tn), lambda j, i: (i, j)),),
        scratch_shapes=[pltpu.VMEM((tn, d), BF16)],
        compiler_params=_params(("arbitrary", "arbitrary")),
        name="proj_plain",
    )
    return p2, cast


def _proj_rot(h, wt, cs, sn, seq, casts):
    t = _tiles()
    tm, tn = t["rot_tm"], t["proj_tn"]
    rows, d = h.shape
    first = (3 * WIDTH) // tn
    pos_blocks = seq // tm
    (qk,), cast = _call_with_casts(
        _proj_rot_kernel, casts, (h, wt, cs, sn),
        out_shape=(jax.ShapeDtypeStruct((rows, 2 * WIDTH), BF16),),
        grid=((2 * WIDTH) // tn, rows // tm),
        in_specs=[
            pl.BlockSpec((tm, d), lambda j, i: (i, 0)),
            pl.BlockSpec((tn, d), lambda j, i: (first + j, 0)),
            pl.BlockSpec((tm, HEAD_DIM), lambda j, i: (i % pos_blocks, 0)),
            pl.BlockSpec((tm, HEAD_DIM), lambda j, i: (i % pos_blocks, 0)),
        ],
        out_specs=(pl.BlockSpec((tm, tn), lambda j, i: (i, j)),),
        scratch_shapes=[pltpu.VMEM((tn, d), BF16)],
        compiler_params=_params(("arbitrary", "arbitrary")),
        name="proj_rotary",
    )
    return qk, cast


def _forget_kernel(f_ref, b_ref, tri_ref, c_ref, *, chunk):
    z = f_ref[...] + b_ref[...]
    lf = jnp.minimum(z, 0.0) - jnp.log1p(jnp.exp(-jnp.abs(z)))
    lft = lf.T[:N_HEADS, :]
    tri = tri_ref[...]
    carry = jnp.zeros((N_HEADS, 1), F32)
    for n in range(lft.shape[1] // chunk):
        xc = lft[:, n * chunk:(n + 1) * chunk]
        hi = xc.astype(BF16)
        r1 = xc - hi.astype(F32)
        mid = r1.astype(BF16)
        lo = (r1 - mid.astype(F32)).astype(BF16)
        loc = (jnp.dot(hi, tri, preferred_element_type=F32)
               + jnp.dot(mid, tri, preferred_element_type=F32)
               + jnp.dot(lo, tri, preferred_element_type=F32)) + carry
        c_ref[0, :, n * chunk:(n + 1) * chunk] = loc
        carry = loc[:, chunk - 1:chunk]


def _forget_cumsum(f_pad, b_pad, batch, seq):
    chunk = _tiles()["cum_chunk"]
    r = lax.broadcasted_iota(jnp.int32, (chunk, chunk), 0)
    c = lax.broadcasted_iota(jnp.int32, (chunk, chunk), 1)
    tri = (r <= c).astype(BF16)
    return pl.pallas_call(
        functools.partial(_forget_kernel, chunk=chunk),
        out_shape=jax.ShapeDtypeStruct((batch, N_HEADS, seq), F32),
        grid=(batch,),
        in_specs=[
            pl.BlockSpec((seq, LANES), lambda b: (b, 0)),
            pl.BlockSpec((1, LANES), lambda b: (0, 0)),
            pl.BlockSpec((chunk, chunk), lambda b: (0, 0)),
        ],
        out_specs=pl.BlockSpec((1, N_HEADS, seq), lambda b: (b, 0, 0)),
        compiler_params=_params(("parallel",)),
        name="forget_cumsum",
    )(f_pad, b_pad, tri)


def _head(ref, hh, rows=slice(None)):
    return ref[0, rows, hh * HEAD_DIM:(hh + 1) * HEAD_DIM]


def _causal_softmax_pv(s, v, tq):
    length = s.shape[1]
    row = lax.broadcasted_iota(jnp.int32, (tq, tq), 0)
    col = lax.broadcasted_iota(jnp.int32, (tq, tq), 1)
    own = jnp.where(col <= row, s[:, length - tq:], -jnp.inf)
    s = own if length == tq else jnp.concatenate([s[:, :length - tq], own], axis=1)
    m = jnp.max(s, axis=-1, keepdims=True)
    p = jnp.exp2(s - m).astype(BF16)
    v1 = jnp.concatenate([v, jnp.ones(v.shape, v.dtype)], axis=1)
    acc = jnp.dot(p, v1, preferred_element_type=F32)
    return (acc[:, :HEAD_DIM] / acc[:, HEAD_DIM:]).astype(BF16)


def _tile_rows(ii, tq):
    return slice(ii * tq, (ii + 1) * tq)


def _attend(scores, v_ref, o_ref, *, n_tiles, per_step, tq, heads):
    def run(tiles):
        jobs = [(ii, hh) for ii in tiles for hh in range(heads)]
        for j0 in range(0, len(jobs), SCORE_GROUP):
            group = jobs[j0:j0 + SCORE_GROUP]
            ss = [scores(ii, hh) for ii, hh in group]
            for (ii, hh), s in zip(group, ss):
                o_ref[0, _tile_rows(ii, tq), hh * HEAD_DIM:(hh + 1) * HEAD_DIM] = (
                    _causal_softmax_pv(s, _head(v_ref, hh, slice(0, (ii + 1) * tq)), tq))

    n_steps = n_tiles // per_step
    assert per_step % 2 == 0 and n_steps * per_step == n_tiles
    step = pl.program_id(2)
    for pp in range(n_steps):
        lows = [pp + j * n_steps for j in range(per_step // 2)]
        tiles = [t for lo in lows for t in (lo, n_tiles - 1 - lo)]
        pl.when(step == pp)(functools.partial(run, tiles))


def _fox_kernel(q_ref, k_ref, v_ref, c_ref, o_ref, *, tq, heads, n_tiles, per_step):
    def scores(ii, hh):
        length = (ii + 1) * tq
        q = (_head(q_ref, hh, _tile_rows(ii, tq)).astype(F32) * (QK_SCALE * LOG2E)).astype(BF16)
        c = c_ref[0, hh]
        bias = (c[:, ii * tq:ii * tq + 1] - c[:, :length]) * LOG2E
        return _dot_nt(q, _head(k_ref, hh, slice(0, length))) + bias

    _attend(scores, v_ref, o_ref, n_tiles=n_tiles, per_step=per_step, tq=tq, heads=heads)


def _fox(p3, c4, batch, seq, casts):
    t = _tiles()
    tq, heads, per_step = t["attn_tq"], t["attn_heads"], t["attn_tiles_per_step"]
    hw = heads * HEAD_DIM
    groups = N_HEADS // heads
    n_tiles = seq // tq
    (o,), cast = _call_with_casts(
        functools.partial(_fox_kernel, tq=tq, heads=heads, n_tiles=n_tiles, per_step=per_step),
        casts, (p3, p3, p3, c4),
        out_shape=(jax.ShapeDtypeStruct((batch, seq, WIDTH), BF16),),
        grid=(batch, groups, n_tiles // per_step),
        in_specs=[
            pl.BlockSpec((1, seq, hw), lambda b, g, i: (b, 0, g)),
            pl.BlockSpec((1, seq, hw), lambda b, g, i: (b, 0, groups + g)),
            pl.BlockSpec((1, seq, hw), lambda b, g, i: (b, 0, 2 * groups + g)),
            pl.BlockSpec((1, heads, 1, seq), lambda b, g, i: (b, g, 0, 0)),
        ],
        out_specs=(pl.BlockSpec((1, seq, hw), lambda b, g, i: (b, 0, g)),),
        compiler_params=_params(("arbitrary", "arbitrary", "arbitrary")),
        name="fox_attention",
    )
    return o, cast


def _moba_block_bias(q_raw, km, ii):
    km_hi, km_lo = _split_bf16(km)
    gate = _dot_nt(km_hi, q_raw) + _dot_nt(km_lo, q_raw)
    block = lax.broadcasted_iota(jnp.int32, gate.shape, 0)
    g = jnp.where(block < ii, gate, -jnp.inf)
    rank = jnp.zeros(gate.shape, jnp.int32)
    for n in range(ii):
        gn = g[n:n + 1, :]
        ahead = jnp.logical_or(gn > g, jnp.logical_and(gn == g, n < block))
        rank = rank + ahead.astype(jnp.int32)
    bias_t = jnp.where(rank < MOBA_TOPK, 0.0, MASK_BIAS)
    pad = jnp.zeros((LANES - bias_t.shape[0], bias_t.shape[1]), F32)
    return jnp.concatenate([bias_t, pad], axis=0).T


def _moba_kernel(q_ref, k_ref, v_ref, o_ref, km_scr, *, n_blocks, heads, per_step):
    blk = MOBA_BLOCK

    @pl.when(pl.program_id(2) == 0)
    def _():
        for hh in range(heads):
            for n in range(n_blocks):
                kb = _head(k_ref, hh, _tile_rows(n, blk)).astype(F32)
                km_scr[hh, n:n + 1, :] = jnp.mean(kb, axis=0, keepdims=True)

    def scores(ii, hh):
        length = (ii + 1) * blk
        q_raw = _head(q_ref, hh, _tile_rows(ii, blk))
        q = (q_raw.astype(F32) * (QK_SCALE * LOG2E)).astype(BF16)
        s = _dot_nt(q, _head(k_ref, hh, slice(0, length)))
        if ii <= MOBA_TOPK:
            return s
        bias = _moba_block_bias(q_raw, km_scr[hh], ii)
        past = [s[:, _tile_rows(j, blk)] + bias[:, j:j + 1] for j in range(ii)]
        return jnp.concatenate(past + [s[:, _tile_rows(ii, blk)]], axis=1)

    _attend(scores, v_ref, o_ref, n_tiles=n_blocks, per_step=per_step, tq=blk, heads=heads)


def _moba(qk3, p3, batch, seq, casts):
    n_blocks = seq // MOBA_BLOCK
    assert seq % MOBA_BLOCK == 0 and n_blocks <= LANES
    t = _tiles()
    heads, per_step = t["attn_heads"], t["attn_tiles_per_step"]
    hw = heads * HEAD_DIM
    groups = N_HEADS // heads
    (o,), cast = _call_with_casts(
        functools.partial(_moba_kernel, n_blocks=n_blocks, heads=heads, per_step=per_step), casts,
        (qk3, qk3, p3),
        out_shape=(jax.ShapeDtypeStruct((batch, seq, WIDTH), BF16),),
        grid=(batch, groups, n_blocks // per_step),
        in_specs=[
            pl.BlockSpec((1, seq, hw), lambda b, g, i: (b, 0, g)),
            pl.BlockSpec((1, seq, hw), lambda b, g, i: (b, 0, groups + g)),
            pl.BlockSpec((1, seq, hw), lambda b, g, i: (b, 0, 3 * groups + g)),
        ],
        out_specs=(pl.BlockSpec((1, seq, hw), lambda b, g, i: (b, 0, g)),),
        scratch_shapes=[pltpu.VMEM((heads, n_blocks, HEAD_DIM), F32)],
        compiler_params=_params(("arbitrary", "arbitrary", "arbitrary")),
        name="moba_attention",
    )
    return o, cast


def _mix_resid_kernel(oa_ref, ob_ref, ga_ref, gb_ref, x_ref, wa_ref, wb_ref, wo_ref, g_ref,
                      x1_ref, h_ref):
    ya = jnp.dot(oa_ref[...], wa_ref[...], preferred_element_type=F32)
    yb = jnp.dot(ob_ref[...], wb_ref[...], preferred_element_type=F32)
    ga = jax.nn.sigmoid(ga_ref[...].astype(F32))
    gb = jax.nn.sigmoid(gb_ref[...].astype(F32))
    y = (ga * ya + gb * yb).astype(BF16)
    x1 = x_ref[...] + jnp.dot(y, wo_ref[...], preferred_element_type=F32)
    x1_ref[...] = x1
    h_ref[...] = _rms(x1, g_ref[...]).astype(BF16)


def _mix_resid(oa, ob, p2, x2, wa, wb, wo, g):
    tm = _tiles()["mix_tm"]
    rows, kdim = oa.shape
    d = wa.shape[1]
    ga_blk = (4 * WIDTH) // d
    resident = pl.Buffered(1)
    return pl.pallas_call(
        _mix_resid_kernel,
        out_shape=(jax.ShapeDtypeStruct((rows, d), F32),
                   jax.ShapeDtypeStruct((rows, d), BF16)),
        grid=(rows // tm,),
        in_specs=[
            pl.BlockSpec((tm, kdim), lambda i: (i, 0)),
            pl.BlockSpec((tm, kdim), lambda i: (i, 0)),
            pl.BlockSpec((tm, d), lambda i: (i, ga_blk)),
            pl.BlockSpec((tm, d), lambda i: (i, ga_blk + 1)),
            pl.BlockSpec((tm, d), lambda i: (i, 0)),
            pl.BlockSpec((kdim, d), lambda i: (0, 0), pipeline_mode=resident),
            pl.BlockSpec((kdim, d), lambda i: (0, 0), pipeline_mode=resident),
            pl.BlockSpec((d, d), lambda i: (0, 0), pipeline_mode=resident),
            pl.BlockSpec((1, d), lambda i: (0, 0)),
        ],
        out_specs=(pl.BlockSpec((tm, d), lambda i: (i, 0)),
                   pl.BlockSpec((tm, d), lambda i: (i, 0))),
        compiler_params=_params(("parallel",)),
        name="mix_residual",
    )(oa, ob, p2, p2, x2, wa, wb, wo, g)


def _ffn_kernel(h_ref, x1_ref, wg_ref, wu_ref, wd_ref, gf_ref, o_ref):
    k = pl.program_id(1)

    @pl.when(k == 0)
    def _():
        o_ref[...] = x1_ref[...]

    h = h_ref[...]
    gate = jnp.dot(h, wg_ref[...], preferred_element_type=F32)
    up = jnp.dot(h, wu_ref[...], preferred_element_type=F32)
    a = (gate * jax.nn.sigmoid(gate) * up).astype(BF16)
    o_ref[...] += jnp.dot(a, wd_ref[...], preferred_element_type=F32)

    @pl.when(k == pl.num_programs(1) - 1)
    def _():
        o_ref[...] = _rms(o_ref[...], gf_ref[...])


def _ffn(h2, x1, wg, wu, wd, gf):
    t = _tiles()
    tm, tf = t["ffn_tm"], t["ffn_tf"]
    rows, d = h2.shape
    dff = wd.shape[0]
    return pl.pallas_call(
        _ffn_kernel,
        out_shape=jax.ShapeDtypeStruct((rows, d), F32),
        grid=(rows // tm, dff // tf),
        in_specs=[
            pl.BlockSpec((tm, d), lambda i, k: (i, 0)),
            pl.BlockSpec((tm, d), lambda i, k: (i, 0)),
            pl.BlockSpec((d, tf), lambda i, k: (0, k)),
            pl.BlockSpec((d, tf), lambda i, k: (0, k)),
            pl.BlockSpec((tf, d), lambda i, k: (k, 0)),
            pl.BlockSpec((1, d), lambda i, k: (0, 0)),
        ],
        out_specs=pl.BlockSpec((tm, d), lambda i, k: (i, 0)),
        compiler_params=_params(("parallel", "arbitrary")),
        name="swiglu_ffn",
    )(h2, x1, wg, wu, wd, gf)


def _rope_tables(seq):
    half = HEAD_DIM // 2
    inv_freq = ROPE_THETA ** (-jnp.arange(half, dtype=F32) / half)
    ang = jnp.arange(seq, dtype=F32)[:, None] * inv_freq[None, :]
    cos, sin = jnp.cos(ang), jnp.sin(ang)
    return jnp.concatenate([cos, cos], axis=-1), jnp.concatenate([-sin, sin], axis=-1)


def kernel(x, norm_mix, w_in, b_forget, w_o_fox, w_o_moba, w_out, norm_ffn, w_gate_up, w_down,
           norm_final):
    batch, seq, d = x.shape
    assert w_in.shape[0] == 1
    cs, sn = _rope_tables(seq)
    x2 = x.reshape(batch * seq, d)
    wt = w_in[0].T
    b_pad = jnp.pad(b_forget[0][None, :], ((0, 0), (0, LANES - N_HEADS)))

    dff = w_down.shape[1]
    h, f_pad = _norm(x2, norm_mix[0][None, :], wt)
    p2, (wo_a, wo_b, w_res) = _proj_plain(
        h, wt, [_Cast(w_o_fox[0], d), _Cast(w_o_moba[0], d), _Cast(w_out[0], d)])
    qk, _ = _proj_rot(h, wt, cs, sn, seq, [])
    c = _forget_cumsum(f_pad, b_pad, batch, seq)
    p3 = p2.reshape(batch, seq, p2.shape[1])
    o_a, (wg, wd) = _fox(p3, c.reshape(batch, N_HEADS, 1, seq), batch, seq,
                         [_Cast(w_gate_up[0], dff, 0), _Cast(w_down[0], d)])
    o_b, (wu,) = _moba(qk.reshape(batch, seq, qk.shape[1]), p3, batch, seq,
                       [_Cast(w_gate_up[0], dff, 1)])
    x1, h2 = _mix_resid(o_a.reshape(batch * seq, WIDTH), o_b.reshape(batch * seq, WIDTH), p2, x2,
                        wo_a, wo_b, w_res, norm_ffn[0][None, :])
    out = _ffn(h2, x1, wg, wu, wd, norm_final[None, :])
    return out.reshape(batch, seq, d)
```

```python
import functools
import math
from typing import NamedTuple

import jax
import jax.numpy as jnp
from jax import lax
from jax.experimental import pallas as pl
from jax.experimental.pallas import tpu as pltpu

F32 = jnp.float32
BF16 = jnp.bfloat16

HEAD_DIM = 128
N_HEADS = 8
WIDTH = N_HEADS * HEAD_DIM
MOBA_BLOCK = 256
MOBA_TOPK = 3
ROPE_THETA = 10000.0
RMS_EPS = 1e-6
LOG2E = math.log2(math.e)
QK_SCALE = HEAD_DIM ** -0.5
LANES = 128
BF16_SUBLANES = 16
MASK_BIAS = -1e30
SCORE_GROUP = 16
VMEM_LIMIT = 60 * 1024 * 1024


def _tiles():
    return dict(
        norm_tm=1024, proj_tm=2048, rot_tm=2048, proj_tn=1024,
        attn_tq=256, attn_heads=4, attn_tiles_per_step=4,
        mix_tm=512,
        ffn_tm=1024, ffn_tf=512,
        cum_chunk=256,
    )


def _params(sem):
    return pltpu.CompilerParams(dimension_semantics=sem, vmem_limit_bytes=VMEM_LIMIT)


def _rms(x, g):
    ms = jnp.mean(x * x, axis=-1, keepdims=True)
    return x * lax.rsqrt(ms + RMS_EPS) * g


def _dot_nt(a, b):
    return lax.dot_general(a, b, (((1,), (1,)), ((), ())), preferred_element_type=F32)


class _Cast(NamedTuple):
    src: jax.Array
    cols: int
    col_block: int = 0


def _call_with_casts(kernel_fn, casts, args, *, grid, in_specs, out_specs, out_shape, **kw):
    steps = math.prod(grid)
    strides = [math.prod(grid[a + 1:]) for a in range(len(grid))]

    def step_of(*g):
        return sum(gi * st for gi, st in zip(g, strides))

    n_in, n_out, n_side = len(in_specs), len(out_shape), len(casts)
    side_in, side_out, side_shape = [], [], []
    for c in casts:
        rows = c.src.shape[0] // steps
        assert rows * steps == c.src.shape[0] and rows % BF16_SUBLANES == 0
        side_in.append(pl.BlockSpec((rows, c.cols), lambda *g, c=c: (step_of(*g), c.col_block)))
        side_out.append(pl.BlockSpec((rows, c.cols), lambda *g: (step_of(*g), 0)))
        side_shape.append(jax.ShapeDtypeStruct((c.src.shape[0], c.cols), BF16))

    def body(*refs):
        src = refs[n_in:n_in + n_side]
        dst = refs[n_in + n_side + n_out:n_in + 2 * n_side + n_out]
        for s_ref, d_ref in zip(src, dst):
            d_ref[...] = s_ref[...].astype(BF16)
        kernel_fn(*refs[:n_in], *refs[n_in + n_side:n_in + n_side + n_out],
                  *refs[n_in + 2 * n_side + n_out:])

    res = pl.pallas_call(
        body, grid=grid, in_specs=[*in_specs, *side_in], out_specs=(*out_specs, *side_out),
        out_shape=(*out_shape, *side_shape), **kw)(*args, *[c.src for c in casts])
    return res[:n_out], res[n_out:]


def _split_bf16(v):
    hi = v.astype(BF16)
    return hi, (v - hi.astype(F32)).astype(BF16)


def _norm_kernel(x_ref, g_ref, wf_ref, h_ref, f_ref):
    h = _rms(x_ref[...], g_ref[...]).astype(BF16)
    h_ref[...] = h
    f_ref[...] = _dot_nt(h, wf_ref[...].astype(BF16))


def _norm(x2, g, wt):
    tm = _tiles()["norm_tm"]
    rows, d = x2.shape
    f_blk = (6 * WIDTH) // LANES
    return pl.pallas_call(
        _norm_kernel,
        out_shape=(jax.ShapeDtypeStruct((rows, d), BF16),
                   jax.ShapeDtypeStruct((rows, LANES), F32)),
        grid=(rows // tm,),
        in_specs=[
            pl.BlockSpec((tm, d), lambda i: (i, 0)),
            pl.BlockSpec((1, d), lambda i: (0, 0)),
            pl.BlockSpec((LANES, d), lambda i: (f_blk, 0)),
        ],
        out_specs=(pl.BlockSpec((tm, d), lambda i: (i, 0)),
                   pl.BlockSpec((tm, LANES), lambda i: (i, 0))),
        compiler_params=_params(("parallel",)),
        name="pre_norm",
    )(x2, g, wt)


def _cast_weight_tile(w_ref, w2_ref, w_scr, shift):
    if shift == 0:
        w_scr[...] = w_ref[...].astype(BF16)
    else:
        w_scr[...] = jnp.concatenate([w_ref[shift:, :], w2_ref[...]], axis=0).astype(BF16)


def _proj_kernel(h_ref, w_ref, w2_ref, o_ref, w_scr, *, n_aligned):
    j = pl.program_id(0)

    @pl.when(jnp.logical_and(pl.program_id(1) == 0, j < n_aligned))
    def _():
        _cast_weight_tile(w_ref, w2_ref, w_scr, 0)

    @pl.when(jnp.logical_and(pl.program_id(1) == 0, j >= n_aligned))
    def _():
        _cast_weight_tile(w_ref, w2_ref, w_scr, N_HEADS)

    o_ref[...] = _dot_nt(h_ref[...], w_scr[...]).astype(BF16)


def _proj_rot_kernel(h_ref, w_ref, cs_ref, sn_ref, o_ref, w_scr):
    @pl.when(pl.program_id(1) == 0)
    def _():
        _cast_weight_tile(w_ref, None, w_scr, 0)

    acc = _dot_nt(h_ref[...], w_scr[...])
    cs = cs_ref[...]
    sn = sn_ref[...]
    for hh in range(acc.shape[1] // HEAD_DIM):
        t = acc[:, hh * HEAD_DIM:(hh + 1) * HEAD_DIM]
        r = pltpu.roll(t, HEAD_DIM // 2, axis=1)
        o_ref[:, hh * HEAD_DIM:(hh + 1) * HEAD_DIM] = (t * cs + r * sn).astype(BF16)


def _proj_plain(h, wt, casts):
    t = _tiles()
    tm, tn = t["proj_tm"], t["proj_tn"]
    rows, d = h.shape
    per_group = WIDTH // tn
    n_aligned = 4 * per_group
    n_gate = (2 * d) // tn
    gate0 = (6 * WIDTH) // tn

    def w_blk(j, i):
        aligned = jnp.where(j < 3 * per_group, j, j + 2 * per_group)
        return jnp.where(j < n_aligned, aligned, gate0 + (j - n_aligned)), 0

    def w2_blk(j, i):
        nxt = jnp.maximum(j - n_aligned, 0) + 1
        return (gate0 + nxt) * (tn // N_HEADS), 0

    (p2,), cast = _call_with_casts(
        functools.partial(_proj_kernel, n_aligned=n_aligned), casts, (h, wt, wt),
        out_shape=(jax.ShapeDtypeStruct((rows, (n_aligned + n_gate) * tn), BF16),),
        grid=(n_aligned + n_gate, rows // tm),
        in_specs=[
            pl.BlockSpec((tm, d), lambda j, i: (i, 0)),
            pl.BlockSpec((tn, d), w_blk),
            pl.BlockSpec((N_HEADS, d), w2_blk),
        ],
        out_specs=(pl.BlockSpec((tm, tn), lambda j, i: (i, j)),),
        scratch_shapes=[pltpu.VMEM((tn, d), BF16)],
        compiler_params=_params(("arbitrary", "arbitrary")),
        name="proj_plain",
    )
    return p2, cast


def _proj_rot(h, wt, cs, sn, seq, casts):
    t = _tiles()
    tm, tn = t["rot_tm"], t["proj_tn"]
    rows, d = h.shape
    first = (3 * WIDTH) // tn
    pos_blocks = seq // tm
    (qk,), cast = _call_with_casts(
        _proj_rot_kernel, casts, (h, wt, cs, sn),
        out_shape=(jax.ShapeDtypeStruct((rows, 2 * WIDTH), BF16),),
        grid=((2 * WIDTH) // tn, rows // tm),
        in_specs=[
            pl.BlockSpec((tm, d), lambda j, i: (i, 0)),
            pl.BlockSpec((tn, d), lambda j, i: (first + j, 0)),
            pl.BlockSpec((tm, HEAD_DIM), lambda j, i: (i % pos_blocks, 0)),
            pl.BlockSpec((tm, HEAD_DIM), lambda j, i: (i % pos_blocks, 0)),
        ],
        out_specs=(pl.BlockSpec((tm, tn), lambda j, i: (i, j)),),
        scratch_shapes=[pltpu.VMEM((tn, d), BF16)],
        compiler_params=_params(("arbitrary", "arbitrary")),
        name="proj_rotary",
    )
    return qk, cast


def _forget_kernel(f_ref, b_ref, tri_ref, c_ref, *, chunk):
    z = f_ref[...] + b_ref[...]
    lf = jnp.minimum(z, 0.0) - jnp.log1p(jnp.exp(-jnp.abs(z)))
    lft = lf.T[:N_HEADS, :]
    tri = tri_ref[...]
    carry = jnp.zeros((N_HEADS, 1), F32)
    for n in range(lft.shape[1] // chunk):
        xc = lft[:, n * chunk:(n + 1) * chunk]
        hi = xc.astype(BF16)
        r1 = xc - hi.astype(F32)
        mid = r1.astype(BF16)
        lo = (r1 - mid.astype(F32)).astype(BF16)
        loc = (jnp.dot(hi, tri, preferred_element_type=F32)
               + jnp.dot(mid, tri, preferred_element_type=F32)
               + jnp.dot(lo, tri, preferred_element_type=F32)) + carry
        c_ref[0, :, n * chunk:(n + 1) * chunk] = loc
        carry = loc[:, chunk - 1:chunk]


def _forget_cumsum(f_pad, b_pad, batch, seq):
    chunk = _tiles()["cum_chunk"]
    r = lax.broadcasted_iota(jnp.int32, (chunk, chunk), 0)
    c = lax.broadcasted_iota(jnp.int32, (chunk, chunk), 1)
    tri = (r <= c).astype(BF16)
    return pl.pallas_call(
        functools.partial(_forget_kernel, chunk=chunk),
        out_shape=jax.ShapeDtypeStruct((batch, N_HEADS, seq), F32),
        grid=(batch,),
        in_specs=[
            pl.BlockSpec((seq, LANES), lambda b: (b, 0)),
            pl.BlockSpec((1, LANES), lambda b: (0, 0)),
            pl.BlockSpec((chunk, chunk), lambda b: (0, 0)),
        ],
        out_specs=pl.BlockSpec((1, N_HEADS, seq), lambda b: (b, 0, 0)),
        compiler_params=_params(("parallel",)),
        name="forget_cumsum",
    )(f_pad, b_pad, tri)


def _head(ref, hh, rows=slice(None)):
    return ref[0, rows, hh * HEAD_DIM:(hh + 1) * HEAD_DIM]


def _causal_softmax_pv(s, v, tq):
    length = s.shape[1]
    row = lax.broadcasted_iota(jnp.int32, (tq, tq), 0)
    col = lax.broadcasted_iota(jnp.int32, (tq, tq), 1)
    own = jnp.where(col <= row, s[:, length - tq:], -jnp.inf)
    s = own if length == tq else jnp.concatenate([s[:, :length - tq], own], axis=1)
    m = jnp.max(s, axis=-1, keepdims=True)
    p = jnp.exp2(s - m).astype(BF16)
    v1 = jnp.concatenate([v, jnp.ones(v.shape, v.dtype)], axis=1)
    acc = jnp.dot(p, v1, preferred_element_type=F32)
    return (acc[:, :HEAD_DIM] / acc[:, HEAD_DIM:]).astype(BF16)


def _tile_rows(ii, tq):
    return slice(ii * tq, (ii + 1) * tq)


def _attend(scores, v_ref, o_ref, *, n_tiles, per_step, tq, heads):
    def run(tiles):
        jobs = [(ii, hh) for ii in sorted(tiles) for hh in range(heads)]
        for j0 in range(0, len(jobs), SCORE_GROUP):
            group = jobs[j0:j0 + SCORE_GROUP]
            ss = [scores(ii, hh) for ii, hh in group]
            for (ii, hh), s in zip(group, ss):
                o_ref[0, _tile_rows(ii, tq), hh * HEAD_DIM:(hh + 1) * HEAD_DIM] = (
                    _causal_softmax_pv(s, _head(v_ref, hh, slice(0, (ii + 1) * tq)), tq))

    n_steps = n_tiles // per_step
    assert per_step % 2 == 0 and n_steps * per_step == n_tiles
    step = pl.program_id(2)
    for pp in range(n_steps):
        lows = [pp + j * n_steps for j in range(per_step // 2)]
        tiles = [t for lo in lows for t in (lo, n_tiles - 1 - lo)]
        pl.when(step == pp)(functools.partial(run, tiles))


def _fox_kernel(q_ref, k_ref, v_ref, c_ref, o_ref, *, tq, heads, n_tiles, per_step):
    def scores(ii, hh):
        length = (ii + 1) * tq
        q = (_head(q_ref, hh, _tile_rows(ii, tq)).astype(F32) * (QK_SCALE * LOG2E)).astype(BF16)
        c = c_ref[0, hh]
        bias = (c[:, ii * tq:ii * tq + 1] - c[:, :length]) * LOG2E
        return _dot_nt(q, _head(k_ref, hh, slice(0, length))) + bias

    _attend(scores, v_ref, o_ref, n_tiles=n_tiles, per_step=per_step, tq=tq, heads=heads)


def _fox(p3, c4, batch, seq, casts):
    t = _tiles()
    tq, heads, per_step = t["attn_tq"], t["attn_heads"], t["attn_tiles_per_step"]
    hw = heads * HEAD_DIM
    groups = N_HEADS // heads
    n_tiles = seq // tq
    (o,), cast = _call_with_casts(
        functools.partial(_fox_kernel, tq=tq, heads=heads, n_tiles=n_tiles, per_step=per_step),
        casts, (p3, p3, p3, c4),
        out_shape=(jax.ShapeDtypeStruct((batch, seq, WIDTH), BF16),),
        grid=(batch, groups, n_tiles // per_step),
        in_specs=[
            pl.BlockSpec((1, seq, hw), lambda b, g, i: (b, 0, g)),
            pl.BlockSpec((1, seq, hw), lambda b, g, i: (b, 0, groups + g)),
            pl.BlockSpec((1, seq, hw), lambda b, g, i: (b, 0, 2 * groups + g)),
            pl.BlockSpec((1, heads, 1, seq), lambda b, g, i: (b, g, 0, 0)),
        ],
        out_specs=(pl.BlockSpec((1, seq, hw), lambda b, g, i: (b, 0, g)),),
        compiler_params=_params(("arbitrary", "arbitrary", "arbitrary")),
        name="fox_attention",
    )
    return o, cast


def _moba_block_bias(q_raw, km, ii):
    km_hi, km_lo = _split_bf16(km)
    gate = _dot_nt(km_hi, q_raw) + _dot_nt(km_lo, q_raw)
    block = lax.broadcasted_iota(jnp.int32, gate.shape, 0)
    g = jnp.where(block < ii, gate, -jnp.inf)
    rank = jnp.zeros(gate.shape, jnp.int32)
    for n in range(ii):
        gn = g[n:n + 1, :]
        ahead = jnp.logical_or(gn > g, jnp.logical_and(gn == g, n < block))
        rank = rank + ahead.astype(jnp.int32)
    bias_t = jnp.where(rank < MOBA_TOPK, 0.0, MASK_BIAS)
    pad = jnp.zeros((LANES - bias_t.shape[0], bias_t.shape[1]), F32)
    return jnp.concatenate([bias_t, pad], axis=0).T


def _moba_kernel(q_ref, k_ref, v_ref, o_ref, km_scr, *, n_blocks, heads, per_step):
    blk = MOBA_BLOCK

    @pl.when(pl.program_id(2) == 0)
    def _():
        for hh in range(heads):
            for n in range(n_blocks):
                kb = _head(k_ref, hh, _tile_rows(n, blk)).astype(F32)
                km_scr[hh, n:n + 1, :] = jnp.mean(kb, axis=0, keepdims=True)

    def scores(ii, hh):
        length = (ii + 1) * blk
        q_raw = _head(q_ref, hh, _tile_rows(ii, blk))
        q = (q_raw.astype(F32) * (QK_SCALE * LOG2E)).astype(BF16)
        s = _dot_nt(q, _head(k_ref, hh, slice(0, length)))
        if ii <= MOBA_TOPK:
            return s
        bias = _moba_block_bias(q_raw, km_scr[hh], ii)
        past = [s[:, _tile_rows(j, blk)] + bias[:, j:j + 1] for j in range(ii)]
        return jnp.concatenate(past + [s[:, _tile_rows(ii, blk)]], axis=1)

    _attend(scores, v_ref, o_ref, n_tiles=n_blocks, per_step=per_step, tq=blk, heads=heads)


def _moba(qk3, p3, batch, seq, casts):
    n_blocks = seq // MOBA_BLOCK
    assert seq % MOBA_BLOCK == 0 and n_blocks <= LANES
    t = _tiles()
    heads, per_step = t["attn_heads"], t["attn_tiles_per_step"]
    hw = heads * HEAD_DIM
    groups = N_HEADS // heads
    (o,), cast = _call_with_casts(
        functools.partial(_moba_kernel, n_blocks=n_blocks, heads=heads, per_step=per_step), casts,
        (qk3, qk3, p3),
        out_shape=(jax.ShapeDtypeStruct((batch, seq, WIDTH), BF16),),
        grid=(batch, groups, n_blocks // per_step),
        in_specs=[
            pl.BlockSpec((1, seq, hw), lambda b, g, i: (b, 0, g)),
            pl.BlockSpec((1, seq, hw), lambda b, g, i: (b, 0, groups + g)),
            pl.BlockSpec((1, seq, hw), lambda b, g, i: (b, 0, 3 * groups + g)),
        ],
        out_specs=(pl.BlockSpec((1, seq, hw), lambda b, g, i: (b, 0, g)),),
        scratch_shapes=[pltpu.VMEM((heads, n_blocks, HEAD_DIM), F32)],
        compiler_params=_params(("arbitrary", "arbitrary", "arbitrary")),
        name="moba_attention",
    )
    return o, cast


def _mix_resid_kernel(oa_ref, ob_ref, ga_ref, gb_ref, x_ref, wa_ref, wb_ref, wo_ref, g_ref,
                      x1_ref, h_ref):
    ya = jnp.dot(oa_ref[...], wa_ref[...], preferred_element_type=F32)
    yb = jnp.dot(ob_ref[...], wb_ref[...], preferred_element_type=F32)
    ga = jax.nn.sigmoid(ga_ref[...].astype(F32))
    gb = jax.nn.sigmoid(gb_ref[...].astype(F32))
    y = (ga * ya + gb * yb).astype(BF16)
    x1 = x_ref[...] + jnp.dot(y, wo_ref[...], preferred_element_type=F32)
    x1_ref[...] = x1
    h_ref[...] = _rms(x1, g_ref[...]).astype(BF16)


def _mix_resid(oa, ob, p2, x2, wa, wb, wo, g):
    tm = _tiles()["mix_tm"]
    rows, kdim = oa.shape
    d = wa.shape[1]
    ga_blk = (4 * WIDTH) // d
    resident = pl.Buffered(1)
    return pl.pallas_call(
        _mix_resid_kernel,
        out_shape=(jax.ShapeDtypeStruct((rows, d), F32),
                   jax.ShapeDtypeStruct((rows, d), BF16)),
        grid=(rows // tm,),
        in_specs=[
            pl.BlockSpec((tm, kdim), lambda i: (i, 0)),
            pl.BlockSpec((tm, kdim), lambda i: (i, 0)),
            pl.BlockSpec((tm, d), lambda i: (i, ga_blk)),
            pl.BlockSpec((tm, d), lambda i: (i, ga_blk + 1)),
            pl.BlockSpec((tm, d), lambda i: (i, 0)),
            pl.BlockSpec((kdim, d), lambda i: (0, 0), pipeline_mode=resident),
            pl.BlockSpec((kdim, d), lambda i: (0, 0), pipeline_mode=resident),
            pl.BlockSpec((d, d), lambda i: (0, 0), pipeline_mode=resident),
            pl.BlockSpec((1, d), lambda i: (0, 0)),
        ],
        out_specs=(pl.BlockSpec((tm, d), lambda i: (i, 0)),
                   pl.BlockSpec((tm, d), lambda i: (i, 0))),
        compiler_params=_params(("parallel",)),
        name="mix_residual",
    )(oa, ob, p2, p2, x2, wa, wb, wo, g)


def _ffn_kernel(h_ref, x1_ref, wg_ref, wu_ref, wd_ref, gf_ref, o_ref):
    k = pl.program_id(1)

    @pl.when(k == 0)
    def _():
        o_ref[...] = x1_ref[...]

    h = h_ref[...]
    gate = jnp.dot(h, wg_ref[...], preferred_element_type=F32)
    up = jnp.dot(h, wu_ref[...], preferred_element_type=F32)
    a = (gate * jax.nn.sigmoid(gate) * up).astype(BF16)
    o_ref[...] += jnp.dot(a, wd_ref[...], preferred_element_type=F32)

    @pl.when(k == pl.num_programs(1) - 1)
    def _():
        o_ref[...] = _rms(o_ref[...], gf_ref[...])


def _ffn(h2, x1, wg, wu, wd, gf):
    t = _tiles()
    tm, tf = t["ffn_tm"], t["ffn_tf"]
    rows, d = h2.shape
    dff = wd.shape[0]
    return pl.pallas_call(
        _ffn_kernel,
        out_shape=jax.ShapeDtypeStruct((rows, d), F32),
        grid=(rows // tm, dff // tf),
        in_specs=[
            pl.BlockSpec((tm, d), lambda i, k: (i, 0)),
            pl.BlockSpec((tm, d), lambda i, k: (i, 0)),
            pl.BlockSpec((d, tf), lambda i, k: (0, k)),
            pl.BlockSpec((d, tf), lambda i, k: (0, k)),
            pl.BlockSpec((tf, d), lambda i, k: (k, 0)),
            pl.BlockSpec((1, d), lambda i, k: (0, 0)),
        ],
        out_specs=pl.BlockSpec((tm, d), lambda i, k: (i, 0)),
        compiler_params=_params(("parallel", "arbitrary")),
        name="swiglu_ffn",
    )(h2, x1, wg, wu, wd, gf)


def _rope_tables(seq):
    half = HEAD_DIM // 2
    inv_freq = ROPE_THETA ** (-jnp.arange(half, dtype=F32) / half)
    ang = jnp.arange(seq, dtype=F32)[:, None] * inv_freq[None, :]
    cos, sin = jnp.cos(ang), jnp.sin(ang)
    return jnp.concatenate([cos, cos], axis=-1), jnp.concatenate([-sin, sin], axis=-1)


def kernel(x, norm_mix, w_in, b_forget, w_o_fox, w_o_moba, w_out, norm_ffn, w_gate_up, w_down,
           norm_final):
    batch, seq, d = x.shape
    assert w_in.shape[0] == 1
    cs, sn = _rope_tables(seq)
    x2 = x.reshape(batch * seq, d)
    wt = w_in[0].T
    b_pad = jnp.pad(b_forget[0][None, :], ((0, 0), (0, LANES - N_HEADS)))

    dff = w_down.shape[1]
    h, f_pad = _norm(x2, norm_mix[0][None, :], wt)
    p2, (wo_a, wo_b, w_res) = _proj_plain(
        h, wt, [_Cast(w_o_fox[0], d), _Cast(w_o_moba[0], d), _Cast(w_out[0], d)])
    qk, _ = _proj_rot(h, wt, cs, sn, seq, [])
    c = _forget_cumsum(f_pad, b_pad, batch, seq)
    p3 = p2.reshape(batch, seq, p2.shape[1])
    o_a, (wg, wd) = _fox(p3, c.reshape(batch, N_HEADS, 1, seq), batch, seq,
                         [_Cast(w_gate_up[0], dff, 0), _Cast(w_down[0], d)])
    o_b, (wu,) = _moba(qk.reshape(batch, seq, qk.shape[1]), p3, batch, seq,
                       [_Cast(w_gate_up[0], dff, 1)])
    x1, h2 = _mix_resid(o_a.reshape(batch * seq, WIDTH), o_b.reshape(batch * seq, WIDTH), p2, x2,
                        wo_a, wo_b, w_res, norm_ffn[0][None, :])
    out = _ffn(h2, x1, wg, wu, wd, norm_final[None, :])
    return out.reshape(batch, seq, d)
```

```python
import functools
import math
from typing import NamedTuple

import jax
import jax.numpy as jnp
import numpy as np
from jax import lax
from jax.experimental import pallas as pl
from jax.experimental.pallas import tpu as pltpu

F32 = jnp.float32
BF16 = jnp.bfloat16

HEAD_DIM = 128
N_HEADS = 8
WIDTH = N_HEADS * HEAD_DIM
MOBA_BLOCK = 256
MOBA_TOPK = 3
ROPE_THETA = 10000.0
RMS_EPS = 1e-6
LOG2E = math.log2(math.e)
QK_SCALE = HEAD_DIM ** -0.5
LANES = 128
BF16_SUBLANES = 16
MASK_BIAS = -1e30
SCORE_GROUP = 16
VMEM_LIMIT = 60 * 1024 * 1024


def _tiles():
    return dict(
        norm_tm=1024, proj_tm=2048, rot_tm=2048, proj_tn=1024,
        attn_tq=256, attn_heads=4, attn_tiles_per_step=4,
        mix_tm=512,
        ffn_tm=1024, ffn_tf=512,
        cum_chunk=256,
    )


def _params(sem):
    return pltpu.CompilerParams(dimension_semantics=sem, vmem_limit_bytes=VMEM_LIMIT)


def _rms(x, g):
    ms = jnp.mean(x * x, axis=-1, keepdims=True)
    return x * lax.rsqrt(ms + RMS_EPS) * g


def _dot_nt(a, b):
    return lax.dot_general(a, b, (((1,), (1,)), ((), ())), preferred_element_type=F32)


class _Cast(NamedTuple):
    src: jax.Array
    cols: int
    col_block: int = 0


def _call_with_casts(kernel_fn, casts, args, *, grid, in_specs, out_specs, out_shape, **kw):
    steps = math.prod(grid)
    strides = [math.prod(grid[a + 1:]) for a in range(len(grid))]

    def step_of(*g):
        return sum(gi * st for gi, st in zip(g, strides))

    n_in, n_out, n_side = len(in_specs), len(out_shape), len(casts)
    side_in, side_out, side_shape = [], [], []
    for c in casts:
        rows = c.src.shape[0] // steps
        assert rows * steps == c.src.shape[0] and rows % BF16_SUBLANES == 0
        side_in.append(pl.BlockSpec((rows, c.cols), lambda *g, c=c: (step_of(*g), c.col_block)))
        side_out.append(pl.BlockSpec((rows, c.cols), lambda *g: (step_of(*g), 0)))
        side_shape.append(jax.ShapeDtypeStruct((c.src.shape[0], c.cols), BF16))

    def body(*refs):
        src = refs[n_in:n_in + n_side]
        dst = refs[n_in + n_side + n_out:n_in + 2 * n_side + n_out]
        for s_ref, d_ref in zip(src, dst):
            d_ref[...] = s_ref[...].astype(BF16)
        kernel_fn(*refs[:n_in], *refs[n_in + n_side:n_in + n_side + n_out],
                  *refs[n_in + 2 * n_side + n_out:])

    res = pl.pallas_call(
        body, grid=grid, in_specs=[*in_specs, *side_in], out_specs=(*out_specs, *side_out),
        out_shape=(*out_shape, *side_shape), **kw)(*args, *[c.src for c in casts])
    return res[:n_out], res[n_out:]


def _split_bf16(v):
    hi = v.astype(BF16)
    return hi, (v - hi.astype(F32)).astype(BF16)


def _norm_kernel(x_ref, g_ref, wf_ref, h_ref, f_ref):
    h = _rms(x_ref[...], g_ref[...]).astype(BF16)
    h_ref[...] = h
    f_ref[...] = _dot_nt(h, wf_ref[...].astype(BF16))


def _norm(x2, g, wt):
    tm = _tiles()["norm_tm"]
    rows, d = x2.shape
    f_blk = (6 * WIDTH) // LANES
    return pl.pallas_call(
        _norm_kernel,
        out_shape=(jax.ShapeDtypeStruct((rows, d), BF16),
                   jax.ShapeDtypeStruct((rows, LANES), F32)),
        grid=(rows // tm,),
        in_specs=[
            pl.BlockSpec((tm, d), lambda i: (i, 0)),
            pl.BlockSpec((1, d), lambda i: (0, 0)),
            pl.BlockSpec((LANES, d), lambda i: (f_blk, 0)),
        ],
        out_specs=(pl.BlockSpec((tm, d), lambda i: (i, 0)),
                   pl.BlockSpec((tm, LANES), lambda i: (i, 0))),
        compiler_params=_params(("parallel",)),
        name="pre_norm",
    )(x2, g, wt)


def _cast_weight_tile(w_ref, w2_ref, w_scr, shift):
    if shift == 0:
        w_scr[...] = w_ref[...].astype(BF16)
    else:
        w_scr[...] = jnp.concatenate([w_ref[shift:, :], w2_ref[...]], axis=0).astype(BF16)


def _proj_kernel(h_ref, w_ref, w2_ref, o_ref, w_scr, *, n_aligned):
    j = pl.program_id(0)

    @pl.when(jnp.logical_and(pl.program_id(1) == 0, j < n_aligned))
    def _():
        _cast_weight_tile(w_ref, w2_ref, w_scr, 0)

    @pl.when(jnp.logical_and(pl.program_id(1) == 0, j >= n_aligned))
    def _():
        _cast_weight_tile(w_ref, w2_ref, w_scr, N_HEADS)

    o_ref[...] = _dot_nt(h_ref[...], w_scr[...]).astype(BF16)


def _proj_rot_kernel(h_ref, w_ref, cs_ref, sn_ref, o_ref, w_scr):
    @pl.when(pl.program_id(1) == 0)
    def _():
        _cast_weight_tile(w_ref, None, w_scr, 0)

    acc = _dot_nt(h_ref[...], w_scr[...])
    cs = cs_ref[...]
    sn = sn_ref[...]
    for hh in range(acc.shape[1] // HEAD_DIM):
        t = acc[:, hh * HEAD_DIM:(hh + 1) * HEAD_DIM]
        r = pltpu.roll(t, HEAD_DIM // 2, axis=1)
        o_ref[:, hh * HEAD_DIM:(hh + 1) * HEAD_DIM] = (t * cs + r * sn).astype(BF16)


def _proj_plain(h, wt, casts):
    t = _tiles()
    tm, tn = t["proj_tm"], t["proj_tn"]
    rows, d = h.shape
    per_group = WIDTH // tn
    n_aligned = 4 * per_group
    n_gate = (2 * d) // tn
    gate0 = (6 * WIDTH) // tn

    def w_blk(j, i):
        aligned = jnp.where(j < 3 * per_group, j, j + 2 * per_group)
        return jnp.where(j < n_aligned, aligned, gate0 + (j - n_aligned)), 0

    def w2_blk(j, i):
        nxt = jnp.maximum(j - n_aligned, 0) + 1
        return (gate0 + nxt) * (tn // N_HEADS), 0

    (p2,), cast = _call_with_casts(
        functools.partial(_proj_kernel, n_aligned=n_aligned), casts, (h, wt, wt),
        out_shape=(jax.ShapeDtypeStruct((rows, (n_aligned + n_gate) * tn), BF16),),
        grid=(n_aligned + n_gate, rows // tm),
        in_specs=[
            pl.BlockSpec((tm, d), lambda j, i: (i, 0)),
            pl.BlockSpec((tn, d), w_blk),
            pl.BlockSpec((N_HEADS, d), w2_blk),
        ],
        out_specs=(pl.BlockSpec((tm, tn), lambda j, i: (i, j)),),
        scratch_shapes=[pltpu.VMEM((tn, d), BF16)],
        compiler_params=_params(("arbitrary", "arbitrary")),
        name="proj_plain",
    )
    return p2, cast


def _proj_rot(h, wt, cs, sn, seq, casts):
    t = _tiles()
    tm, tn = t["rot_tm"], t["proj_tn"]
    rows, d = h.shape
    first = (3 * WIDTH) // tn
    pos_blocks = seq // tm
    (qk,), cast = _call_with_casts(
        _proj_rot_kernel, casts, (h, wt, cs, sn),
        out_shape=(jax.ShapeDtypeStruct((rows, 2 * WIDTH), BF16),),
        grid=((2 * WIDTH) // tn, rows // tm),
        in_specs=[
            pl.BlockSpec((tm, d), lambda j, i: (i, 0)),
            pl.BlockSpec((tn, d), lambda j, i: (first + j, 0)),
            pl.BlockSpec((tm, HEAD_DIM), lambda j, i: (i % pos_blocks, 0)),
            pl.BlockSpec((tm, HEAD_DIM), lambda j, i: (i % pos_blocks, 0)),
        ],
        out_specs=(pl.BlockSpec((tm, tn), lambda j, i: (i, j)),),
        scratch_shapes=[pltpu.VMEM((tn, d), BF16)],
        compiler_params=_params(("arbitrary", "arbitrary")),
        name="proj_rotary",
    )
    return qk, cast


def _forget_kernel(f_ref, b_ref, tri_ref, c_ref, *, chunk):
    z = f_ref[...] + b_ref[...]
    lf = jnp.minimum(z, 0.0) - jnp.log1p(jnp.exp(-jnp.abs(z)))
    lft = lf.T[:N_HEADS, :]
    tri = tri_ref[...]
    carry = jnp.zeros((N_HEADS, 1), F32)
    for n in range(lft.shape[1] // chunk):
        xc = lft[:, n * chunk:(n + 1) * chunk]
        hi = xc.astype(BF16)
        r1 = xc - hi.astype(F32)
        mid = r1.astype(BF16)
        lo = (r1 - mid.astype(F32)).astype(BF16)
        loc = (jnp.dot(hi, tri, preferred_element_type=F32)
               + jnp.dot(mid, tri, preferred_element_type=F32)
               + jnp.dot(lo, tri, preferred_element_type=F32)) + carry
        c_ref[0, :, n * chunk:(n + 1) * chunk] = loc
        carry = loc[:, chunk - 1:chunk]


def _forget_cumsum(f_pad, b_pad, batch, seq):
    chunk = _tiles()["cum_chunk"]
    tri = np.triu(np.ones((chunk, chunk), np.float32)).astype(BF16)
    return pl.pallas_call(
        functools.partial(_forget_kernel, chunk=chunk),
        out_shape=jax.ShapeDtypeStruct((batch, N_HEADS, seq), F32),
        grid=(batch,),
        in_specs=[
            pl.BlockSpec((seq, LANES), lambda b: (b, 0)),
            pl.BlockSpec((1, LANES), lambda b: (0, 0)),
            pl.BlockSpec((chunk, chunk), lambda b: (0, 0)),
        ],
        out_specs=pl.BlockSpec((1, N_HEADS, seq), lambda b: (b, 0, 0)),
        compiler_params=_params(("parallel",)),
        name="forget_cumsum",
    )(f_pad, b_pad, tri)


def _head(ref, hh, rows=slice(None)):
    return ref[0, rows, hh * HEAD_DIM:(hh + 1) * HEAD_DIM]


def _causal_softmax_pv(s, v, tq):
    length = s.shape[1]
    row = lax.broadcasted_iota(jnp.int32, (tq, tq), 0)
    col = lax.broadcasted_iota(jnp.int32, (tq, tq), 1)
    own = jnp.where(col <= row, s[:, length - tq:], -jnp.inf)
    s = own if length == tq else jnp.concatenate([s[:, :length - tq], own], axis=1)
    m = jnp.max(s, axis=-1, keepdims=True)
    p = jnp.exp2(s - m).astype(BF16)
    v1 = jnp.concatenate([v, jnp.ones(v.shape, v.dtype)], axis=1)
    acc = jnp.dot(p, v1, preferred_element_type=F32)
    return (acc[:, :HEAD_DIM] / acc[:, HEAD_DIM:]).astype(BF16)


def _tile_rows(ii, tq):
    return slice(ii * tq, (ii + 1) * tq)


def _attend(scores, v_ref, o_ref, *, n_tiles, per_step, tq, heads):
    def run(tiles):
        jobs = [(ii, hh) for ii in sorted(tiles) for hh in range(heads)]
        for j0 in range(0, len(jobs), SCORE_GROUP):
            group = jobs[j0:j0 + SCORE_GROUP]
            ss = [scores(ii, hh) for ii, hh in group]
            for (ii, hh), s in zip(group, ss):
                o_ref[0, _tile_rows(ii, tq), hh * HEAD_DIM:(hh + 1) * HEAD_DIM] = (
                    _causal_softmax_pv(s, _head(v_ref, hh, slice(0, (ii + 1) * tq)), tq))

    n_steps = n_tiles // per_step
    assert per_step % 2 == 0 and n_steps * per_step == n_tiles
    step = pl.program_id(2)
    for pp in range(n_steps):
        lows = [pp + j * n_steps for j in range(per_step // 2)]
        tiles = [t for lo in lows for t in (lo, n_tiles - 1 - lo)]
        pl.when(step == pp)(functools.partial(run, tiles))


def _fox_kernel(q_ref, k_ref, v_ref, c_ref, o_ref, *, tq, heads, n_tiles, per_step):
    def scores(ii, hh):
        length = (ii + 1) * tq
        q = (_head(q_ref, hh, _tile_rows(ii, tq)).astype(F32) * (QK_SCALE * LOG2E)).astype(BF16)
        c = c_ref[0, hh:hh + 1, :]
        for gg in range(1, N_HEADS // heads):
            c = jnp.where(pl.program_id(1) == gg, c_ref[0, gg * heads + hh:gg * heads + hh + 1, :], c)
        bias = (c[:, ii * tq:ii * tq + 1] - c[:, :length]) * LOG2E
        return _dot_nt(q, _head(k_ref, hh, slice(0, length))) + bias

    _attend(scores, v_ref, o_ref, n_tiles=n_tiles, per_step=per_step, tq=tq, heads=heads)


def _fox(p3, c, batch, seq, casts):
    t = _tiles()
    tq, heads, per_step = t["attn_tq"], t["attn_heads"], t["attn_tiles_per_step"]
    hw = heads * HEAD_DIM
    groups = N_HEADS // heads
    n_tiles = seq // tq
    (o,), cast = _call_with_casts(
        functools.partial(_fox_kernel, tq=tq, heads=heads, n_tiles=n_tiles, per_step=per_step),
        casts, (p3, p3, p3, c),
        out_shape=(jax.ShapeDtypeStruct((batch, seq, WIDTH), BF16),),
        grid=(batch, groups, n_tiles // per_step),
        in_specs=[
            pl.BlockSpec((1, seq, hw), lambda b, g, i: (b, 0, g)),
            pl.BlockSpec((1, seq, hw), lambda b, g, i: (b, 0, groups + g)),
            pl.BlockSpec((1, seq, hw), lambda b, g, i: (b, 0, 2 * groups + g)),
            pl.BlockSpec((1, N_HEADS, seq), lambda b, g, i: (b, 0, 0)),
        ],
        out_specs=(pl.BlockSpec((1, seq, hw), lambda b, g, i: (b, 0, g)),),
        compiler_params=_params(("arbitrary", "arbitrary", "arbitrary")),
        name="fox_attention",
    )
    return o, cast


def _moba_block_bias(q_raw, km, ii):
    km_hi, km_lo = _split_bf16(km)
    gate = _dot_nt(km_hi, q_raw) + _dot_nt(km_lo, q_raw)
    block = lax.broadcasted_iota(jnp.int32, gate.shape, 0)
    g = jnp.where(block < ii, gate, -jnp.inf)
    rank = jnp.zeros(gate.shape, jnp.int32)
    for n in range(ii):
        gn = g[n:n + 1, :]
        ahead = jnp.logical_or(gn > g, jnp.logical_and(gn == g, n < block))
        rank = rank + ahead.astype(jnp.int32)
    bias_t = jnp.where(rank < MOBA_TOPK, 0.0, MASK_BIAS)
    pad = jnp.zeros((LANES - bias_t.shape[0], bias_t.shape[1]), F32)
    return jnp.concatenate([bias_t, pad], axis=0).T


def _moba_kernel(q_ref, k_ref, v_ref, o_ref, km_scr, *, n_blocks, heads, per_step):
    blk = MOBA_BLOCK

    @pl.when(pl.program_id(2) == 0)
    def _():
        for hh in range(heads):
            for n in range(n_blocks):
                kb = _head(k_ref, hh, _tile_rows(n, blk)).astype(F32)
                km_scr[hh, n:n + 1, :] = jnp.mean(kb, axis=0, keepdims=True)

    def scores(ii, hh):
        length = (ii + 1) * blk
        q_raw = _head(q_ref, hh, _tile_rows(ii, blk))
        q = (q_raw.astype(F32) * (QK_SCALE * LOG2E)).astype(BF16)
        s = _dot_nt(q, _head(k_ref, hh, slice(0, length)))
        if ii <= MOBA_TOPK:
            return s
        bias = _moba_block_bias(q_raw, km_scr[hh], ii)
        past = [s[:, _tile_rows(j, blk)] + bias[:, j:j + 1] for j in range(ii)]
        return jnp.concatenate(past + [s[:, _tile_rows(ii, blk)]], axis=1)

    _attend(scores, v_ref, o_ref, n_tiles=n_blocks, per_step=per_step, tq=blk, heads=heads)


def _moba(qk3, p3, batch, seq, casts):
    n_blocks = seq // MOBA_BLOCK
    assert seq % MOBA_BLOCK == 0 and n_blocks <= LANES
    t = _tiles()
    heads, per_step = t["attn_heads"], t["attn_tiles_per_step"]
    hw = heads * HEAD_DIM
    groups = N_HEADS // heads
    (o,), cast = _call_with_casts(
        functools.partial(_moba_kernel, n_blocks=n_blocks, heads=heads, per_step=per_step), casts,
        (qk3, qk3, p3),
        out_shape=(jax.ShapeDtypeStruct((batch, seq, WIDTH), BF16),),
        grid=(batch, groups, n_blocks // per_step),
        in_specs=[
            pl.BlockSpec((1, seq, hw), lambda b, g, i: (b, 0, g)),
            pl.BlockSpec((1, seq, hw), lambda b, g, i: (b, 0, groups + g)),
            pl.BlockSpec((1, seq, hw), lambda b, g, i: (b, 0, 3 * groups + g)),
        ],
        out_specs=(pl.BlockSpec((1, seq, hw), lambda b, g, i: (b, 0, g)),),
        scratch_shapes=[pltpu.VMEM((heads, n_blocks, HEAD_DIM), F32)],
        compiler_params=_params(("arbitrary", "arbitrary", "arbitrary")),
        name="moba_attention",
    )
    return o, cast


def _mix_resid_kernel(oa_ref, ob_ref, ga_ref, gb_ref, x_ref, wa_ref, wb_ref, wo_ref, g_ref,
                      x1_ref, h_ref):
    ya = jnp.dot(oa_ref[...], wa_ref[...], preferred_element_type=F32)
    yb = jnp.dot(ob_ref[...], wb_ref[...], preferred_element_type=F32)
    ga = jax.nn.sigmoid(ga_ref[...].astype(F32))
    gb = jax.nn.sigmoid(gb_ref[...].astype(F32))
    y = (ga * ya + gb * yb).astype(BF16)
    x1 = x_ref[...] + jnp.dot(y, wo_ref[...], preferred_element_type=F32)
    x1_ref[...] = x1
    h_ref[...] = _rms(x1, g_ref[...]).astype(BF16)


def _mix_resid(oa, ob, p2, x2, wa, wb, wo, g):
    tm = _tiles()["mix_tm"]
    rows, kdim = oa.shape
    d = wa.shape[1]
    ga_blk = (4 * WIDTH) // d
    resident = pl.Buffered(1)
    return pl.pallas_call(
        _mix_resid_kernel,
        out_shape=(jax.ShapeDtypeStruct((rows, d), F32),
                   jax.ShapeDtypeStruct((rows, d), BF16)),
        grid=(rows // tm,),
        in_specs=[
            pl.BlockSpec((tm, kdim), lambda i: (i, 0)),
            pl.BlockSpec((tm, kdim), lambda i: (i, 0)),
            pl.BlockSpec((tm, d), lambda i: (i, ga_blk)),
            pl.BlockSpec((tm, d), lambda i: (i, ga_blk + 1)),
            pl.BlockSpec((tm, d), lambda i: (i, 0)),
            pl.BlockSpec((kdim, d), lambda i: (0, 0), pipeline_mode=resident),
            pl.BlockSpec((kdim, d), lambda i: (0, 0), pipeline_mode=resident),
            pl.BlockSpec((d, d), lambda i: (0, 0), pipeline_mode=resident),
            pl.BlockSpec((1, d), lambda i: (0, 0)),
        ],
        out_specs=(pl.BlockSpec((tm, d), lambda i: (i, 0)),
                   pl.BlockSpec((tm, d), lambda i: (i, 0))),
        compiler_params=_params(("parallel",)),
        name="mix_residual",
    )(oa, ob, p2, p2, x2, wa, wb, wo, g)


def _ffn_kernel(h_ref, x1_ref, wg_ref, wu_ref, wd_ref, gf_ref, o_ref):
    k = pl.program_id(1)

    @pl.when(k == 0)
    def _():
        o_ref[...] = x1_ref[...]

    h = h_ref[...]
    gate = jnp.dot(h, wg_ref[...], preferred_element_type=F32)
    up = jnp.dot(h, wu_ref[...], preferred_element_type=F32)
    a = (gate * jax.nn.sigmoid(gate) * up).astype(BF16)
    o_ref[...] += jnp.dot(a, wd_ref[...], preferred_element_type=F32)

    @pl.when(k == pl.num_programs(1) - 1)
    def _():
        o_ref[...] = _rms(o_ref[...], gf_ref[...])


def _ffn(h2, x1, wg, wu, wd, gf):
    t = _tiles()
    tm, tf = t["ffn_tm"], t["ffn_tf"]
    rows, d = h2.shape
    dff = wd.shape[0]
    return pl.pallas_call(
        _ffn_kernel,
        out_shape=jax.ShapeDtypeStruct((rows, d), F32),
        grid=(rows // tm, dff // tf),
        in_specs=[
            pl.BlockSpec((tm, d), lambda i, k: (i, 0)),
            pl.BlockSpec((tm, d), lambda i, k: (i, 0)),
            pl.BlockSpec((d, tf), lambda i, k: (0, k)),
            pl.BlockSpec((d, tf), lambda i, k: (0, k)),
            pl.BlockSpec((tf, d), lambda i, k: (k, 0)),
            pl.BlockSpec((1, d), lambda i, k: (0, 0)),
        ],
        out_specs=pl.BlockSpec((tm, d), lambda i, k: (i, 0)),
        compiler_params=_params(("parallel", "arbitrary")),
        name="swiglu_ffn",
    )(h2, x1, wg, wu, wd, gf)


def _rope_tables(seq):
    half = HEAD_DIM // 2
    inv_freq = ROPE_THETA ** (-np.arange(half, dtype=np.float64) / half)
    ang = np.arange(seq, dtype=np.float64)[:, None] * inv_freq[None, :]
    cos, sin = np.cos(ang), np.sin(ang)
    return (np.concatenate([cos, cos], axis=-1).astype(np.float32),
            np.concatenate([-sin, sin], axis=-1).astype(np.float32))


def kernel(x, norm_mix, w_in, b_forget, w_o_fox, w_o_moba, w_out, norm_ffn, w_gate_up, w_down,
           norm_final):
    batch, seq, d = x.shape
    assert w_in.shape[0] == 1
    cs, sn = _rope_tables(seq)
    x2 = x.reshape(batch * seq, d)
    wt = w_in[0].T
    b_pad = jnp.pad(b_forget[0][None, :], ((0, 0), (0, LANES - N_HEADS)))

    dff = w_down.shape[1]
    h, f_pad = _norm(x2, norm_mix[0][None, :], wt)
    p2, (wo_a, wo_b, w_res) = _proj_plain(
        h, wt, [_Cast(w_o_fox[0], d), _Cast(w_o_moba[0], d), _Cast(w_out[0], d)])
    qk, _ = _proj_rot(h, wt, cs, sn, seq, [])
    c = _forget_cumsum(f_pad, b_pad, batch, seq)
    p3 = p2.reshape(batch, seq, p2.shape[1])
    o_a, (wg, wd) = _fox(p3, c, batch, seq,
                         [_Cast(w_gate_up[0], dff, 0), _Cast(w_down[0], d)])
    o_b, (wu,) = _moba(qk.reshape(batch, seq, qk.shape[1]), p3, batch, seq,
                       [_Cast(w_gate_up[0], dff, 1)])
    x1, h2 = _mix_resid(o_a.reshape(batch * seq, WIDTH), o_b.reshape(batch * seq, WIDTH), p2, x2,
                        wo_a, wo_b, w_res, norm_ffn[0][None, :])
    out = _ffn(h2, x1, wg, wu, wd, norm_final[None, :])
    return out.reshape(batch, seq, d)
```

```python
import functools
import math
from typing import NamedTuple

import jax
import jax.numpy as jnp
import numpy as np
from jax import lax
from jax.experimental import pallas as pl
from jax.experimental.pallas import tpu as pltpu

F32 = jnp.float32
BF16 = jnp.bfloat16

HEAD_DIM = 128
N_HEADS = 8
WIDTH = N_HEADS * HEAD_DIM
MOBA_BLOCK = 256
MOBA_TOPK = 3
ROPE_THETA = 10000.0
RMS_EPS = 1e-6
LOG2E = math.log2(math.e)
QK_SCALE = HEAD_DIM ** -0.5
LANES = 128
BF16_SUBLANES = 16
MASK_BIAS = -1e30
SCORE_GROUP = 16
VMEM_LIMIT = 60 * 1024 * 1024


def _tiles():
    return dict(
        norm_tm=1024, proj_tm=2048, rot_tm=2048, proj_tn=1024,
        attn_tq=256, attn_heads=4, attn_tiles_per_step=4,
        mix_tm=512,
        ffn_tm=1024, ffn_tf=512,
        cum_chunk=256,
    )


def _params(sem):
    return pltpu.CompilerParams(dimension_semantics=sem, vmem_limit_bytes=VMEM_LIMIT)


def _rms(x, g):
    ms = jnp.mean(x * x, axis=-1, keepdims=True)
    return x * lax.rsqrt(ms + RMS_EPS) * g


def _dot_nt(a, b):
    return lax.dot_general(a, b, (((1,), (1,)), ((), ())), preferred_element_type=F32)


class _Cast(NamedTuple):
    src: jax.Array
    cols: int
    col_block: int = 0


def _call_with_casts(kernel_fn, casts, args, *, grid, in_specs, out_specs, out_shape, **kw):
    steps = math.prod(grid)
    strides = [math.prod(grid[a + 1:]) for a in range(len(grid))]

    def step_of(*g):
        return sum(gi * st for gi, st in zip(g, strides))

    n_in, n_out, n_side = len(in_specs), len(out_shape), len(casts)
    side_in, side_out, side_shape = [], [], []
    for c in casts:
        rows = c.src.shape[0] // steps
        assert rows * steps == c.src.shape[0] and rows % BF16_SUBLANES == 0
        side_in.append(pl.BlockSpec((rows, c.cols), lambda *g, c=c: (step_of(*g), c.col_block)))
        side_out.append(pl.BlockSpec((rows, c.cols), lambda *g: (step_of(*g), 0)))
        side_shape.append(jax.ShapeDtypeStruct((c.src.shape[0], c.cols), BF16))

    def body(*refs):
        src = refs[n_in:n_in + n_side]
        dst = refs[n_in + n_side + n_out:n_in + 2 * n_side + n_out]
        for s_ref, d_ref in zip(src, dst):
            d_ref[...] = s_ref[...].astype(BF16)
        kernel_fn(*refs[:n_in], *refs[n_in + n_side:n_in + n_side + n_out],
                  *refs[n_in + 2 * n_side + n_out:])

    res = pl.pallas_call(
        body, grid=grid, in_specs=[*in_specs, *side_in], out_specs=(*out_specs, *side_out),
        out_shape=(*out_shape, *side_shape), **kw)(*args, *[c.src for c in casts])
    return res[:n_out], res[n_out:]


def _split_bf16(v):
    hi = v.astype(BF16)
    return hi, (v - hi.astype(F32)).astype(BF16)


def _norm_kernel(x_ref, g_ref, wf_ref, h_ref, f_ref):
    h = _rms(x_ref[...], g_ref[...]).astype(BF16)
    h_ref[...] = h
    f_ref[...] = _dot_nt(h, wf_ref[...].astype(BF16))


def _norm(x2, g, wt):
    tm = _tiles()["norm_tm"]
    rows, d = x2.shape
    f_blk = (6 * WIDTH) // LANES
    return pl.pallas_call(
        _norm_kernel,
        out_shape=(jax.ShapeDtypeStruct((rows, d), BF16),
                   jax.ShapeDtypeStruct((rows, LANES), F32)),
        grid=(rows // tm,),
        in_specs=[
            pl.BlockSpec((tm, d), lambda i: (i, 0)),
            pl.BlockSpec((1, d), lambda i: (0, 0)),
            pl.BlockSpec((LANES, d), lambda i: (f_blk, 0)),
        ],
        out_specs=(pl.BlockSpec((tm, d), lambda i: (i, 0)),
                   pl.BlockSpec((tm, LANES), lambda i: (i, 0))),
        compiler_params=_params(("parallel",)),
        name="pre_norm",
    )(x2, g, wt)


def _cast_weight_tile(w_ref, w2_ref, shift):
    if shift == 0:
        return w_ref[...].astype(BF16)
    return jnp.concatenate([w_ref[shift:, :], w2_ref[...]], axis=0).astype(BF16)


def _project(h_ref, w_scr, fresh_weight, epilogue):
    first = pl.program_id(1) == 0
    for cond, make in fresh_weight:
        @pl.when(jnp.logical_and(first, cond))
        def _(make=make):
            w = make()
            w_scr[...] = w
            epilogue(_dot_nt(h_ref[...], w))

    @pl.when(jnp.logical_not(first))
    def _():
        epilogue(_dot_nt(h_ref[...], w_scr[...]))


def _proj_kernel(h_ref, w_ref, w2_ref, o_ref, w_scr, *, n_aligned):
    j = pl.program_id(0)

    def store(acc):
        o_ref[...] = acc.astype(BF16)

    _project(h_ref, w_scr,
             [(j < n_aligned, lambda: _cast_weight_tile(w_ref, w2_ref, 0)),
              (j >= n_aligned, lambda: _cast_weight_tile(w_ref, w2_ref, N_HEADS))], store)


def _proj_rot_kernel(h_ref, w_ref, cs_ref, sn_ref, o_ref, w_scr):
    def rotate(acc):
        cs = cs_ref[...]
        sn = sn_ref[...]
        for hh in range(acc.shape[1] // HEAD_DIM):
            t = acc[:, hh * HEAD_DIM:(hh + 1) * HEAD_DIM]
            r = pltpu.roll(t, HEAD_DIM // 2, axis=1)
            o_ref[:, hh * HEAD_DIM:(hh + 1) * HEAD_DIM] = (t * cs + r * sn).astype(BF16)

    _project(h_ref, w_scr, [(True, lambda: _cast_weight_tile(w_ref, None, 0))], rotate)


def _proj_plain(h, wt, casts):
    t = _tiles()
    tm, tn = t["proj_tm"], t["proj_tn"]
    rows, d = h.shape
    per_group = WIDTH // tn
    n_aligned = 4 * per_group
    n_gate = (2 * d) // tn
    gate0 = (6 * WIDTH) // tn

    def w_blk(j, i):
        aligned = jnp.where(j < 3 * per_group, j, j + 2 * per_group)
        return jnp.where(j < n_aligned, aligned, gate0 + (j - n_aligned)), 0

    def w2_blk(j, i):
        nxt = jnp.maximum(j - n_aligned, 0) + 1
        return (gate0 + nxt) * (tn // N_HEADS), 0

    (p2,), cast = _call_with_casts(
        functools.partial(_proj_kernel, n_aligned=n_aligned), casts, (h, wt, wt),
        out_shape=(jax.ShapeDtypeStruct((rows, (n_aligned + n_gate) * tn), BF16),),
        grid=(n_aligned + n_gate, rows // tm),
        in_specs=[
            pl.BlockSpec((tm, d), lambda j, i: (i, 0)),
            pl.BlockSpec((tn, d), w_blk),
            pl.BlockSpec((N_HEADS, d), w2_blk),
        ],
        out_specs=(pl.BlockSpec((tm, tn), lambda j, i: (i, j)),),
        scratch_shapes=[pltpu.VMEM((tn, d), BF16)],
        compiler_params=_params(("arbitrary", "arbitrary")),
        name="proj_plain",
    )
    return p2, cast


def _proj_rot(h, wt, cs, sn, seq, casts):
    t = _tiles()
    tm, tn = t["rot_tm"], t["proj_tn"]
    rows, d = h.shape
    first = (3 * WIDTH) // tn
    pos_blocks = seq // tm
    (qk,), cast = _call_with_casts(
        _proj_rot_kernel, casts, (h, wt, cs, sn),
        out_shape=(jax.ShapeDtypeStruct((rows, 2 * WIDTH), BF16),),
        grid=((2 * WIDTH) // tn, rows // tm),
        in_specs=[
            pl.BlockSpec((tm, d), lambda j, i: (i, 0)),
            pl.BlockSpec((tn, d), lambda j, i: (first + j, 0)),
            pl.BlockSpec((tm, HEAD_DIM), lambda j, i: (i % pos_blocks, 0)),
            pl.BlockSpec((tm, HEAD_DIM), lambda j, i: (i % pos_blocks, 0)),
        ],
        out_specs=(pl.BlockSpec((tm, tn), lambda j, i: (i, j)),),
        scratch_shapes=[pltpu.VMEM((tn, d), BF16)],
        compiler_params=_params(("arbitrary", "arbitrary")),
        name="proj_rotary",
    )
    return qk, cast


def _forget_kernel(f_ref, b_ref, tri_ref, c_ref, *, chunk):
    z = f_ref[...] + b_ref[...]
    lf = jnp.minimum(z, 0.0) - jnp.log1p(jnp.exp(-jnp.abs(z)))
    lft = lf.T[:N_HEADS, :]
    tri = tri_ref[...]
    carry = jnp.zeros((N_HEADS, 1), F32)
    for n in range(lft.shape[1] // chunk):
        xc = lft[:, n * chunk:(n + 1) * chunk]
        hi = xc.astype(BF16)
        r1 = xc - hi.astype(F32)
        mid = r1.astype(BF16)
        lo = (r1 - mid.astype(F32)).astype(BF16)
        loc = (jnp.dot(hi, tri, preferred_element_type=F32)
               + jnp.dot(mid, tri, preferred_element_type=F32)
               + jnp.dot(lo, tri, preferred_element_type=F32)) + carry
        c_ref[0, :, n * chunk:(n + 1) * chunk] = loc
        carry = loc[:, chunk - 1:chunk]


def _forget_cumsum(f_pad, b_pad, batch, seq):
    chunk = _tiles()["cum_chunk"]
    tri = np.triu(np.ones((chunk, chunk), np.float32)).astype(BF16)
    return pl.pallas_call(
        functools.partial(_forget_kernel, chunk=chunk),
        out_shape=jax.ShapeDtypeStruct((batch, N_HEADS, seq), F32),
        grid=(batch,),
        in_specs=[
            pl.BlockSpec((seq, LANES), lambda b: (b, 0)),
            pl.BlockSpec((1, LANES), lambda b: (0, 0)),
            pl.BlockSpec((chunk, chunk), lambda b: (0, 0)),
        ],
        out_specs=pl.BlockSpec((1, N_HEADS, seq), lambda b: (b, 0, 0)),
        compiler_params=_params(("parallel",)),
        name="forget_cumsum",
    )(f_pad, b_pad, tri)


def _head(ref, hh, rows=slice(None)):
    return ref[0, rows, hh * HEAD_DIM:(hh + 1) * HEAD_DIM]


def _causal_softmax_pv(s, v, tq):
    length = s.shape[1]
    row = lax.broadcasted_iota(jnp.int32, (tq, tq), 0)
    col = lax.broadcasted_iota(jnp.int32, (tq, tq), 1)
    own = jnp.where(col <= row, s[:, length - tq:], -jnp.inf)
    s = own if length == tq else jnp.concatenate([s[:, :length - tq], own], axis=1)
    m = jnp.max(s, axis=-1, keepdims=True)
    p = jnp.exp2(s - m).astype(BF16)
    v1 = jnp.concatenate([v, jnp.ones(v.shape, v.dtype)], axis=1)
    acc = jnp.dot(p, v1, preferred_element_type=F32)
    return (acc[:, :HEAD_DIM] / acc[:, HEAD_DIM:]).astype(BF16)


def _tile_rows(ii, tq):
    return slice(ii * tq, (ii + 1) * tq)


def _attend(scores, v_ref, o_ref, *, n_tiles, per_step, tq, heads):
    def run(tiles):
        jobs = [(ii, hh) for ii in sorted(tiles) for hh in range(heads)]
        for j0 in range(0, len(jobs), SCORE_GROUP):
            group = jobs[j0:j0 + SCORE_GROUP]
            ss = [scores(ii, hh) for ii, hh in group]
            for (ii, hh), s in zip(group, ss):
                o_ref[0, _tile_rows(ii, tq), hh * HEAD_DIM:(hh + 1) * HEAD_DIM] = (
                    _causal_softmax_pv(s, _head(v_ref, hh, slice(0, (ii + 1) * tq)), tq))

    n_steps = n_tiles // per_step
    assert per_step % 2 == 0 and n_steps * per_step == n_tiles
    step = pl.program_id(2)
    for pp in range(n_steps):
        lows = [pp + j * n_steps for j in range(per_step // 2)]
        tiles = [t for lo in lows for t in (lo, n_tiles - 1 - lo)]
        pl.when(step == pp)(functools.partial(run, tiles))


def _fox_kernel(q_ref, k_ref, v_ref, c_ref, o_ref, *, tq, heads, n_tiles, per_step):
    def scores(ii, hh):
        length = (ii + 1) * tq
        q = (_head(q_ref, hh, _tile_rows(ii, tq)).astype(F32) * (QK_SCALE * LOG2E)).astype(BF16)
        c = c_ref[0, hh:hh + 1, :]
        for gg in range(1, N_HEADS // heads):
            c = jnp.where(pl.program_id(1) == gg, c_ref[0, gg * heads + hh:gg * heads + hh + 1, :], c)
        bias = (c[:, ii * tq:ii * tq + 1] - c[:, :length]) * LOG2E
        return _dot_nt(q, _head(k_ref, hh, slice(0, length))) + bias

    _attend(scores, v_ref, o_ref, n_tiles=n_tiles, per_step=per_step, tq=tq, heads=heads)


def _fox(p3, c, batch, seq, casts):
    t = _tiles()
    tq, heads, per_step = t["attn_tq"], t["attn_heads"], t["attn_tiles_per_step"]
    hw = heads * HEAD_DIM
    groups = N_HEADS // heads
    n_tiles = seq // tq
    (o,), cast = _call_with_casts(
        functools.partial(_fox_kernel, tq=tq, heads=heads, n_tiles=n_tiles, per_step=per_step),
        casts, (p3, p3, p3, c),
        out_shape=(jax.ShapeDtypeStruct((batch, seq, WIDTH), BF16),),
        grid=(batch, groups, n_tiles // per_step),
        in_specs=[
            pl.BlockSpec((1, seq, hw), lambda b, g, i: (b, 0, g)),
            pl.BlockSpec((1, seq, hw), lambda b, g, i: (b, 0, groups + g)),
            pl.BlockSpec((1, seq, hw), lambda b, g, i: (b, 0, 2 * groups + g)),
            pl.BlockSpec((1, N_HEADS, seq), lambda b, g, i: (b, 0, 0)),
        ],
        out_specs=(pl.BlockSpec((1, seq, hw), lambda b, g, i: (b, 0, g)),),
        compiler_params=_params(("arbitrary", "arbitrary", "arbitrary")),
        name="fox_attention",
    )
    return o, cast


def _moba_block_bias(q_raw, km, ii):
    km_hi, km_lo = _split_bf16(km)
    gate = _dot_nt(km_hi, q_raw) + _dot_nt(km_lo, q_raw)
    block = lax.broadcasted_iota(jnp.int32, gate.shape, 0)
    g = jnp.where(block < ii, gate, -jnp.inf)
    rank = jnp.zeros(gate.shape, jnp.int32)
    for n in range(ii):
        gn = g[n:n + 1, :]
        ahead = jnp.logical_or(gn > g, jnp.logical_and(gn == g, n < block))
        rank = rank + ahead.astype(jnp.int32)
    bias_t = jnp.where(rank < MOBA_TOPK, 0.0, MASK_BIAS)
    pad = jnp.zeros((LANES - bias_t.shape[0], bias_t.shape[1]), F32)
    return jnp.concatenate([bias_t, pad], axis=0).T


def _moba_kernel(q_ref, k_ref, v_ref, o_ref, km_scr, *, n_blocks, heads, per_step):
    blk = MOBA_BLOCK

    @pl.when(pl.program_id(2) == 0)
    def _():
        for hh in range(heads):
            for n in range(n_blocks):
                kb = _head(k_ref, hh, _tile_rows(n, blk)).astype(F32)
                km_scr[hh, n:n + 1, :] = jnp.mean(kb, axis=0, keepdims=True)

    def scores(ii, hh):
        length = (ii + 1) * blk
        q_raw = _head(q_ref, hh, _tile_rows(ii, blk))
        q = (q_raw.astype(F32) * (QK_SCALE * LOG2E)).astype(BF16)
        s = _dot_nt(q, _head(k_ref, hh, slice(0, length)))
        if ii <= MOBA_TOPK:
            return s
        bias = _moba_block_bias(q_raw, km_scr[hh], ii)
        past = [s[:, _tile_rows(j, blk)] + bias[:, j:j + 1] for j in range(ii)]
        return jnp.concatenate(past + [s[:, _tile_rows(ii, blk)]], axis=1)

    _attend(scores, v_ref, o_ref, n_tiles=n_blocks, per_step=per_step, tq=blk, heads=heads)


def _moba(qk3, p3, batch, seq, casts):
    n_blocks = seq // MOBA_BLOCK
    assert seq % MOBA_BLOCK == 0 and n_blocks <= LANES
    t = _tiles()
    heads, per_step = t["attn_heads"], t["attn_tiles_per_step"]
    hw = heads * HEAD_DIM
    groups = N_HEADS // heads
    (o,), cast = _call_with_casts(
        functools.partial(_moba_kernel, n_blocks=n_blocks, heads=heads, per_step=per_step), casts,
        (qk3, qk3, p3),
        out_shape=(jax.ShapeDtypeStruct((batch, seq, WIDTH), BF16),),
        grid=(batch, groups, n_blocks // per_step),
        in_specs=[
            pl.BlockSpec((1, seq, hw), lambda b, g, i: (b, 0, g)),
            pl.BlockSpec((1, seq, hw), lambda b, g, i: (b, 0, groups + g)),
            pl.BlockSpec((1, seq, hw), lambda b, g, i: (b, 0, 3 * groups + g)),
        ],
        out_specs=(pl.BlockSpec((1, seq, hw), lambda b, g, i: (b, 0, g)),),
        scratch_shapes=[pltpu.VMEM((heads, n_blocks, HEAD_DIM), F32)],
        compiler_params=_params(("arbitrary", "arbitrary", "arbitrary")),
        name="moba_attention",
    )
    return o, cast


def _mix_resid_kernel(oa_ref, ob_ref, ga_ref, gb_ref, x_ref, wa_ref, wb_ref, wo_ref, g_ref,
                      x1_ref, h_ref):
    ya = jnp.dot(oa_ref[...], wa_ref[...], preferred_element_type=F32)
    yb = jnp.dot(ob_ref[...], wb_ref[...], preferred_element_type=F32)
    ga = jax.nn.sigmoid(ga_ref[...].astype(F32))
    gb = jax.nn.sigmoid(gb_ref[...].astype(F32))
    y = (ga * ya + gb * yb).astype(BF16)
    x1 = x_ref[...] + jnp.dot(y, wo_ref[...], preferred_element_type=F32)
    x1_ref[...] = x1
    h_ref[...] = _rms(x1, g_ref[...]).astype(BF16)


def _mix_resid(oa, ob, p2, x2, wa, wb, wo, g):
    tm = _tiles()["mix_tm"]
    rows, kdim = oa.shape
    d = wa.shape[1]
    ga_blk = (4 * WIDTH) // d
    resident = pl.Buffered(1)
    return pl.pallas_call(
        _mix_resid_kernel,
        out_shape=(jax.ShapeDtypeStruct((rows, d), F32),
                   jax.ShapeDtypeStruct((rows, d), BF16)),
        grid=(rows // tm,),
        in_specs=[
            pl.BlockSpec((tm, kdim), lambda i: (i, 0)),
            pl.BlockSpec((tm, kdim), lambda i: (i, 0)),
            pl.BlockSpec((tm, d), lambda i: (i, ga_blk)),
            pl.BlockSpec((tm, d), lambda i: (i, ga_blk + 1)),
            pl.BlockSpec((tm, d), lambda i: (i, 0)),
            pl.BlockSpec((kdim, d), lambda i: (0, 0), pipeline_mode=resident),
            pl.BlockSpec((kdim, d), lambda i: (0, 0), pipeline_mode=resident),
            pl.BlockSpec((d, d), lambda i: (0, 0), pipeline_mode=resident),
            pl.BlockSpec((1, d), lambda i: (0, 0)),
        ],
        out_specs=(pl.BlockSpec((tm, d), lambda i: (i, 0)),
                   pl.BlockSpec((tm, d), lambda i: (i, 0))),
        compiler_params=_params(("parallel",)),
        name="mix_residual",
    )(oa, ob, p2, p2, x2, wa, wb, wo, g)


def _ffn_kernel(h_ref, x1_ref, wg_ref, wu_ref, wd_ref, gf_ref, o_ref):
    k = pl.program_id(1)

    def delta():
        h = h_ref[...]
        gate = jnp.dot(h, wg_ref[...], preferred_element_type=F32)
        up = jnp.dot(h, wu_ref[...], preferred_element_type=F32)
        a = (gate * jax.nn.sigmoid(gate) * up).astype(BF16)
        return jnp.dot(a, wd_ref[...], preferred_element_type=F32)

    @pl.when(k == 0)
    def _():
        o_ref[...] = x1_ref[...] + delta()

    last = pl.num_programs(1) - 1

    @pl.when(jnp.logical_and(k > 0, k < last))
    def _():
        o_ref[...] += delta()

    @pl.when(k == last)
    def _():
        o_ref[...] = _rms(o_ref[...] + delta(), gf_ref[...])


def _ffn(h2, x1, wg, wu, wd, gf):
    t = _tiles()
    tm, tf = t["ffn_tm"], t["ffn_tf"]
    rows, d = h2.shape
    dff = wd.shape[0]
    assert dff // tf >= 2
    return pl.pallas_call(
        _ffn_kernel,
        out_shape=jax.ShapeDtypeStruct((rows, d), F32),
        grid=(rows // tm, dff // tf),
        in_specs=[
            pl.BlockSpec((tm, d), lambda i, k: (i, 0)),
            pl.BlockSpec((tm, d), lambda i, k: (i, 0)),
            pl.BlockSpec((d, tf), lambda i, k: (0, k)),
            pl.BlockSpec((d, tf), lambda i, k: (0, k)),
            pl.BlockSpec((tf, d), lambda i, k: (k, 0)),
            pl.BlockSpec((1, d), lambda i, k: (0, 0)),
        ],
        out_specs=pl.BlockSpec((tm, d), lambda i, k: (i, 0)),
        compiler_params=_params(("parallel", "arbitrary")),
        name="swiglu_ffn",
    )(h2, x1, wg, wu, wd, gf)


def _rope_tables(seq):
    half = HEAD_DIM // 2
    inv_freq = ROPE_THETA ** (-np.arange(half, dtype=np.float64) / half)
    ang = np.arange(seq, dtype=np.float64)[:, None] * inv_freq[None, :]
    cos, sin = np.cos(ang), np.sin(ang)
    return (np.concatenate([cos, cos], axis=-1).astype(np.float32),
            np.concatenate([-sin, sin], axis=-1).astype(np.float32))


def kernel(x, norm_mix, w_in, b_forget, w_o_fox, w_o_moba, w_out, norm_ffn, w_gate_up, w_down,
           norm_final):
    batch, seq, d = x.shape
    assert w_in.shape[0] == 1
    cs, sn = _rope_tables(seq)
    x2 = x.reshape(batch * seq, d)
    wt = w_in[0].T
    b_pad = jnp.pad(b_forget[0][None, :], ((0, 0), (0, LANES - N_HEADS)))

    dff = w_down.shape[1]
    h, f_pad = _norm(x2, norm_mix[0][None, :], wt)
    p2, (wo_a, wo_b, w_res) = _proj_plain(
        h, wt, [_Cast(w_o_fox[0], d), _Cast(w_o_moba[0], d), _Cast(w_out[0], d)])
    qk, _ = _proj_rot(h, wt, cs, sn, seq, [])
    c = _forget_cumsum(f_pad, b_pad, batch, seq)
    p3 = p2.reshape(batch, seq, p2.shape[1])
    o_a, (wg, wd) = _fox(p3, c, batch, seq,
                         [_Cast(w_gate_up[0], dff, 0), _Cast(w_down[0], d)])
    o_b, (wu,) = _moba(qk.reshape(batch, seq, qk.shape[1]), p3, batch, seq,
                       [_Cast(w_gate_up[0], dff, 1)])
    x1, h2 = _mix_resid(o_a.reshape(batch * seq, WIDTH), o_b.reshape(batch * seq, WIDTH), p2, x2,
                        wo_a, wo_b, w_res, norm_ffn[0][None, :])
    out = _ffn(h2, x1, wg, wu, wd, norm_final[None, :])
    return out.reshape(batch, seq, d)
```

```python
import functools
import math
from typing import NamedTuple

import jax
import jax.numpy as jnp
import numpy as np
from jax import lax
from jax.experimental import pallas as pl
from jax.experimental.pallas import tpu as pltpu

F32 = jnp.float32
BF16 = jnp.bfloat16

HEAD_DIM = 128
N_HEADS = 8
WIDTH = N_HEADS * HEAD_DIM
MOBA_BLOCK = 256
MOBA_TOPK = 3
ROPE_THETA = 10000.0
RMS_EPS = 1e-6
LOG2E = math.log2(math.e)
QK_SCALE = HEAD_DIM ** -0.5
LANES = 128
BF16_SUBLANES = 16
MASK_BIAS = -1e30
SCORE_GROUP = 16
VMEM_LIMIT = 60 * 1024 * 1024


def _tiles():
    return dict(
        norm_tm=1024, proj_tm=2048, rot_tm=2048, proj_tn=1024,
        attn_tq=256, attn_heads=4, attn_tiles_per_step=4,
        mix_tm=512,
        ffn_tm=1024, ffn_tf=512,
        cum_chunk=256,
    )


def _params(sem):
    return pltpu.CompilerParams(dimension_semantics=sem, vmem_limit_bytes=VMEM_LIMIT)


def _rms(x, g):
    ms = jnp.mean(x * x, axis=-1, keepdims=True)
    return x * lax.rsqrt(ms + RMS_EPS) * g


def _dot_nt(a, b):
    return lax.dot_general(a, b, (((1,), (1,)), ((), ())), preferred_element_type=F32)


class _Cast(NamedTuple):
    src: jax.Array
    cols: int
    col_block: int = 0


def _call_with_casts(kernel_fn, casts, args, *, grid, in_specs, out_specs, out_shape, **kw):
    steps = math.prod(grid)
    strides = [math.prod(grid[a + 1:]) for a in range(len(grid))]

    def step_of(*g):
        return sum(gi * st for gi, st in zip(g, strides))

    n_in, n_out, n_side = len(in_specs), len(out_shape), len(casts)
    side_in, side_out, side_shape = [], [], []
    for c in casts:
        rows = c.src.shape[0] // steps
        assert rows * steps == c.src.shape[0] and rows % BF16_SUBLANES == 0
        side_in.append(pl.BlockSpec((rows, c.cols), lambda *g, c=c: (step_of(*g), c.col_block)))
        side_out.append(pl.BlockSpec((rows, c.cols), lambda *g: (step_of(*g), 0)))
        side_shape.append(jax.ShapeDtypeStruct((c.src.shape[0], c.cols), BF16))

    def body(*refs):
        src = refs[n_in:n_in + n_side]
        dst = refs[n_in + n_side + n_out:n_in + 2 * n_side + n_out]
        for s_ref, d_ref in zip(src, dst):
            d_ref[...] = s_ref[...].astype(BF16)
        kernel_fn(*refs[:n_in], *refs[n_in + n_side:n_in + n_side + n_out],
                  *refs[n_in + 2 * n_side + n_out:])

    res = pl.pallas_call(
        body, grid=grid, in_specs=[*in_specs, *side_in], out_specs=(*out_specs, *side_out),
        out_shape=(*out_shape, *side_shape), **kw)(*args, *[c.src for c in casts])
    return res[:n_out], res[n_out:]


def _split_bf16(v):
    hi = v.astype(BF16)
    return hi, (v - hi.astype(F32)).astype(BF16)


def _norm_kernel(x_ref, g_ref, wf_ref, h_ref, f_ref):
    h = _rms(x_ref[...], g_ref[...]).astype(BF16)
    h_ref[...] = h
    f_ref[...] = _dot_nt(h, wf_ref[...].astype(BF16))


def _norm(x2, g, wt):
    tm = _tiles()["norm_tm"]
    rows, d = x2.shape
    f_blk = (6 * WIDTH) // LANES
    return pl.pallas_call(
        _norm_kernel,
        out_shape=(jax.ShapeDtypeStruct((rows, d), BF16),
                   jax.ShapeDtypeStruct((rows, LANES), F32)),
        grid=(rows // tm,),
        in_specs=[
            pl.BlockSpec((tm, d), lambda i: (i, 0)),
            pl.BlockSpec((1, d), lambda i: (0, 0)),
            pl.BlockSpec((LANES, d), lambda i: (f_blk, 0)),
        ],
        out_specs=(pl.BlockSpec((tm, d), lambda i: (i, 0)),
                   pl.BlockSpec((tm, LANES), lambda i: (i, 0))),
        compiler_params=_params(("parallel",)),
        name="pre_norm",
    )(x2, g, wt)


def _cast_weight_tile(w_ref, w2_ref, w_scr, shift):
    if shift == 0:
        w_scr[...] = w_ref[...].astype(BF16)
    else:
        w_scr[...] = jnp.concatenate([w_ref[shift:, :], w2_ref[...]], axis=0).astype(BF16)


def _proj_kernel(h_ref, w_ref, w2_ref, o_ref, w_scr, *, n_aligned):
    j = pl.program_id(0)

    @pl.when(jnp.logical_and(pl.program_id(1) == 0, j < n_aligned))
    def _():
        _cast_weight_tile(w_ref, w2_ref, w_scr, 0)

    @pl.when(jnp.logical_and(pl.program_id(1) == 0, j >= n_aligned))
    def _():
        _cast_weight_tile(w_ref, w2_ref, w_scr, N_HEADS)

    o_ref[...] = _dot_nt(h_ref[...], w_scr[...]).astype(BF16)


def _proj_rot_kernel(h_ref, w_ref, cs_ref, sn_ref, o_ref, w_scr):
    @pl.when(pl.program_id(1) == 0)
    def _():
        _cast_weight_tile(w_ref, None, w_scr, 0)

    acc = _dot_nt(h_ref[...], w_scr[...])
    cs = cs_ref[...]
    sn = sn_ref[...]
    for hh in range(acc.shape[1] // HEAD_DIM):
        t = acc[:, hh * HEAD_DIM:(hh + 1) * HEAD_DIM]
        r = pltpu.roll(t, HEAD_DIM // 2, axis=1)
        o_ref[:, hh * HEAD_DIM:(hh + 1) * HEAD_DIM] = (t * cs + r * sn).astype(BF16)


def _proj_plain(h, wt, casts):
    t = _tiles()
    tm, tn = t["proj_tm"], t["proj_tn"]
    rows, d = h.shape
    per_group = WIDTH // tn
    n_aligned = 4 * per_group
    n_gate = (2 * d) // tn
    gate0 = (6 * WIDTH) // tn

    def w_blk(j, i):
        aligned = jnp.where(j < 3 * per_group, j, j + 2 * per_group)
        return jnp.where(j < n_aligned, aligned, gate0 + (j - n_aligned)), 0

    def w2_blk(j, i):
        nxt = jnp.maximum(j - n_aligned, 0) + 1
        return (gate0 + nxt) * (tn // N_HEADS), 0

    (p2,), cast = _call_with_casts(
        functools.partial(_proj_kernel, n_aligned=n_aligned), casts, (h, wt, wt),
        out_shape=(jax.ShapeDtypeStruct((rows, (n_aligned + n_gate) * tn), BF16),),
        grid=(n_aligned + n_gate, rows // tm),
        in_specs=[
            pl.BlockSpec((tm, d), lambda j, i: (i, 0)),
            pl.BlockSpec((tn, d), w_blk),
            pl.BlockSpec((N_HEADS, d), w2_blk),
        ],
        out_specs=(pl.BlockSpec((tm, tn), lambda j, i: (i, j)),),
        scratch_shapes=[pltpu.VMEM((tn, d), BF16)],
        compiler_params=_params(("arbitrary", "arbitrary")),
        name="proj_plain",
    )
    return p2, cast


def _proj_rot(h, wt, cs, sn, seq, casts):
    t = _tiles()
    tm, tn = t["rot_tm"], t["proj_tn"]
    rows, d = h.shape
    first = (3 * WIDTH) // tn
    pos_blocks = seq // tm
    (qk,), cast = _call_with_casts(
        _proj_rot_kernel, casts, (h, wt, cs, sn),
        out_shape=(jax.ShapeDtypeStruct((rows, 2 * WIDTH), BF16),),
        grid=((2 * WIDTH) // tn, rows // tm),
        in_specs=[
            pl.BlockSpec((tm, d), lambda j, i: (i, 0)),
            pl.BlockSpec((tn, d), lambda j, i: (first + j, 0)),
            pl.BlockSpec((tm, HEAD_DIM), lambda j, i: (i % pos_blocks, 0)),
            pl.BlockSpec((tm, HEAD_DIM), lambda j, i: (i % pos_blocks, 0)),
        ],
        out_specs=(pl.BlockSpec((tm, tn), lambda j, i: (i, j)),),
        scratch_shapes=[pltpu.VMEM((tn, d), BF16)],
        compiler_params=_params(("arbitrary", "arbitrary")),
        name="proj_rotary",
    )
    return qk, cast


def _forget_kernel(f_ref, b_ref, tri_ref, c_ref, *, chunk):
    z = f_ref[...] + b_ref[...]
    lf = jnp.minimum(z, 0.0) - jnp.log1p(jnp.exp(-jnp.abs(z)))
    lft = lf.T[:N_HEADS, :]
    tri = tri_ref[...]
    carry = jnp.zeros((N_HEADS, 1), F32)
    for n in range(lft.shape[1] // chunk):
        xc = lft[:, n * chunk:(n + 1) * chunk]
        hi = xc.astype(BF16)
        r1 = xc - hi.astype(F32)
        mid = r1.astype(BF16)
        lo = (r1 - mid.astype(F32)).astype(BF16)
        loc = (jnp.dot(hi, tri, preferred_element_type=F32)
               + jnp.dot(mid, tri, preferred_element_type=F32)
               + jnp.dot(lo, tri, preferred_element_type=F32)) + carry
        c_ref[0, :, n * chunk:(n + 1) * chunk] = loc
        carry = loc[:, chunk - 1:chunk]


def _forget_cumsum(f_pad, b_pad, batch, seq):
    chunk = _tiles()["cum_chunk"]
    tri = np.triu(np.ones((chunk, chunk), np.float32)).astype(BF16)
    return pl.pallas_call(
        functools.partial(_forget_kernel, chunk=chunk),
        out_shape=jax.ShapeDtypeStruct((batch, N_HEADS, seq), F32),
        grid=(batch,),
        in_specs=[
            pl.BlockSpec((seq, LANES), lambda b: (b, 0)),
            pl.BlockSpec((1, LANES), lambda b: (0, 0)),
            pl.BlockSpec((chunk, chunk), lambda b: (0, 0)),
        ],
        out_specs=pl.BlockSpec((1, N_HEADS, seq), lambda b: (b, 0, 0)),
        compiler_params=_params(("parallel",)),
        name="forget_cumsum",
    )(f_pad, b_pad, tri)


def _head(ref, hh, rows=slice(None)):
    return ref[0, rows, hh * HEAD_DIM:(hh + 1) * HEAD_DIM]


def _causal_softmax_pv(s, v, tq):
    length = s.shape[1]
    row = lax.broadcasted_iota(jnp.int32, (tq, tq), 0)
    col = lax.broadcasted_iota(jnp.int32, (tq, tq), 1)
    own = jnp.where(col <= row, s[:, length - tq:], -jnp.inf)
    s = own if length == tq else jnp.concatenate([s[:, :length - tq], own], axis=1)
    m = jnp.max(s, axis=-1, keepdims=True)
    p = jnp.exp2(s - m).astype(BF16)
    v1 = jnp.concatenate([v, jnp.ones(v.shape, v.dtype)], axis=1)
    acc = jnp.dot(p, v1, preferred_element_type=F32)
    return (acc[:, :HEAD_DIM] / acc[:, HEAD_DIM:]).astype(BF16)


def _tile_rows(ii, tq):
    return slice(ii * tq, (ii + 1) * tq)


def _attend(scores, v_ref, o_ref, *, n_tiles, per_step, tq, heads):
    def run(tiles):
        jobs = [(ii, hh) for ii in sorted(tiles) for hh in range(heads)]
        for j0 in range(0, len(jobs), SCORE_GROUP):
            group = jobs[j0:j0 + SCORE_GROUP]
            ss = [scores(ii, hh) for ii, hh in group]
            for (ii, hh), s in zip(group, ss):
                o_ref[0, _tile_rows(ii, tq), hh * HEAD_DIM:(hh + 1) * HEAD_DIM] = (
                    _causal_softmax_pv(s, _head(v_ref, hh, slice(0, (ii + 1) * tq)), tq))

    n_steps = n_tiles // per_step
    assert per_step % 2 == 0 and n_steps * per_step == n_tiles
    step = pl.program_id(2)
    for pp in range(n_steps):
        lows = [pp + j * n_steps for j in range(per_step // 2)]
        tiles = [t for lo in lows for t in (lo, n_tiles - 1 - lo)]
        pl.when(step == pp)(functools.partial(run, tiles))


def _fox_kernel(q_ref, k_ref, v_ref, c_ref, o_ref, *, tq, heads, n_tiles, per_step):
    def scores(ii, hh):
        length = (ii + 1) * tq
        q = (_head(q_ref, hh, _tile_rows(ii, tq)).astype(F32) * (QK_SCALE * LOG2E)).astype(BF16)
        c = c_ref[0, hh:hh + 1, :]
        for gg in range(1, N_HEADS // heads):
            c = jnp.where(pl.program_id(1) == gg, c_ref[0, gg * heads + hh:gg * heads + hh + 1, :], c)
        bias = (c[:, ii * tq:ii * tq + 1] - c[:, :length]) * LOG2E
        return _dot_nt(q, _head(k_ref, hh, slice(0, length))) + bias

    _attend(scores, v_ref, o_ref, n_tiles=n_tiles, per_step=per_step, tq=tq, heads=heads)


def _fox(p3, c, batch, seq, casts):
    t = _tiles()
    tq, heads, per_step = t["attn_tq"], t["attn_heads"], t["attn_tiles_per_step"]
    hw = heads * HEAD_DIM
    groups = N_HEADS // heads
    n_tiles = seq // tq
    (o,), cast = _call_with_casts(
        functools.partial(_fox_kernel, tq=tq, heads=heads, n_tiles=n_tiles, per_step=per_step),
        casts, (p3, p3, p3, c),
        out_shape=(jax.ShapeDtypeStruct((batch, seq, WIDTH), BF16),),
        grid=(batch, groups, n_tiles // per_step),
        in_specs=[
            pl.BlockSpec((1, seq, hw), lambda b, g, i: (b, 0, g)),
            pl.BlockSpec((1, seq, hw), lambda b, g, i: (b, 0, groups + g)),
            pl.BlockSpec((1, seq, hw), lambda b, g, i: (b, 0, 2 * groups + g)),
            pl.BlockSpec((1, N_HEADS, seq), lambda b, g, i: (b, 0, 0)),
        ],
        out_specs=(pl.BlockSpec((1, seq, hw), lambda b, g, i: (b, 0, g)),),
        compiler_params=_params(("arbitrary", "arbitrary", "arbitrary")),
        name="fox_attention",
    )
    return o, cast


def _moba_block_bias(q_raw, km, ii):
    km_hi, km_lo = _split_bf16(km)
    gate = _dot_nt(km_hi, q_raw) + _dot_nt(km_lo, q_raw)
    block = lax.broadcasted_iota(jnp.int32, gate.shape, 0)
    g = jnp.where(block < ii, gate, -jnp.inf)
    rank = jnp.zeros(gate.shape, jnp.int32)
    for n in range(ii):
        gn = g[n:n + 1, :]
        ahead = jnp.logical_or(gn > g, jnp.logical_and(gn == g, n < block))
        rank = rank + ahead.astype(jnp.int32)
    bias_t = jnp.where(rank < MOBA_TOPK, 0.0, MASK_BIAS)
    pad = jnp.zeros((LANES - bias_t.shape[0], bias_t.shape[1]), F32)
    return jnp.concatenate([bias_t, pad], axis=0).T


def _moba_kernel(q_ref, k_ref, v_ref, o_ref, km_scr, *, n_blocks, heads, per_step):
    blk = MOBA_BLOCK

    @pl.when(pl.program_id(2) == 0)
    def _():
        for hh in range(heads):
            for n in range(n_blocks):
                kb = _head(k_ref, hh, _tile_rows(n, blk)).astype(F32)
                km_scr[hh, n:n + 1, :] = jnp.mean(kb, axis=0, keepdims=True)

    def scores(ii, hh):
        length = (ii + 1) * blk
        q_raw = _head(q_ref, hh, _tile_rows(ii, blk))
        q = (q_raw.astype(F32) * (QK_SCALE * LOG2E)).astype(BF16)
        s = _dot_nt(q, _head(k_ref, hh, slice(0, length)))
        if ii <= MOBA_TOPK:
            return s
        bias = _moba_block_bias(q_raw, km_scr[hh], ii)
        past = [s[:, _tile_rows(j, blk)] + bias[:, j:j + 1] for j in range(ii)]
        return jnp.concatenate(past + [s[:, _tile_rows(ii, blk)]], axis=1)

    _attend(scores, v_ref, o_ref, n_tiles=n_blocks, per_step=per_step, tq=blk, heads=heads)


def _moba(qk3, p3, batch, seq, casts):
    n_blocks = seq // MOBA_BLOCK
    assert seq % MOBA_BLOCK == 0 and n_blocks <= LANES
    t = _tiles()
    heads, per_step = t["attn_heads"], t["attn_tiles_per_step"]
    hw = heads * HEAD_DIM
    groups = N_HEADS // heads
    (o,), cast = _call_with_casts(
        functools.partial(_moba_kernel, n_blocks=n_blocks, heads=heads, per_step=per_step), casts,
        (qk3, qk3, p3),
        out_shape=(jax.ShapeDtypeStruct((batch, seq, WIDTH), BF16),),
        grid=(batch, groups, n_blocks // per_step),
        in_specs=[
            pl.BlockSpec((1, seq, hw), lambda b, g, i: (b, 0, g)),
            pl.BlockSpec((1, seq, hw), lambda b, g, i: (b, 0, groups + g)),
            pl.BlockSpec((1, seq, hw), lambda b, g, i: (b, 0, 3 * groups + g)),
        ],
        out_specs=(pl.BlockSpec((1, seq, hw), lambda b, g, i: (b, 0, g)),),
        scratch_shapes=[pltpu.VMEM((heads, n_blocks, HEAD_DIM), F32)],
        compiler_params=_params(("arbitrary", "arbitrary", "arbitrary")),
        name="moba_attention",
    )
    return o, cast


def _mix_resid_kernel(oa_ref, ob_ref, ga_ref, gb_ref, x_ref, wa_ref, wb_ref, wo_ref, g_ref,
                      x1_ref, h_ref):
    ya = jnp.dot(oa_ref[...], wa_ref[...], preferred_element_type=F32)
    yb = jnp.dot(ob_ref[...], wb_ref[...], preferred_element_type=F32)
    ga = jax.nn.sigmoid(ga_ref[...].astype(F32))
    gb = jax.nn.sigmoid(gb_ref[...].astype(F32))
    y = (ga * ya + gb * yb).astype(BF16)
    x1 = x_ref[...] + jnp.dot(y, wo_ref[...], preferred_element_type=F32)
    x1_ref[...] = x1
    h_ref[...] = _rms(x1, g_ref[...]).astype(BF16)


def _mix_resid(oa, ob, p2, x2, wa, wb, wo, g):
    tm = _tiles()["mix_tm"]
    rows, kdim = oa.shape
    d = wa.shape[1]
    ga_blk = (4 * WIDTH) // d
    resident = pl.Buffered(1)
    return pl.pallas_call(
        _mix_resid_kernel,
        out_shape=(jax.ShapeDtypeStruct((rows, d), F32),
                   jax.ShapeDtypeStruct((rows, d), BF16)),
        grid=(rows // tm,),
        in_specs=[
            pl.BlockSpec((tm, kdim), lambda i: (i, 0)),
            pl.BlockSpec((tm, kdim), lambda i: (i, 0)),
            pl.BlockSpec((tm, d), lambda i: (i, ga_blk)),
            pl.BlockSpec((tm, d), lambda i: (i, ga_blk + 1)),
            pl.BlockSpec((tm, d), lambda i: (i, 0)),
            pl.BlockSpec((kdim, d), lambda i: (0, 0), pipeline_mode=resident),
            pl.BlockSpec((kdim, d), lambda i: (0, 0), pipeline_mode=resident),
            pl.BlockSpec((d, d), lambda i: (0, 0), pipeline_mode=resident),
            pl.BlockSpec((1, d), lambda i: (0, 0)),
        ],
        out_specs=(pl.BlockSpec((tm, d), lambda i: (i, 0)),
                   pl.BlockSpec((tm, d), lambda i: (i, 0))),
        compiler_params=_params(("parallel",)),
        name="mix_residual",
    )(oa, ob, p2, p2, x2, wa, wb, wo, g)


def _ffn_kernel(h_ref, x1_ref, wg_ref, wu_ref, wd_ref, gf_ref, o_ref):
    k = pl.program_id(1)

    def delta():
        h = h_ref[...]
        gate = jnp.dot(h, wg_ref[...], preferred_element_type=F32)
        up = jnp.dot(h, wu_ref[...], preferred_element_type=F32)
        a = (gate * jax.nn.sigmoid(gate) * up).astype(BF16)
        return jnp.dot(a, wd_ref[...], preferred_element_type=F32)

    @pl.when(k == 0)
    def _():
        o_ref[...] = x1_ref[...] + delta()

    last = pl.num_programs(1) - 1

    @pl.when(jnp.logical_and(k > 0, k < last))
    def _():
        o_ref[...] += delta()

    @pl.when(k == last)
    def _():
        o_ref[...] = _rms(o_ref[...] + delta(), gf_ref[...])


def _ffn(h2, x1, wg, wu, wd, gf):
    t = _tiles()
    tm, tf = t["ffn_tm"], t["ffn_tf"]
    rows, d = h2.shape
    dff = wd.shape[0]
    assert dff // tf >= 2
    return pl.pallas_call(
        _ffn_kernel,
        out_shape=jax.ShapeDtypeStruct((rows, d), F32),
        grid=(rows // tm, dff // tf),
        in_specs=[
            pl.BlockSpec((tm, d), lambda i, k: (i, 0)),
            pl.BlockSpec((tm, d), lambda i, k: (i, 0)),
            pl.BlockSpec((d, tf), lambda i, k: (0, k)),
            pl.BlockSpec((d, tf), lambda i, k: (0, k)),
            pl.BlockSpec((tf, d), lambda i, k: (k, 0)),
            pl.BlockSpec((1, d), lambda i, k: (0, 0)),
        ],
        out_specs=pl.BlockSpec((tm, d), lambda i, k: (i, 0)),
        compiler_params=_params(("parallel", "arbitrary")),
        name="swiglu_ffn",
    )(h2, x1, wg, wu, wd, gf)


def _rope_tables(seq):
    half = HEAD_DIM // 2
    inv_freq = ROPE_THETA ** (-np.arange(half, dtype=np.float64) / half)
    ang = np.arange(seq, dtype=np.float64)[:, None] * inv_freq[None, :]
    cos, sin = np.cos(ang), np.sin(ang)
    return (np.concatenate([cos, cos], axis=-1).astype(np.float32),
            np.concatenate([-sin, sin], axis=-1).astype(np.float32))


def kernel(x, norm_mix, w_in, b_forget, w_o_fox, w_o_moba, w_out, norm_ffn, w_gate_up, w_down,
           norm_final):
    batch, seq, d = x.shape
    assert w_in.shape[0] == 1
    cs, sn = _rope_tables(seq)
    x2 = x.reshape(batch * seq, d)
    wt = w_in[0].T
    b_pad = jnp.pad(b_forget[0][None, :], ((0, 0), (0, LANES - N_HEADS)))

    dff = w_down.shape[1]
    h, f_pad = _norm(x2, norm_mix[0][None, :], wt)
    p2, (wo_a, wo_b, w_res) = _proj_plain(
        h, wt, [_Cast(w_o_fox[0], d), _Cast(w_o_moba[0], d), _Cast(w_out[0], d)])
    qk, _ = _proj_rot(h, wt, cs, sn, seq, [])
    c = _forget_cumsum(f_pad, b_pad, batch, seq)
    p3 = p2.reshape(batch, seq, p2.shape[1])
    o_a, (wg, wd) = _fox(p3, c, batch, seq,
                         [_Cast(w_gate_up[0], dff, 0), _Cast(w_down[0], d)])
    o_b, (wu,) = _moba(qk.reshape(batch, seq, qk.shape[1]), p3, batch, seq,
                       [_Cast(w_gate_up[0], dff, 1)])
    x1, h2 = _mix_resid(o_a.reshape(batch * seq, WIDTH), o_b.reshape(batch * seq, WIDTH), p2, x2,
                        wo_a, wo_b, w_res, norm_ffn[0][None, :])
    out = _ffn(h2, x1, wg, wu, wd, norm_final[None, :])
    return out.reshape(batch, seq, d)
```

```python
import functools
import math
from typing import NamedTuple

import jax
import jax.numpy as jnp
import numpy as np
from jax import lax
from jax.experimental import pallas as pl
from jax.experimental.pallas import tpu as pltpu

F32 = jnp.float32
BF16 = jnp.bfloat16

HEAD_DIM = 128
N_HEADS = 8
WIDTH = N_HEADS * HEAD_DIM
MOBA_BLOCK = 256
MOBA_TOPK = 3
ROPE_THETA = 10000.0
RMS_EPS = 1e-6
LOG2E = math.log2(math.e)
QK_SCALE = HEAD_DIM ** -0.5
LANES = 128
BF16_SUBLANES = 16
MASK_BIAS = -1e30
SCORE_GROUP = 16
VMEM_LIMIT = 60 * 1024 * 1024


def _tiles():
    return dict(
        norm_tm=1024, proj_tm=2048, rot_tm=2048, proj_tn=1024,
        attn_tq=256, attn_heads=4, attn_tiles_per_step=4,
        mix_tm=512,
        ffn_tm=1024, ffn_tf=512,
        cum_chunk=256,
    )


def _params(sem):
    return pltpu.CompilerParams(dimension_semantics=sem, vmem_limit_bytes=VMEM_LIMIT)


def _rms(x, g):
    ms = jnp.mean(x * x, axis=-1, keepdims=True)
    return x * lax.rsqrt(ms + RMS_EPS) * g


def _dot_nt(a, b):
    return lax.dot_general(a, b, (((1,), (1,)), ((), ())), preferred_element_type=F32)


class _Cast(NamedTuple):
    src: jax.Array
    cols: int
    col_block: int = 0


def _call_with_casts(kernel_fn, casts, args, *, grid, in_specs, out_specs, out_shape, **kw):
    steps = math.prod(grid)
    strides = [math.prod(grid[a + 1:]) for a in range(len(grid))]

    def step_of(*g):
        return sum(gi * st for gi, st in zip(g, strides))

    n_in, n_out, n_side = len(in_specs), len(out_shape), len(casts)
    side_in, side_out, side_shape = [], [], []
    for c in casts:
        rows = c.src.shape[0] // steps
        assert rows * steps == c.src.shape[0] and rows % BF16_SUBLANES == 0
        side_in.append(pl.BlockSpec((rows, c.cols), lambda *g, c=c: (step_of(*g), c.col_block)))
        side_out.append(pl.BlockSpec((rows, c.cols), lambda *g: (step_of(*g), 0)))
        side_shape.append(jax.ShapeDtypeStruct((c.src.shape[0], c.cols), BF16))

    def body(*refs):
        src = refs[n_in:n_in + n_side]
        dst = refs[n_in + n_side + n_out:n_in + 2 * n_side + n_out]
        for s_ref, d_ref in zip(src, dst):
            d_ref[...] = s_ref[...].astype(BF16)
        kernel_fn(*refs[:n_in], *refs[n_in + n_side:n_in + n_side + n_out],
                  *refs[n_in + 2 * n_side + n_out:])

    res = pl.pallas_call(
        body, grid=grid, in_specs=[*in_specs, *side_in], out_specs=(*out_specs, *side_out),
        out_shape=(*out_shape, *side_shape), **kw)(*args, *[c.src for c in casts])
    return res[:n_out], res[n_out:]


def _split_bf16(v):
    hi = v.astype(BF16)
    return hi, (v - hi.astype(F32)).astype(BF16)


def _norm_kernel(x_ref, g_ref, wf_ref, h_ref, f_ref):
    h = _rms(x_ref[...], g_ref[...]).astype(BF16)
    h_ref[...] = h
    f_ref[...] = _dot_nt(h, wf_ref[...].astype(BF16))


def _norm(x2, g, wt):
    tm = _tiles()["norm_tm"]
    rows, d = x2.shape
    f_blk = (6 * WIDTH) // LANES
    return pl.pallas_call(
        _norm_kernel,
        out_shape=(jax.ShapeDtypeStruct((rows, d), BF16),
                   jax.ShapeDtypeStruct((rows, LANES), F32)),
        grid=(rows // tm,),
        in_specs=[
            pl.BlockSpec((tm, d), lambda i: (i, 0)),
            pl.BlockSpec((1, d), lambda i: (0, 0)),
            pl.BlockSpec((LANES, d), lambda i: (f_blk, 0)),
        ],
        out_specs=(pl.BlockSpec((tm, d), lambda i: (i, 0)),
                   pl.BlockSpec((tm, LANES), lambda i: (i, 0))),
        compiler_params=_params(("parallel",)),
        name="pre_norm",
    )(x2, g, wt)


def _cast_weight_tile(w_ref, w2_ref, w_scr, shift):
    if shift == 0:
        w_scr[...] = w_ref[...].astype(BF16)
    else:
        w_scr[...] = jnp.concatenate([w_ref[shift:, :], w2_ref[...]], axis=0).astype(BF16)


def _proj_kernel(h_ref, w_ref, w2_ref, o_ref, w_scr, *, n_aligned):
    j = pl.program_id(0)

    @pl.when(jnp.logical_and(pl.program_id(1) == 0, j < n_aligned))
    def _():
        _cast_weight_tile(w_ref, w2_ref, w_scr, 0)

    @pl.when(jnp.logical_and(pl.program_id(1) == 0, j >= n_aligned))
    def _():
        _cast_weight_tile(w_ref, w2_ref, w_scr, N_HEADS)

    o_ref[...] = _dot_nt(h_ref[...], w_scr[...]).astype(BF16)


def _proj_rot_kernel(h_ref, w_ref, cs_ref, sn_ref, o_ref, w_scr):
    @pl.when(pl.program_id(1) == 0)
    def _():
        _cast_weight_tile(w_ref, None, w_scr, 0)

    acc = _dot_nt(h_ref[...], w_scr[...])
    cs = cs_ref[...]
    sn = sn_ref[...]
    for hh in range(acc.shape[1] // HEAD_DIM):
        t = acc[:, hh * HEAD_DIM:(hh + 1) * HEAD_DIM]
        r = pltpu.roll(t, HEAD_DIM // 2, axis=1)
        o_ref[:, hh * HEAD_DIM:(hh + 1) * HEAD_DIM] = (t * cs + r * sn).astype(BF16)


def _proj_plain(h, wt, casts):
    t = _tiles()
    tm, tn = t["proj_tm"], t["proj_tn"]
    rows, d = h.shape
    per_group = WIDTH // tn
    n_aligned = 4 * per_group
    n_gate = (2 * d) // tn
    gate0 = (6 * WIDTH) // tn

    def w_blk(j, i):
        aligned = jnp.where(j < 3 * per_group, j, j + 2 * per_group)
        return jnp.where(j < n_aligned, aligned, gate0 + (j - n_aligned)), 0

    def w2_blk(j, i):
        nxt = jnp.maximum(j - n_aligned, 0) + 1
        return (gate0 + nxt) * (tn // N_HEADS), 0

    (p2,), cast = _call_with_casts(
        functools.partial(_proj_kernel, n_aligned=n_aligned), casts, (h, wt, wt),
        out_shape=(jax.ShapeDtypeStruct((rows, (n_aligned + n_gate) * tn), BF16),),
        grid=(n_aligned + n_gate, rows // tm),
        in_specs=[
            pl.BlockSpec((tm, d), lambda j, i: (i, 0)),
            pl.BlockSpec((tn, d), w_blk),
            pl.BlockSpec((N_HEADS, d), w2_blk),
        ],
        out_specs=(pl.BlockSpec((tm, tn), lambda j, i: (i, j)),),
        scratch_shapes=[pltpu.VMEM((tn, d), BF16)],
        compiler_params=_params(("arbitrary", "arbitrary")),
        name="proj_plain",
    )
    return p2, cast


def _proj_rot(h, wt, cs, sn, seq, casts):
    t = _tiles()
    tm, tn = t["rot_tm"], t["proj_tn"]
    rows, d = h.shape
    first = (3 * WIDTH) // tn
    pos_blocks = seq // tm
    (qk,), cast = _call_with_casts(
        _proj_rot_kernel, casts, (h, wt, cs, sn),
        out_shape=(jax.ShapeDtypeStruct((rows, 2 * WIDTH), BF16),),
        grid=((2 * WIDTH) // tn, rows // tm),
        in_specs=[
            pl.BlockSpec((tm, d), lambda j, i: (i, 0)),
            pl.BlockSpec((tn, d), lambda j, i: (first + j, 0)),
            pl.BlockSpec((tm, HEAD_DIM), lambda j, i: (i % pos_blocks, 0)),
            pl.BlockSpec((tm, HEAD_DIM), lambda j, i: (i % pos_blocks, 0)),
        ],
        out_specs=(pl.BlockSpec((tm, tn), lambda j, i: (i, j)),),
        scratch_shapes=[pltpu.VMEM((tn, d), BF16)],
        compiler_params=_params(("arbitrary", "arbitrary")),
        name="proj_rotary",
    )
    return qk, cast


def _forget_kernel(f_ref, b_ref, tri_ref, c_ref, *, chunk):
    z = f_ref[...] + b_ref[...]
    lf = jnp.minimum(z, 0.0) - jnp.log1p(jnp.exp(-jnp.abs(z)))
    lft = lf.T[:N_HEADS, :]
    tri = tri_ref[...]
    carry = jnp.zeros((N_HEADS, 1), F32)
    for n in range(lft.shape[1] // chunk):
        xc = lft[:, n * chunk:(n + 1) * chunk]
        hi = xc.astype(BF16)
        r1 = xc - hi.astype(F32)
        mid = r1.astype(BF16)
        lo = (r1 - mid.astype(F32)).astype(BF16)
        loc = (jnp.dot(hi, tri, preferred_element_type=F32)
               + jnp.dot(mid, tri, preferred_element_type=F32)
               + jnp.dot(lo, tri, preferred_element_type=F32)) + carry
        c_ref[0, :, n * chunk:(n + 1) * chunk] = loc
        carry = loc[:, chunk - 1:chunk]


def _forget_cumsum(f_pad, b_pad, batch, seq):
    chunk = _tiles()["cum_chunk"]
    tri = np.triu(np.ones((chunk, chunk), np.float32)).astype(BF16)
    return pl.pallas_call(
        functools.partial(_forget_kernel, chunk=chunk),
        out_shape=jax.ShapeDtypeStruct((batch, N_HEADS, seq), F32),
        grid=(batch,),
        in_specs=[
            pl.BlockSpec((seq, LANES), lambda b: (b, 0)),
            pl.BlockSpec((1, LANES), lambda b: (0, 0)),
            pl.BlockSpec((chunk, chunk), lambda b: (0, 0)),
        ],
        out_specs=pl.BlockSpec((1, N_HEADS, seq), lambda b: (b, 0, 0)),
        compiler_params=_params(("parallel",)),
        name="forget_cumsum",
    )(f_pad, b_pad, tri)


def _head(ref, hh, rows=slice(None)):
    return ref[0, rows, hh * HEAD_DIM:(hh + 1) * HEAD_DIM]


def _causal_softmax_pv(s, v, tq):
    length = s.shape[1]
    row = lax.broadcasted_iota(jnp.int32, (tq, tq), 0)
    col = lax.broadcasted_iota(jnp.int32, (tq, tq), 1)
    own = jnp.where(col <= row, s[:, length - tq:], -jnp.inf)
    s = own if length == tq else jnp.concatenate([s[:, :length - tq], own], axis=1)
    m = jnp.max(s, axis=-1, keepdims=True)
    p = jnp.exp2(s - m).astype(BF16)
    v1 = jnp.concatenate([v, jnp.ones(v.shape, v.dtype)], axis=1)
    acc = jnp.dot(p, v1, preferred_element_type=F32)
    return (acc[:, :HEAD_DIM] / acc[:, HEAD_DIM:]).astype(BF16)


def _tile_rows(ii, tq):
    return slice(ii * tq, (ii + 1) * tq)


def _attend(scores, v_ref, o_ref, *, n_tiles, per_step, tq, heads):
    def run(tiles):
        jobs = [(ii, hh) for ii in sorted(tiles) for hh in range(heads)]
        for j0 in range(0, len(jobs), SCORE_GROUP):
            group = jobs[j0:j0 + SCORE_GROUP]
            ss = [scores(ii, hh) for ii, hh in group]
            for (ii, hh), s in zip(group, ss):
                o_ref[0, _tile_rows(ii, tq), hh * HEAD_DIM:(hh + 1) * HEAD_DIM] = (
                    _causal_softmax_pv(s, _head(v_ref, hh, slice(0, (ii + 1) * tq)), tq))

    n_steps = n_tiles // per_step
    assert per_step % 2 == 0 and n_steps * per_step == n_tiles
    step = pl.program_id(2)
    for pp in range(n_steps):
        lows = [pp + j * n_steps for j in range(per_step // 2)]
        tiles = [t for lo in lows for t in (lo, n_tiles - 1 - lo)]
        pl.when(step == pp)(functools.partial(run, tiles))


def _fox_kernel(q_ref, k_ref, v_ref, c_ref, o_ref, *, tq, heads, n_tiles, per_step):
    def scores(ii, hh):
        length = (ii + 1) * tq
        q = (_head(q_ref, hh, _tile_rows(ii, tq)).astype(F32) * (QK_SCALE * LOG2E)).astype(BF16)
        c = c_ref[0, hh:hh + 1, :]
        for gg in range(1, N_HEADS // heads):
            c = jnp.where(pl.program_id(1) == gg, c_ref[0, gg * heads + hh:gg * heads + hh + 1, :], c)
        bias = (c[:, ii * tq:ii * tq + 1] - c[:, :length]) * LOG2E
        return _dot_nt(q, _head(k_ref, hh, slice(0, length))) + bias

    _attend(scores, v_ref, o_ref, n_tiles=n_tiles, per_step=per_step, tq=tq, heads=heads)


def _fox(p3, c, batch, seq, casts):
    t = _tiles()
    tq, heads, per_step = t["attn_tq"], t["attn_heads"], t["attn_tiles_per_step"]
    hw = heads * HEAD_DIM
    groups = N_HEADS // heads
    n_tiles = seq // tq
    (o,), cast = _call_with_casts(
        functools.partial(_fox_kernel, tq=tq, heads=heads, n_tiles=n_tiles, per_step=per_step),
        casts, (p3, p3, p3, c),
        out_shape=(jax.ShapeDtypeStruct((batch, seq, WIDTH), BF16),),
        grid=(batch, groups, n_tiles // per_step),
        in_specs=[
            pl.BlockSpec((1, seq, hw), lambda b, g, i: (b, 0, g)),
            pl.BlockSpec((1, seq, hw), lambda b, g, i: (b, 0, groups + g)),
            pl.BlockSpec((1, seq, hw), lambda b, g, i: (b, 0, 2 * groups + g)),
            pl.BlockSpec((1, N_HEADS, seq), lambda b, g, i: (b, 0, 0)),
        ],
        out_specs=(pl.BlockSpec((1, seq, hw), lambda b, g, i: (b, 0, g)),),
        compiler_params=_params(("arbitrary", "arbitrary", "arbitrary")),
        name="fox_attention",
    )
    return o, cast


def _moba_block_bias(q_raw, km, ii):
    km_hi, km_lo = _split_bf16(km)
    gate = _dot_nt(km_hi, q_raw) + _dot_nt(km_lo, q_raw)
    block = lax.broadcasted_iota(jnp.int32, gate.shape, 0)
    g = jnp.where(block < ii, gate, -jnp.inf)
    rank = jnp.zeros(gate.shape, jnp.int32)
    for n in range(ii):
        gn = g[n:n + 1, :]
        ahead = jnp.logical_or(gn > g, jnp.logical_and(gn == g, n < block))
        rank = rank + ahead.astype(jnp.int32)
    bias_t = jnp.where(rank < MOBA_TOPK, 0.0, MASK_BIAS)
    pad = jnp.zeros((LANES - bias_t.shape[0], bias_t.shape[1]), F32)
    return jnp.concatenate([bias_t, pad], axis=0).T


def _moba_kernel(q_ref, k_ref, v_ref, o_ref, km_scr, *, n_blocks, heads, per_step):
    blk = MOBA_BLOCK

    @pl.when(pl.program_id(2) == 0)
    def _():
        for hh in range(heads):
            for n in range(n_blocks):
                kb = _head(k_ref, hh, _tile_rows(n, blk)).astype(F32)
                km_scr[hh, n:n + 1, :] = jnp.mean(kb, axis=0, keepdims=True)

    def scores(ii, hh):
        length = (ii + 1) * blk
        q_raw = _head(q_ref, hh, _tile_rows(ii, blk))
        q = (q_raw.astype(F32) * (QK_SCALE * LOG2E)).astype(BF16)
        s = _dot_nt(q, _head(k_ref, hh, slice(0, length)))
        if ii <= MOBA_TOPK:
            return s
        bias = _moba_block_bias(q_raw, km_scr[hh], ii)
        past = [s[:, _tile_rows(j, blk)] + bias[:, j:j + 1] for j in range(ii)]
        return jnp.concatenate(past + [s[:, _tile_rows(ii, blk)]], axis=1)

    _attend(scores, v_ref, o_ref, n_tiles=n_blocks, per_step=per_step, tq=blk, heads=heads)


def _moba(qk3, p3, batch, seq, casts):
    n_blocks = seq // MOBA_BLOCK
    assert seq % MOBA_BLOCK == 0 and n_blocks <= LANES
    t = _tiles()
    heads, per_step = t["attn_heads"], t["attn_tiles_per_step"]
    hw = heads * HEAD_DIM
    groups = N_HEADS // heads
    (o,), cast = _call_with_casts(
        functools.partial(_moba_kernel, n_blocks=n_blocks, heads=heads, per_step=per_step), casts,
        (qk3, qk3, p3),
        out_shape=(jax.ShapeDtypeStruct((batch, seq, WIDTH), BF16),),
        grid=(batch, groups, n_blocks // per_step),
        in_specs=[
            pl.BlockSpec((1, seq, hw), lambda b, g, i: (b, 0, g)),
            pl.BlockSpec((1, seq, hw), lambda b, g, i: (b, 0, groups + g)),
            pl.BlockSpec((1, seq, hw), lambda b, g, i: (b, 0, 3 * groups + g)),
        ],
        out_specs=(pl.BlockSpec((1, seq, hw), lambda b, g, i: (b, 0, g)),),
        scratch_shapes=[pltpu.VMEM((heads, n_blocks, HEAD_DIM), F32)],
        compiler_params=_params(("arbitrary", "arbitrary", "arbitrary")),
        name="moba_attention",
    )
    return o, cast


def _mix_resid_kernel(oa_ref, ob_ref, ga_ref, gb_ref, x_ref, wa_ref, wb_ref, wo_ref, g_ref,
                      x1_ref, h_ref):
    ya = jnp.dot(oa_ref[...], wa_ref[...], preferred_element_type=F32)
    yb = jnp.dot(ob_ref[...], wb_ref[...], preferred_element_type=F32)
    ga = jax.nn.sigmoid(ga_ref[...].astype(F32))
    gb = jax.nn.sigmoid(gb_ref[...].astype(F32))
    y = (ga * ya + gb * yb).astype(BF16)
    x1 = x_ref[...] + jnp.dot(y, wo_ref[...], preferred_element_type=F32)
    x1_ref[...] = x1
    h_ref[...] = _rms(x1, g_ref[...]).astype(BF16)


def _mix_resid(oa, ob, p2, x2, wa, wb, wo, g):
    tm = _tiles()["mix_tm"]
    rows, kdim = oa.shape
    d = wa.shape[1]
    ga_blk = (4 * WIDTH) // d
    resident = pl.Buffered(1)
    return pl.pallas_call(
        _mix_resid_kernel,
        out_shape=(jax.ShapeDtypeStruct((rows, d), F32),
                   jax.ShapeDtypeStruct((rows, d), BF16)),
        grid=(rows // tm,),
        in_specs=[
            pl.BlockSpec((tm, kdim), lambda i: (i, 0)),
            pl.BlockSpec((tm, kdim), lambda i: (i, 0)),
            pl.BlockSpec((tm, d), lambda i: (i, ga_blk)),
            pl.BlockSpec((tm, d), lambda i: (i, ga_blk + 1)),
            pl.BlockSpec((tm, d), lambda i: (i, 0)),
            pl.BlockSpec((kdim, d), lambda i: (0, 0), pipeline_mode=resident),
            pl.BlockSpec((kdim, d), lambda i: (0, 0), pipeline_mode=resident),
            pl.BlockSpec((d, d), lambda i: (0, 0), pipeline_mode=resident),
            pl.BlockSpec((1, d), lambda i: (0, 0)),
        ],
        out_specs=(pl.BlockSpec((tm, d), lambda i: (i, 0)),
                   pl.BlockSpec((tm, d), lambda i: (i, 0))),
        compiler_params=_params(("parallel",)),
        name="mix_residual",
    )(oa, ob, p2, p2, x2, wa, wb, wo, g)


def _ffn_kernel(h_ref, x1_ref, wg_ref, wu_ref, wd_ref, gf_ref, o_ref):
    k = pl.program_id(1)

    def delta():
        h = h_ref[...]
        gate = jnp.dot(h, wg_ref[...], preferred_element_type=F32)
        up = jnp.dot(h, wu_ref[...], preferred_element_type=F32)
        a = (gate * jax.nn.sigmoid(gate) * up).astype(BF16)
        return jnp.dot(a, wd_ref[...], preferred_element_type=F32)

    @pl.when(k == 0)
    def _():
        o_ref[...] = delta()

    last = pl.num_programs(1) - 1

    @pl.when(jnp.logical_and(k > 0, k < last))
    def _():
        o_ref[...] += delta()

    @pl.when(k == last)
    def _():
        o_ref[...] = _rms(x1_ref[...] + (o_ref[...] + delta()), gf_ref[...])


def _ffn(h2, x1, wg, wu, wd, gf):
    t = _tiles()
    tm, tf = t["ffn_tm"], t["ffn_tf"]
    rows, d = h2.shape
    dff = wd.shape[0]
    assert dff // tf >= 2
    return pl.pallas_call(
        _ffn_kernel,
        out_shape=jax.ShapeDtypeStruct((rows, d), F32),
        grid=(rows // tm, dff // tf),
        in_specs=[
            pl.BlockSpec((tm, d), lambda i, k: (i, 0)),
            pl.BlockSpec((tm, d), lambda i, k: (jnp.where(k == 0, jnp.maximum(i - 1, 0), i), 0)),
            pl.BlockSpec((d, tf), lambda i, k: (0, k)),
            pl.BlockSpec((d, tf), lambda i, k: (0, k)),
            pl.BlockSpec((tf, d), lambda i, k: (k, 0)),
            pl.BlockSpec((1, d), lambda i, k: (0, 0)),
        ],
        out_specs=pl.BlockSpec((tm, d), lambda i, k: (i, 0)),
        compiler_params=_params(("parallel", "arbitrary")),
        name="swiglu_ffn",
    )(h2, x1, wg, wu, wd, gf)


def _rope_tables(seq):
    half = HEAD_DIM // 2
    inv_freq = ROPE_THETA ** (-np.arange(half, dtype=np.float64) / half)
    ang = np.arange(seq, dtype=np.float64)[:, None] * inv_freq[None, :]
    cos, sin = np.cos(ang), np.sin(ang)
    return (np.concatenate([cos, cos], axis=-1).astype(np.float32),
            np.concatenate([-sin, sin], axis=-1).astype(np.float32))


def kernel(x, norm_mix, w_in, b_forget, w_o_fox, w_o_moba, w_out, norm_ffn, w_gate_up, w_down,
           norm_final):
    batch, seq, d = x.shape
    assert w_in.shape[0] == 1
    cs, sn = _rope_tables(seq)
    x2 = x.reshape(batch * seq, d)
    wt = w_in[0].T
    b_pad = jnp.pad(b_forget[0][None, :], ((0, 0), (0, LANES - N_HEADS)))

    dff = w_down.shape[1]
    h, f_pad = _norm(x2, norm_mix[0][None, :], wt)
    p2, (wo_a, wo_b, w_res) = _proj_plain(
        h, wt, [_Cast(w_o_fox[0], d), _Cast(w_o_moba[0], d), _Cast(w_out[0], d)])
    qk, _ = _proj_rot(h, wt, cs, sn, seq, [])
    c = _forget_cumsum(f_pad, b_pad, batch, seq)
    p3 = p2.reshape(batch, seq, p2.shape[1])
    o_a, (wg, wd) = _fox(p3, c, batch, seq,
                         [_Cast(w_gate_up[0], dff, 0), _Cast(w_down[0], d)])
    o_b, (wu,) = _moba(qk.reshape(batch, seq, qk.shape[1]), p3, batch, seq,
                       [_Cast(w_gate_up[0], dff, 1)])
    x1, h2 = _mix_resid(o_a.reshape(batch * seq, WIDTH), o_b.reshape(batch * seq, WIDTH), p2, x2,
                        wo_a, wo_b, w_res, norm_ffn[0][None, :])
    out = _ffn(h2, x1, wg, wu, wd, norm_final[None, :])
    return out.reshape(batch, seq, d)
```

```python
import functools
import math
from typing import NamedTuple

import jax
import jax.numpy as jnp
import numpy as np
from jax import lax
from jax.experimental import pallas as pl
from jax.experimental.pallas import tpu as pltpu

F32 = jnp.float32
BF16 = jnp.bfloat16

HEAD_DIM = 128
N_HEADS = 8
WIDTH = N_HEADS * HEAD_DIM
MOBA_BLOCK = 256
MOBA_TOPK = 3
ROPE_THETA = 10000.0
RMS_EPS = 1e-6
LOG2E = math.log2(math.e)
QK_SCALE = HEAD_DIM ** -0.5
LANES = 128
BF16_SUBLANES = 16
MASK_BIAS = -1e30
SCORE_GROUP = 16
VMEM_LIMIT = 60 * 1024 * 1024


def _tiles():
    return dict(
        norm_tm=1024, proj_tm=2048, rot_tm=2048, proj_tn=1024,
        attn_tq=256, attn_heads=4, attn_tiles_per_step=4, fox_tiles_per_step=8,
        mix_tm=512,
        ffn_tm=1024, ffn_tf=512,
        cum_chunk=256,
    )


def _params(sem):
    return pltpu.CompilerParams(dimension_semantics=sem, vmem_limit_bytes=VMEM_LIMIT)


def _rms(x, g):
    ms = jnp.mean(x * x, axis=-1, keepdims=True)
    return x * lax.rsqrt(ms + RMS_EPS) * g


def _dot_nt(a, b):
    return lax.dot_general(a, b, (((1,), (1,)), ((), ())), preferred_element_type=F32)


class _Cast(NamedTuple):
    src: jax.Array
    cols: int
    col_block: int = 0


def _call_with_casts(kernel_fn, casts, args, *, grid, in_specs, out_specs, out_shape, **kw):
    steps = math.prod(grid)
    strides = [math.prod(grid[a + 1:]) for a in range(len(grid))]

    def step_of(*g):
        return sum(gi * st for gi, st in zip(g, strides))

    n_in, n_out, n_side = len(in_specs), len(out_shape), len(casts)
    side_in, side_out, side_shape = [], [], []
    for c in casts:
        rows = c.src.shape[0] // steps
        assert rows * steps == c.src.shape[0] and rows % BF16_SUBLANES == 0
        side_in.append(pl.BlockSpec((rows, c.cols), lambda *g, c=c: (step_of(*g), c.col_block)))
        side_out.append(pl.BlockSpec((rows, c.cols), lambda *g: (step_of(*g), 0)))
        side_shape.append(jax.ShapeDtypeStruct((c.src.shape[0], c.cols), BF16))

    def body(*refs):
        src = refs[n_in:n_in + n_side]
        dst = refs[n_in + n_side + n_out:n_in + 2 * n_side + n_out]
        for s_ref, d_ref in zip(src, dst):
            d_ref[...] = s_ref[...].astype(BF16)
        kernel_fn(*refs[:n_in], *refs[n_in + n_side:n_in + n_side + n_out],
                  *refs[n_in + 2 * n_side + n_out:])

    res = pl.pallas_call(
        body, grid=grid, in_specs=[*in_specs, *side_in], out_specs=(*out_specs, *side_out),
        out_shape=(*out_shape, *side_shape), **kw)(*args, *[c.src for c in casts])
    return res[:n_out], res[n_out:]


def _split_bf16(v):
    hi = v.astype(BF16)
    return hi, (v - hi.astype(F32)).astype(BF16)


def _norm_kernel(x_ref, g_ref, wf_ref, h_ref, f_ref):
    h = _rms(x_ref[...], g_ref[...]).astype(BF16)
    h_ref[...] = h
    f_ref[...] = _dot_nt(h, wf_ref[...].astype(BF16))


def _norm(x2, g, wt):
    tm = _tiles()["norm_tm"]
    rows, d = x2.shape
    f_blk = (6 * WIDTH) // LANES
    return pl.pallas_call(
        _norm_kernel,
        out_shape=(jax.ShapeDtypeStruct((rows, d), BF16),
                   jax.ShapeDtypeStruct((rows, LANES), F32)),
        grid=(rows // tm,),
        in_specs=[
            pl.BlockSpec((tm, d), lambda i: (i, 0)),
            pl.BlockSpec((1, d), lambda i: (0, 0)),
            pl.BlockSpec((LANES, d), lambda i: (f_blk, 0)),
        ],
        out_specs=(pl.BlockSpec((tm, d), lambda i: (i, 0)),
                   pl.BlockSpec((tm, LANES), lambda i: (i, 0))),
        compiler_params=_params(("parallel",)),
        name="pre_norm",
    )(x2, g, wt)


def _cast_weight_tile(w_ref, w2_ref, w_scr, shift):
    if shift == 0:
        w_scr[...] = w_ref[...].astype(BF16)
    else:
        w_scr[...] = jnp.concatenate([w_ref[shift:, :], w2_ref[...]], axis=0).astype(BF16)


def _proj_kernel(h_ref, w_ref, w2_ref, o_ref, w_scr, *, n_aligned):
    j = pl.program_id(0)

    @pl.when(jnp.logical_and(pl.program_id(1) == 0, j < n_aligned))
    def _():
        _cast_weight_tile(w_ref, w2_ref, w_scr, 0)

    @pl.when(jnp.logical_and(pl.program_id(1) == 0, j >= n_aligned))
    def _():
        _cast_weight_tile(w_ref, w2_ref, w_scr, N_HEADS)

    o_ref[...] = _dot_nt(h_ref[...], w_scr[...]).astype(BF16)


def _proj_rot_kernel(h_ref, w_ref, cs_ref, sn_ref, o_ref, w_scr):
    @pl.when(pl.program_id(1) == 0)
    def _():
        _cast_weight_tile(w_ref, None, w_scr, 0)

    acc = _dot_nt(h_ref[...], w_scr[...])
    cs = cs_ref[...]
    sn = sn_ref[...]
    for hh in range(acc.shape[1] // HEAD_DIM):
        t = acc[:, hh * HEAD_DIM:(hh + 1) * HEAD_DIM]
        r = pltpu.roll(t, HEAD_DIM // 2, axis=1)
        o_ref[:, hh * HEAD_DIM:(hh + 1) * HEAD_DIM] = (t * cs + r * sn).astype(BF16)


def _proj_plain(h, wt, casts):
    t = _tiles()
    tm, tn = t["proj_tm"], t["proj_tn"]
    rows, d = h.shape
    per_group = WIDTH // tn
    n_aligned = 4 * per_group
    n_gate = (2 * d) // tn
    gate0 = (6 * WIDTH) // tn

    def w_blk(j, i):
        aligned = jnp.where(j < 3 * per_group, j, j + 2 * per_group)
        return jnp.where(j < n_aligned, aligned, gate0 + (j - n_aligned)), 0

    def w2_blk(j, i):
        nxt = jnp.maximum(j - n_aligned, 0) + 1
        return (gate0 + nxt) * (tn // N_HEADS), 0

    (p2,), cast = _call_with_casts(
        functools.partial(_proj_kernel, n_aligned=n_aligned), casts, (h, wt, wt),
        out_shape=(jax.ShapeDtypeStruct((rows, (n_aligned + n_gate) * tn), BF16),),
        grid=(n_aligned + n_gate, rows // tm),
        in_specs=[
            pl.BlockSpec((tm, d), lambda j, i: (i, 0)),
            pl.BlockSpec((tn, d), w_blk),
            pl.BlockSpec((N_HEADS, d), w2_blk),
        ],
        out_specs=(pl.BlockSpec((tm, tn), lambda j, i: (i, j)),),
        scratch_shapes=[pltpu.VMEM((tn, d), BF16)],
        compiler_params=_params(("arbitrary", "arbitrary")),
        name="proj_plain",
    )
    return p2, cast


def _proj_rot(h, wt, cs, sn, seq, casts):
    t = _tiles()
    tm, tn = t["rot_tm"], t["proj_tn"]
    rows, d = h.shape
    first = (3 * WIDTH) // tn
    pos_blocks = seq // tm
    (qk,), cast = _call_with_casts(
        _proj_rot_kernel, casts, (h, wt, cs, sn),
        out_shape=(jax.ShapeDtypeStruct((rows, 2 * WIDTH), BF16),),
        grid=((2 * WIDTH) // tn, rows // tm),
        in_specs=[
            pl.BlockSpec((tm, d), lambda j, i: (i, 0)),
            pl.BlockSpec((tn, d), lambda j, i: (first + j, 0)),
            pl.BlockSpec((tm, HEAD_DIM), lambda j, i: (i % pos_blocks, 0)),
            pl.BlockSpec((tm, HEAD_DIM), lambda j, i: (i % pos_blocks, 0)),
        ],
        out_specs=(pl.BlockSpec((tm, tn), lambda j, i: (i, j)),),
        scratch_shapes=[pltpu.VMEM((tn, d), BF16)],
        compiler_params=_params(("arbitrary", "arbitrary")),
        name="proj_rotary",
    )
    return qk, cast


def _forget_kernel(f_ref, b_ref, tri_ref, c_ref, *, chunk):
    z = f_ref[...] + b_ref[...]
    lf = jnp.minimum(z, 0.0) - jnp.log1p(jnp.exp(-jnp.abs(z)))
    lft = lf.T[:N_HEADS, :]
    tri = tri_ref[...]
    carry = jnp.zeros((N_HEADS, 1), F32)
    for n in range(lft.shape[1] // chunk):
        xc = lft[:, n * chunk:(n + 1) * chunk]
        hi = xc.astype(BF16)
        r1 = xc - hi.astype(F32)
        mid = r1.astype(BF16)
        lo = (r1 - mid.astype(F32)).astype(BF16)
        loc = (jnp.dot(hi, tri, preferred_element_type=F32)
               + jnp.dot(mid, tri, preferred_element_type=F32)
               + jnp.dot(lo, tri, preferred_element_type=F32)) + carry
        c_ref[0, :, n * chunk:(n + 1) * chunk] = loc
        carry = loc[:, chunk - 1:chunk]


def _forget_cumsum(f_pad, b_pad, batch, seq):
    chunk = _tiles()["cum_chunk"]
    tri = np.triu(np.ones((chunk, chunk), np.float32)).astype(BF16)
    return pl.pallas_call(
        functools.partial(_forget_kernel, chunk=chunk),
        out_shape=jax.ShapeDtypeStruct((batch, N_HEADS, seq), F32),
        grid=(batch,),
        in_specs=[
            pl.BlockSpec((seq, LANES), lambda b: (b, 0)),
            pl.BlockSpec((1, LANES), lambda b: (0, 0)),
            pl.BlockSpec((chunk, chunk), lambda b: (0, 0)),
        ],
        out_specs=pl.BlockSpec((1, N_HEADS, seq), lambda b: (b, 0, 0)),
        compiler_params=_params(("parallel",)),
        name="forget_cumsum",
    )(f_pad, b_pad, tri)


def _head(ref, hh, rows=slice(None)):
    return ref[0, rows, hh * HEAD_DIM:(hh + 1) * HEAD_DIM]


def _causal_softmax_pv(s, v, tq):
    length = s.shape[1]
    row = lax.broadcasted_iota(jnp.int32, (tq, tq), 0)
    col = lax.broadcasted_iota(jnp.int32, (tq, tq), 1)
    own = jnp.where(col <= row, s[:, length - tq:], -jnp.inf)
    s = own if length == tq else jnp.concatenate([s[:, :length - tq], own], axis=1)
    m = jnp.max(s, axis=-1, keepdims=True)
    p = jnp.exp2(s - m).astype(BF16)
    v1 = jnp.concatenate([v, jnp.ones(v.shape, v.dtype)], axis=1)
    acc = jnp.dot(p, v1, preferred_element_type=F32)
    return (acc[:, :HEAD_DIM] / acc[:, HEAD_DIM:]).astype(BF16)


def _tile_rows(ii, tq):
    return slice(ii * tq, (ii + 1) * tq)


def _attend(scores, v_ref, o_ref, *, n_tiles, per_step, tq, heads):
    def run(tiles):
        jobs = [(ii, hh) for ii in sorted(tiles) for hh in range(heads)]
        for j0 in range(0, len(jobs), SCORE_GROUP):
            group = jobs[j0:j0 + SCORE_GROUP]
            ss = [scores(ii, hh) for ii, hh in group]
            for (ii, hh), s in zip(group, ss):
                o_ref[0, _tile_rows(ii, tq), hh * HEAD_DIM:(hh + 1) * HEAD_DIM] = (
                    _causal_softmax_pv(s, _head(v_ref, hh, slice(0, (ii + 1) * tq)), tq))

    n_steps = n_tiles // per_step
    assert per_step % 2 == 0 and n_steps * per_step == n_tiles
    step = pl.program_id(2)
    for pp in range(n_steps):
        lows = [pp + j * n_steps for j in range(per_step // 2)]
        tiles = [t for lo in lows for t in (lo, n_tiles - 1 - lo)]
        if n_steps == 1:
            run(tiles)
        else:
            pl.when(step == pp)(functools.partial(run, tiles))


def _fox_kernel(q_ref, k_ref, v_ref, c_ref, o_ref, *, tq, heads, n_tiles, per_step):
    def scores(ii, hh):
        length = (ii + 1) * tq
        q = (_head(q_ref, hh, _tile_rows(ii, tq)).astype(F32) * (QK_SCALE * LOG2E)).astype(BF16)
        c = c_ref[0, hh:hh + 1, :]
        for gg in range(1, N_HEADS // heads):
            c = jnp.where(pl.program_id(1) == gg, c_ref[0, gg * heads + hh:gg * heads + hh + 1, :], c)
        bias = (c[:, ii * tq:ii * tq + 1] - c[:, :length]) * LOG2E
        return _dot_nt(q, _head(k_ref, hh, slice(0, length))) + bias

    _attend(scores, v_ref, o_ref, n_tiles=n_tiles, per_step=per_step, tq=tq, heads=heads)


def _fox(p3, c, batch, seq, casts):
    t = _tiles()
    tq, heads, per_step = t["attn_tq"], t["attn_heads"], t["fox_tiles_per_step"]
    hw = heads * HEAD_DIM
    groups = N_HEADS // heads
    n_tiles = seq // tq
    (o,), cast = _call_with_casts(
        functools.partial(_fox_kernel, tq=tq, heads=heads, n_tiles=n_tiles, per_step=per_step),
        casts, (p3, p3, p3, c),
        out_shape=(jax.ShapeDtypeStruct((batch, seq, WIDTH), BF16),),
        grid=(batch, groups, n_tiles // per_step),
        in_specs=[
            pl.BlockSpec((1, seq, hw), lambda b, g, i: (b, 0, g)),
            pl.BlockSpec((1, seq, hw), lambda b, g, i: (b, 0, groups + g)),
            pl.BlockSpec((1, seq, hw), lambda b, g, i: (b, 0, 2 * groups + g)),
            pl.BlockSpec((1, N_HEADS, seq), lambda b, g, i: (b, 0, 0)),
        ],
        out_specs=(pl.BlockSpec((1, seq, hw), lambda b, g, i: (b, 0, g)),),
        compiler_params=_params(("arbitrary", "arbitrary", "arbitrary")),
        name="fox_attention",
    )
    return o, cast


def _moba_block_bias(q_raw, km, ii):
    km_hi, km_lo = _split_bf16(km)
    gate = _dot_nt(km_hi, q_raw) + _dot_nt(km_lo, q_raw)
    block = lax.broadcasted_iota(jnp.int32, gate.shape, 0)
    g = jnp.where(block < ii, gate, -jnp.inf)
    rank = jnp.zeros(gate.shape, jnp.int32)
    for n in range(ii):
        gn = g[n:n + 1, :]
        ahead = jnp.logical_or(gn > g, jnp.logical_and(gn == g, n < block))
        rank = rank + ahead.astype(jnp.int32)
    bias_t = jnp.where(rank < MOBA_TOPK, 0.0, MASK_BIAS)
    pad = jnp.zeros((LANES - bias_t.shape[0], bias_t.shape[1]), F32)
    return jnp.concatenate([bias_t, pad], axis=0).T


def _moba_kernel(q_ref, k_ref, v_ref, o_ref, km_scr, *, n_blocks, heads, per_step):
    blk = MOBA_BLOCK

    @pl.when(pl.program_id(2) == 0)
    def _():
        for hh in range(heads):
            for n in range(n_blocks):
                kb = _head(k_ref, hh, _tile_rows(n, blk)).astype(F32)
                km_scr[hh, n:n + 1, :] = jnp.mean(kb, axis=0, keepdims=True)

    def scores(ii, hh):
        length = (ii + 1) * blk
        q_raw = _head(q_ref, hh, _tile_rows(ii, blk))
        q = (q_raw.astype(F32) * (QK_SCALE * LOG2E)).astype(BF16)
        s = _dot_nt(q, _head(k_ref, hh, slice(0, length)))
        if ii <= MOBA_TOPK:
            return s
        bias = _moba_block_bias(q_raw, km_scr[hh], ii)
        past = [s[:, _tile_rows(j, blk)] + bias[:, j:j + 1] for j in range(ii)]
        return jnp.concatenate(past + [s[:, _tile_rows(ii, blk)]], axis=1)

    _attend(scores, v_ref, o_ref, n_tiles=n_blocks, per_step=per_step, tq=blk, heads=heads)


def _moba(qk3, p3, batch, seq, casts):
    n_blocks = seq // MOBA_BLOCK
    assert seq % MOBA_BLOCK == 0 and n_blocks <= LANES
    t = _tiles()
    heads, per_step = t["attn_heads"], t["attn_tiles_per_step"]
    hw = heads * HEAD_DIM
    groups = N_HEADS // heads
    (o,), cast = _call_with_casts(
        functools.partial(_moba_kernel, n_blocks=n_blocks, heads=heads, per_step=per_step), casts,
        (qk3, qk3, p3),
        out_shape=(jax.ShapeDtypeStruct((batch, seq, WIDTH), BF16),),
        grid=(batch, groups, n_blocks // per_step),
        in_specs=[
            pl.BlockSpec((1, seq, hw), lambda b, g, i: (b, 0, g)),
            pl.BlockSpec((1, seq, hw), lambda b, g, i: (b, 0, groups + g)),
            pl.BlockSpec((1, seq, hw), lambda b, g, i: (b, 0, 3 * groups + g)),
        ],
        out_specs=(pl.BlockSpec((1, seq, hw), lambda b, g, i: (b, 0, g)),),
        scratch_shapes=[pltpu.VMEM((heads, n_blocks, HEAD_DIM), F32)],
        compiler_params=_params(("arbitrary", "arbitrary", "arbitrary")),
        name="moba_attention",
    )
    return o, cast


def _mix_resid_kernel(oa_ref, ob_ref, ga_ref, gb_ref, x_ref, wa_ref, wb_ref, wo_ref, g_ref,
                      x1_ref, h_ref):
    ya = jnp.dot(oa_ref[...], wa_ref[...], preferred_element_type=F32)
    yb = jnp.dot(ob_ref[...], wb_ref[...], preferred_element_type=F32)
    ga = jax.nn.sigmoid(ga_ref[...].astype(F32))
    gb = jax.nn.sigmoid(gb_ref[...].astype(F32))
    y = (ga * ya + gb * yb).astype(BF16)
    x1 = x_ref[...] + jnp.dot(y, wo_ref[...], preferred_element_type=F32)
    x1_ref[...] = x1
    h_ref[...] = _rms(x1, g_ref[...]).astype(BF16)


def _mix_resid(oa, ob, p2, x2, wa, wb, wo, g):
    tm = _tiles()["mix_tm"]
    rows, kdim = oa.shape
    d = wa.shape[1]
    ga_blk = (4 * WIDTH) // d
    resident = pl.Buffered(1)
    return pl.pallas_call(
        _mix_resid_kernel,
        out_shape=(jax.ShapeDtypeStruct((rows, d), F32),
                   jax.ShapeDtypeStruct((rows, d), BF16)),
        grid=(rows // tm,),
        in_specs=[
            pl.BlockSpec((tm, kdim), lambda i: (i, 0)),
            pl.BlockSpec((tm, kdim), lambda i: (i, 0)),
            pl.BlockSpec((tm, d), lambda i: (i, ga_blk)),
            pl.BlockSpec((tm, d), lambda i: (i, ga_blk + 1)),
            pl.BlockSpec((tm, d), lambda i: (i, 0)),
            pl.BlockSpec((kdim, d), lambda i: (0, 0), pipeline_mode=resident),
            pl.BlockSpec((kdim, d), lambda i: (0, 0), pipeline_mode=resident),
            pl.BlockSpec((d, d), lambda i: (0, 0), pipeline_mode=resident),
            pl.BlockSpec((1, d), lambda i: (0, 0)),
        ],
        out_specs=(pl.BlockSpec((tm, d), lambda i: (i, 0)),
                   pl.BlockSpec((tm, d), lambda i: (i, 0))),
        compiler_params=_params(("parallel",)),
        name="mix_residual",
    )(oa, ob, p2, p2, x2, wa, wb, wo, g)


def _ffn_kernel(h_ref, x1_ref, wg_ref, wu_ref, wd_ref, gf_ref, o_ref):
    k = pl.program_id(1)

    def delta():
        h = h_ref[...]
        gate = jnp.dot(h, wg_ref[...], preferred_element_type=F32)
        up = jnp.dot(h, wu_ref[...], preferred_element_type=F32)
        a = (gate * jax.nn.sigmoid(gate) * up).astype(BF16)
        return jnp.dot(a, wd_ref[...], preferred_element_type=F32)

    @pl.when(k == 0)
    def _():
        o_ref[...] = x1_ref[...] + delta()

    last = pl.num_programs(1) - 1

    @pl.when(jnp.logical_and(k > 0, k < last))
    def _():
        o_ref[...] += delta()

    @pl.when(k == last)
    def _():
        o_ref[...] = _rms(o_ref[...] + delta(), gf_ref[...])


def _ffn(h2, x1, wg, wu, wd, gf):
    t = _tiles()
    tm, tf = t["ffn_tm"], t["ffn_tf"]
    rows, d = h2.shape
    dff = wd.shape[0]
    assert dff // tf >= 2
    return pl.pallas_call(
        _ffn_kernel,
        out_shape=jax.ShapeDtypeStruct((rows, d), F32),
        grid=(rows // tm, dff // tf),
        in_specs=[
            pl.BlockSpec((tm, d), lambda i, k: (i, 0)),
            pl.BlockSpec((tm, d), lambda i, k: (i, 0)),
            pl.BlockSpec((d, tf), lambda i, k: (0, k)),
            pl.BlockSpec((d, tf), lambda i, k: (0, k)),
            pl.BlockSpec((tf, d), lambda i, k: (k, 0)),
            pl.BlockSpec((1, d), lambda i, k: (0, 0)),
        ],
        out_specs=pl.BlockSpec((tm, d), lambda i, k: (i, 0)),
        compiler_params=_params(("parallel", "arbitrary")),
        name="swiglu_ffn",
    )(h2, x1, wg, wu, wd, gf)


def _rope_tables(seq):
    half = HEAD_DIM // 2
    inv_freq = ROPE_THETA ** (-np.arange(half, dtype=np.float64) / half)
    ang = np.arange(seq, dtype=np.float64)[:, None] * inv_freq[None, :]
    cos, sin = np.cos(ang), np.sin(ang)
    return (np.concatenate([cos, cos], axis=-1).astype(np.float32),
            np.concatenate([-sin, sin], axis=-1).astype(np.float32))


def kernel(x, norm_mix, w_in, b_forget, w_o_fox, w_o_moba, w_out, norm_ffn, w_gate_up, w_down,
           norm_final):
    batch, seq, d = x.shape
    assert w_in.shape[0] == 1
    cs, sn = _rope_tables(seq)
    x2 = x.reshape(batch * seq, d)
    wt = w_in[0].T
    b_pad = jnp.pad(b_forget[0][None, :], ((0, 0), (0, LANES - N_HEADS)))

    dff = w_down.shape[1]
    h, f_pad = _norm(x2, norm_mix[0][None, :], wt)
    p2, (wo_a, wo_b, w_res) = _proj_plain(
        h, wt, [_Cast(w_o_fox[0], d), _Cast(w_o_moba[0], d), _Cast(w_out[0], d)])
    qk, _ = _proj_rot(h, wt, cs, sn, seq, [])
    c = _forget_cumsum(f_pad, b_pad, batch, seq)
    p3 = p2.reshape(batch, seq, p2.shape[1])
    o_a, (wg,) = _fox(p3, c, batch, seq, [_Cast(w_gate_up[0], dff, 0)])
    o_b, (wu, wd) = _moba(qk.reshape(batch, seq, qk.shape[1]), p3, batch, seq,
                          [_Cast(w_gate_up[0], dff, 1), _Cast(w_down[0], d)])
    x1, h2 = _mix_resid(o_a.reshape(batch * seq, WIDTH), o_b.reshape(batch * seq, WIDTH), p2, x2,
                        wo_a, wo_b, w_res, norm_ffn[0][None, :])
    out = _ffn(h2, x1, wg, wu, wd, norm_final[None, :])
    return out.reshape(batch, seq, d)
```

```python
import functools
import math
from typing import NamedTuple

import jax
import jax.numpy as jnp
import numpy as np
from jax import lax
from jax.experimental import pallas as pl
from jax.experimental.pallas import tpu as pltpu

F32 = jnp.float32
BF16 = jnp.bfloat16

HEAD_DIM = 128
N_HEADS = 8
WIDTH = N_HEADS * HEAD_DIM
MOBA_BLOCK = 256
MOBA_TOPK = 3
ROPE_THETA = 10000.0
RMS_EPS = 1e-6
LOG2E = math.log2(math.e)
QK_SCALE = HEAD_DIM ** -0.5
LANES = 128
BF16_SUBLANES = 16
MASK_BIAS = -1e30
SCORE_GROUP = 16
VMEM_LIMIT = 60 * 1024 * 1024


def _tiles():
    return dict(
        norm_tm=1024, proj_tm=2048, rot_tm=2048, proj_tn=1024,
        attn_tq=256, attn_heads=4, attn_tiles_per_step=4, fox_tiles_per_step=8,
        mix_tm=512,
        ffn_tm=1024, ffn_tf=512,
        cum_chunk=256,
    )


def _params(sem):
    return pltpu.CompilerParams(dimension_semantics=sem, vmem_limit_bytes=VMEM_LIMIT)


def _rms(x, g):
    ms = jnp.mean(x * x, axis=-1, keepdims=True)
    return x * lax.rsqrt(ms + RMS_EPS) * g


def _dot_nt(a, b):
    return lax.dot_general(a, b, (((1,), (1,)), ((), ())), preferred_element_type=F32)


class _Cast(NamedTuple):
    src: jax.Array
    cols: int
    col_block: int = 0


def _call_with_casts(kernel_fn, casts, args, *, grid, in_specs, out_specs, out_shape, **kw):
    steps = math.prod(grid)
    strides = [math.prod(grid[a + 1:]) for a in range(len(grid))]

    def step_of(*g):
        return sum(gi * st for gi, st in zip(g, strides))

    n_in, n_out, n_side = len(in_specs), len(out_shape), len(casts)
    side_in, side_out, side_shape = [], [], []
    for c in casts:
        rows = c.src.shape[0] // steps
        assert rows * steps == c.src.shape[0] and rows % BF16_SUBLANES == 0
        side_in.append(pl.BlockSpec((rows, c.cols), lambda *g, c=c: (step_of(*g), c.col_block)))
        side_out.append(pl.BlockSpec((rows, c.cols), lambda *g: (step_of(*g), 0)))
        side_shape.append(jax.ShapeDtypeStruct((c.src.shape[0], c.cols), BF16))

    def body(*refs):
        src = refs[n_in:n_in + n_side]
        dst = refs[n_in + n_side + n_out:n_in + 2 * n_side + n_out]
        for s_ref, d_ref in zip(src, dst):
            d_ref[...] = s_ref[...].astype(BF16)
        kernel_fn(*refs[:n_in], *refs[n_in + n_side:n_in + n_side + n_out],
                  *refs[n_in + 2 * n_side + n_out:])

    res = pl.pallas_call(
        body, grid=grid, in_specs=[*in_specs, *side_in], out_specs=(*out_specs, *side_out),
        out_shape=(*out_shape, *side_shape), **kw)(*args, *[c.src for c in casts])
    return res[:n_out], res[n_out:]


def _split_bf16(v):
    hi = v.astype(BF16)
    return hi, (v - hi.astype(F32)).astype(BF16)


def _norm_kernel(x_ref, g_ref, wf_ref, h_ref, f_ref):
    h = _rms(x_ref[...], g_ref[...]).astype(BF16)
    h_ref[...] = h
    f_ref[...] = _dot_nt(h, wf_ref[...].astype(BF16))


def _norm(x2, g, wt):
    tm = _tiles()["norm_tm"]
    rows, d = x2.shape
    f_blk = (6 * WIDTH) // LANES
    return pl.pallas_call(
        _norm_kernel,
        out_shape=(jax.ShapeDtypeStruct((rows, d), BF16),
                   jax.ShapeDtypeStruct((rows, LANES), F32)),
        grid=(rows // tm,),
        in_specs=[
            pl.BlockSpec((tm, d), lambda i: (i, 0)),
            pl.BlockSpec((1, d), lambda i: (0, 0)),
            pl.BlockSpec((LANES, d), lambda i: (f_blk, 0)),
        ],
        out_specs=(pl.BlockSpec((tm, d), lambda i: (i, 0)),
                   pl.BlockSpec((tm, LANES), lambda i: (i, 0))),
        compiler_params=_params(("parallel",)),
        name="pre_norm",
    )(x2, g, wt)


def _cast_weight_tile(w_ref, w2_ref, w_scr, shift):
    if shift == 0:
        w_scr[...] = w_ref[...].astype(BF16)
    else:
        w_scr[...] = jnp.concatenate([w_ref[shift:, :], w2_ref[...]], axis=0).astype(BF16)


def _proj_kernel(h_ref, w_ref, w2_ref, o_ref, w_scr, *, n_aligned):
    j = pl.program_id(0)

    @pl.when(jnp.logical_and(pl.program_id(1) == 0, j < n_aligned))
    def _():
        _cast_weight_tile(w_ref, w2_ref, w_scr, 0)

    @pl.when(jnp.logical_and(pl.program_id(1) == 0, j >= n_aligned))
    def _():
        _cast_weight_tile(w_ref, w2_ref, w_scr, N_HEADS)

    o_ref[...] = _dot_nt(h_ref[...], w_scr[...]).astype(BF16)


def _proj_rot_kernel(h_ref, w_ref, cs_ref, sn_ref, o_ref, w_scr):
    @pl.when(pl.program_id(1) == 0)
    def _():
        _cast_weight_tile(w_ref, None, w_scr, 0)

    acc = _dot_nt(h_ref[...], w_scr[...])
    cs = cs_ref[...]
    sn = sn_ref[...]
    for hh in range(acc.shape[1] // HEAD_DIM):
        t = acc[:, hh * HEAD_DIM:(hh + 1) * HEAD_DIM]
        r = pltpu.roll(t, HEAD_DIM // 2, axis=1)
        o_ref[:, hh * HEAD_DIM:(hh + 1) * HEAD_DIM] = (t * cs + r * sn).astype(BF16)


def _proj_plain(h, wt, casts):
    t = _tiles()
    tm, tn = t["proj_tm"], t["proj_tn"]
    rows, d = h.shape
    per_group = WIDTH // tn
    n_aligned = 4 * per_group
    n_gate = (2 * d) // tn
    gate0 = (6 * WIDTH) // tn

    def w_blk(j, i):
        aligned = jnp.where(j < 3 * per_group, j, j + 2 * per_group)
        return jnp.where(j < n_aligned, aligned, gate0 + (j - n_aligned)), 0

    def w2_blk(j, i):
        nxt = jnp.maximum(j - n_aligned, 0) + 1
        return (gate0 + nxt) * (tn // N_HEADS), 0

    (p2,), cast = _call_with_casts(
        functools.partial(_proj_kernel, n_aligned=n_aligned), casts, (h, wt, wt),
        out_shape=(jax.ShapeDtypeStruct((rows, (n_aligned + n_gate) * tn), BF16),),
        grid=(n_aligned + n_gate, rows // tm),
        in_specs=[
            pl.BlockSpec((tm, d), lambda j, i: (i, 0)),
            pl.BlockSpec((tn, d), w_blk),
            pl.BlockSpec((N_HEADS, d), w2_blk),
        ],
        out_specs=(pl.BlockSpec((tm, tn), lambda j, i: (i, j)),),
        scratch_shapes=[pltpu.VMEM((tn, d), BF16)],
        compiler_params=_params(("arbitrary", "arbitrary")),
        name="proj_plain",
    )
    return p2, cast


def _proj_rot(h, wt, cs, sn, seq, casts):
    t = _tiles()
    tm, tn = t["rot_tm"], t["proj_tn"]
    rows, d = h.shape
    first = (3 * WIDTH) // tn
    pos_blocks = seq // tm
    (qk,), cast = _call_with_casts(
        _proj_rot_kernel, casts, (h, wt, cs, sn),
        out_shape=(jax.ShapeDtypeStruct((rows, 2 * WIDTH), BF16),),
        grid=((2 * WIDTH) // tn, rows // tm),
        in_specs=[
            pl.BlockSpec((tm, d), lambda j, i: (i, 0)),
            pl.BlockSpec((tn, d), lambda j, i: (first + j, 0)),
            pl.BlockSpec((tm, HEAD_DIM), lambda j, i: (i % pos_blocks, 0)),
            pl.BlockSpec((tm, HEAD_DIM), lambda j, i: (i % pos_blocks, 0)),
        ],
        out_specs=(pl.BlockSpec((tm, tn), lambda j, i: (i, j)),),
        scratch_shapes=[pltpu.VMEM((tn, d), BF16)],
        compiler_params=_params(("arbitrary", "arbitrary")),
        name="proj_rotary",
    )
    return qk, cast


def _forget_kernel(f_ref, b_ref, tri_ref, c_ref, *, chunk):
    z = f_ref[...] + b_ref[...]
    lf = jnp.minimum(z, 0.0) - jnp.log1p(jnp.exp(-jnp.abs(z)))
    lft = lf.T[:N_HEADS, :]
    tri = tri_ref[...]
    carry = jnp.zeros((N_HEADS, 1), F32)
    for n in range(lft.shape[1] // chunk):
        xc = lft[:, n * chunk:(n + 1) * chunk]
        hi = xc.astype(BF16)
        r1 = xc - hi.astype(F32)
        mid = r1.astype(BF16)
        lo = (r1 - mid.astype(F32)).astype(BF16)
        loc = (jnp.dot(hi, tri, preferred_element_type=F32)
               + jnp.dot(mid, tri, preferred_element_type=F32)
               + jnp.dot(lo, tri, preferred_element_type=F32)) + carry
        c_ref[0, :, n * chunk:(n + 1) * chunk] = loc
        carry = loc[:, chunk - 1:chunk]


def _forget_cumsum(f_pad, b_pad, batch, seq):
    chunk = _tiles()["cum_chunk"]
    tri = np.triu(np.ones((chunk, chunk), np.float32)).astype(BF16)
    return pl.pallas_call(
        functools.partial(_forget_kernel, chunk=chunk),
        out_shape=jax.ShapeDtypeStruct((batch, N_HEADS, seq), F32),
        grid=(batch,),
        in_specs=[
            pl.BlockSpec((seq, LANES), lambda b: (b, 0)),
            pl.BlockSpec((1, LANES), lambda b: (0, 0)),
            pl.BlockSpec((chunk, chunk), lambda b: (0, 0)),
        ],
        out_specs=pl.BlockSpec((1, N_HEADS, seq), lambda b: (b, 0, 0)),
        compiler_params=_params(("parallel",)),
        name="forget_cumsum",
    )(f_pad, b_pad, tri)


def _head(ref, hh, rows=slice(None)):
    return ref[0, rows, hh * HEAD_DIM:(hh + 1) * HEAD_DIM]


def _causal_softmax_pv(s, v, tq):
    length = s.shape[1]
    row = lax.broadcasted_iota(jnp.int32, (tq, tq), 0)
    col = lax.broadcasted_iota(jnp.int32, (tq, tq), 1)
    own = jnp.where(col <= row, s[:, length - tq:], -jnp.inf)
    s = own if length == tq else jnp.concatenate([s[:, :length - tq], own], axis=1)
    m = jnp.max(s, axis=-1, keepdims=True)
    p = jnp.exp2(s - m).astype(BF16)
    v1 = jnp.concatenate([v, jnp.ones(v.shape, v.dtype)], axis=1)
    acc = jnp.dot(p, v1, preferred_element_type=F32)
    return (acc[:, :HEAD_DIM] / acc[:, HEAD_DIM:]).astype(BF16)


def _tile_rows(ii, tq):
    return slice(ii * tq, (ii + 1) * tq)


def _attend(scores, v_ref, o_ref, *, n_tiles, per_step, tq, heads):
    def run(tiles):
        jobs = [(ii, hh) for ii in sorted(tiles) for hh in range(heads)]
        for j0 in range(0, len(jobs), SCORE_GROUP):
            group = jobs[j0:j0 + SCORE_GROUP]
            ss = [scores(ii, hh) for ii, hh in group]
            for (ii, hh), s in zip(group, ss):
                o_ref[0, _tile_rows(ii, tq), hh * HEAD_DIM:(hh + 1) * HEAD_DIM] = (
                    _causal_softmax_pv(s, _head(v_ref, hh, slice(0, (ii + 1) * tq)), tq))

    n_steps = n_tiles // per_step
    assert per_step % 2 == 0 and n_steps * per_step == n_tiles
    step = pl.program_id(2)
    for pp in range(n_steps):
        lows = [pp + j * n_steps for j in range(per_step // 2)]
        tiles = [t for lo in lows for t in (lo, n_tiles - 1 - lo)]
        if n_steps == 1:
            run(tiles)
        else:
            pl.when(step == pp)(functools.partial(run, tiles))


def _fox_kernel(q_ref, k_ref, v_ref, c_ref, o_ref, *, tq, heads, n_tiles, per_step):
    def scores(ii, hh):
        length = (ii + 1) * tq
        q = (_head(q_ref, hh, _tile_rows(ii, tq)).astype(F32) * (QK_SCALE * LOG2E)).astype(BF16)
        c = c_ref[0, hh:hh + 1, :]
        for gg in range(1, N_HEADS // heads):
            c = jnp.where(pl.program_id(1) == gg, c_ref[0, gg * heads + hh:gg * heads + hh + 1, :], c)
        bias = (c[:, ii * tq:ii * tq + 1] - c[:, :length]) * LOG2E
        return _dot_nt(q, _head(k_ref, hh, slice(0, length))) + bias

    _attend(scores, v_ref, o_ref, n_tiles=n_tiles, per_step=per_step, tq=tq, heads=heads)


def _fox(p3, c, batch, seq, casts):
    t = _tiles()
    tq, heads, per_step = t["attn_tq"], t["attn_heads"], t["fox_tiles_per_step"]
    hw = heads * HEAD_DIM
    groups = N_HEADS // heads
    n_tiles = seq // tq
    (o,), cast = _call_with_casts(
        functools.partial(_fox_kernel, tq=tq, heads=heads, n_tiles=n_tiles, per_step=per_step),
        casts, (p3, p3, p3, c),
        out_shape=(jax.ShapeDtypeStruct((batch, seq, WIDTH), BF16),),
        grid=(batch, groups, n_tiles // per_step),
        in_specs=[
            pl.BlockSpec((1, seq, hw), lambda b, g, i: (b, 0, g)),
            pl.BlockSpec((1, seq, hw), lambda b, g, i: (b, 0, groups + g)),
            pl.BlockSpec((1, seq, hw), lambda b, g, i: (b, 0, 2 * groups + g)),
            pl.BlockSpec((1, N_HEADS, seq), lambda b, g, i: (b, 0, 0)),
        ],
        out_specs=(pl.BlockSpec((1, seq, hw), lambda b, g, i: (b, 0, g)),),
        compiler_params=_params(("arbitrary", "arbitrary", "arbitrary")),
        name="fox_attention",
    )
    return o, cast


def _moba_block_bias(q_raw, km, ii):
    km_hi, km_lo = _split_bf16(km)
    gate = _dot_nt(km_hi, q_raw) + _dot_nt(km_lo, q_raw)
    block = lax.broadcasted_iota(jnp.int32, gate.shape, 0)
    g = jnp.where(block < ii, gate, -jnp.inf)
    rank = jnp.zeros(gate.shape, jnp.int32)
    for n in range(ii):
        gn = g[n:n + 1, :]
        ahead = jnp.logical_or(gn > g, jnp.logical_and(gn == g, n < block))
        rank = rank + ahead.astype(jnp.int32)
    bias_t = jnp.where(rank < MOBA_TOPK, 0.0, MASK_BIAS)
    pad = jnp.zeros((LANES - bias_t.shape[0], bias_t.shape[1]), F32)
    return jnp.concatenate([bias_t, pad], axis=0).T


def _moba_kernel(q_ref, k_ref, v_ref, o_ref, km_scr, *, n_blocks, heads, per_step):
    blk = MOBA_BLOCK

    @pl.when(pl.program_id(2) == 0)
    def _():
        for hh in range(heads):
            for n in range(n_blocks):
                kb = _head(k_ref, hh, _tile_rows(n, blk)).astype(F32)
                km_scr[hh, n:n + 1, :] = jnp.mean(kb, axis=0, keepdims=True)

    def scores(ii, hh):
        length = (ii + 1) * blk
        q_raw = _head(q_ref, hh, _tile_rows(ii, blk))
        q = (q_raw.astype(F32) * (QK_SCALE * LOG2E)).astype(BF16)
        s = _dot_nt(q, _head(k_ref, hh, slice(0, length)))
        if ii <= MOBA_TOPK:
            return s
        bias = _moba_block_bias(q_raw, km_scr[hh], ii)
        past = [s[:, _tile_rows(j, blk)] + bias[:, j:j + 1] for j in range(ii)]
        return jnp.concatenate(past + [s[:, _tile_rows(ii, blk)]], axis=1)

    _attend(scores, v_ref, o_ref, n_tiles=n_blocks, per_step=per_step, tq=blk, heads=heads)


def _moba(qk3, p3, batch, seq, casts):
    n_blocks = seq // MOBA_BLOCK
    assert seq % MOBA_BLOCK == 0 and n_blocks <= LANES
    t = _tiles()
    heads, per_step = t["attn_heads"], t["attn_tiles_per_step"]
    hw = heads * HEAD_DIM
    groups = N_HEADS // heads
    (o,), cast = _call_with_casts(
        functools.partial(_moba_kernel, n_blocks=n_blocks, heads=heads, per_step=per_step), casts,
        (qk3, qk3, p3),
        out_shape=(jax.ShapeDtypeStruct((batch, seq, WIDTH), BF16),),
        grid=(batch, groups, n_blocks // per_step),
        in_specs=[
            pl.BlockSpec((1, seq, hw), lambda b, g, i: (b, 0, g)),
            pl.BlockSpec((1, seq, hw), lambda b, g, i: (b, 0, groups + g)),
            pl.BlockSpec((1, seq, hw), lambda b, g, i: (b, 0, 3 * groups + g)),
        ],
        out_specs=(pl.BlockSpec((1, seq, hw), lambda b, g, i: (b, 0, g)),),
        scratch_shapes=[pltpu.VMEM((heads, n_blocks, HEAD_DIM), F32)],
        compiler_params=_params(("arbitrary", "arbitrary", "arbitrary")),
        name="moba_attention",
    )
    return o, cast


def _mix_resid_kernel(oa_ref, ob_ref, ga_ref, gb_ref, x_ref, wa_ref, wb_ref, wo_ref, g_ref,
                      x1_ref, h_ref):
    ya = jnp.dot(oa_ref[...], wa_ref[...], preferred_element_type=F32)
    yb = jnp.dot(ob_ref[...], wb_ref[...], preferred_element_type=F32)
    ga = jax.nn.sigmoid(ga_ref[...].astype(F32))
    gb = jax.nn.sigmoid(gb_ref[...].astype(F32))
    y = (ga * ya + gb * yb).astype(BF16)
    x1 = x_ref[...] + jnp.dot(y, wo_ref[...], preferred_element_type=F32)
    x1_ref[...] = x1
    h_ref[...] = _rms(x1, g_ref[...]).astype(BF16)


def _mix_resid(oa, ob, p2, x2, wa, wb, wo, g):
    tm = _tiles()["mix_tm"]
    rows, kdim = oa.shape
    d = wa.shape[1]
    ga_blk = (4 * WIDTH) // d
    resident = pl.Buffered(1)
    return pl.pallas_call(
        _mix_resid_kernel,
        out_shape=(jax.ShapeDtypeStruct((rows, d), F32),
                   jax.ShapeDtypeStruct((rows, d), BF16)),
        grid=(rows // tm,),
        in_specs=[
            pl.BlockSpec((tm, kdim), lambda i: (i, 0)),
            pl.BlockSpec((tm, kdim), lambda i: (i, 0)),
            pl.BlockSpec((tm, d), lambda i: (i, ga_blk)),
            pl.BlockSpec((tm, d), lambda i: (i, ga_blk + 1)),
            pl.BlockSpec((tm, d), lambda i: (i, 0)),
            pl.BlockSpec((kdim, d), lambda i: (0, 0), pipeline_mode=resident),
            pl.BlockSpec((kdim, d), lambda i: (0, 0), pipeline_mode=resident),
            pl.BlockSpec((d, d), lambda i: (0, 0), pipeline_mode=resident),
            pl.BlockSpec((1, d), lambda i: (0, 0)),
        ],
        out_specs=(pl.BlockSpec((tm, d), lambda i: (i, 0)),
                   pl.BlockSpec((tm, d), lambda i: (i, 0))),
        compiler_params=_params(("parallel",)),
        name="mix_residual",
    )(oa, ob, p2, p2, x2, wa, wb, wo, g)


def _ffn_kernel(h_ref, x1_ref, wg_ref, wu_ref, wd_ref, gf_ref, o_ref):
    k = pl.program_id(1)

    def delta():
        h = h_ref[...]
        gate = jnp.dot(h, wg_ref[...], preferred_element_type=F32)
        up = jnp.dot(h, wu_ref[...], preferred_element_type=F32)
        a = (gate * jax.nn.sigmoid(gate) * up).astype(BF16)
        return jnp.dot(a, wd_ref[...], preferred_element_type=F32)

    @pl.when(k == 0)
    def _():
        o_ref[...] = x1_ref[...] + delta()

    last = pl.num_programs(1) - 1

    @pl.when(jnp.logical_and(k > 0, k < last))
    def _():
        o_ref[...] += delta()

    @pl.when(k == last)
    def _():
        o_ref[...] = _rms(o_ref[...] + delta(), gf_ref[...])


def _ffn(h2, x1, wg, wu, wd, gf):
    t = _tiles()
    tm, tf = t["ffn_tm"], t["ffn_tf"]
    rows, d = h2.shape
    dff = wd.shape[0]
    assert dff // tf >= 2
    return pl.pallas_call(
        _ffn_kernel,
        out_shape=jax.ShapeDtypeStruct((rows, d), F32),
        grid=(rows // tm, dff // tf),
        in_specs=[
            pl.BlockSpec((tm, d), lambda i, k: (i, 0)),
            pl.BlockSpec((tm, d), lambda i, k: (i, 0)),
            pl.BlockSpec((d, tf), lambda i, k: (0, k)),
            pl.BlockSpec((d, tf), lambda i, k: (0, k)),
            pl.BlockSpec((tf, d), lambda i, k: (k, 0)),
            pl.BlockSpec((1, d), lambda i, k: (0, 0)),
        ],
        out_specs=pl.BlockSpec((tm, d), lambda i, k: (i, 0)),
        compiler_params=_params(("parallel", "arbitrary")),
        name="swiglu_ffn",
    )(h2, x1, wg, wu, wd, gf)


def _rope_tables(seq):
    half = HEAD_DIM // 2
    inv_freq = ROPE_THETA ** (-np.arange(half, dtype=np.float64) / half)
    ang = np.arange(seq, dtype=np.float64)[:, None] * inv_freq[None, :]
    cos, sin = np.cos(ang), np.sin(ang)
    return (np.concatenate([cos, cos], axis=-1).astype(np.float32),
            np.concatenate([-sin, sin], axis=-1).astype(np.float32))


def kernel(x, norm_mix, w_in, b_forget, w_o_fox, w_o_moba, w_out, norm_ffn, w_gate_up, w_down,
           norm_final):
    batch, seq, d = x.shape
    assert w_in.shape[0] == 1
    cs, sn = _rope_tables(seq)
    x2 = x.reshape(batch * seq, d)
    wt = w_in[0].T
    b_pad = jnp.pad(b_forget[0][None, :], ((0, 0), (0, LANES - N_HEADS)))

    dff = w_down.shape[1]
    h, f_pad = _norm(x2, norm_mix[0][None, :], wt)
    p2, (wo_a, wo_b, w_res) = _proj_plain(
        h, wt, [_Cast(w_o_fox[0], d), _Cast(w_o_moba[0], d), _Cast(w_out[0], d)])
    qk, _ = _proj_rot(h, wt, cs, sn, seq, [])
    c = _forget_cumsum(f_pad, b_pad, batch, seq)
    p3 = p2.reshape(batch, seq, p2.shape[1])
    o_a, _ = _fox(p3, c, batch, seq, [])
    o_b, (wg, wu, wd) = _moba(
        qk.reshape(batch, seq, qk.shape[1]), p3, batch, seq,
        [_Cast(w_gate_up[0], dff, 0), _Cast(w_gate_up[0], dff, 1), _Cast(w_down[0], d)])
    x1, h2 = _mix_resid(o_a.reshape(batch * seq, WIDTH), o_b.reshape(batch * seq, WIDTH), p2, x2,
                        wo_a, wo_b, w_res, norm_ffn[0][None, :])
    out = _ffn(h2, x1, wg, wu, wd, norm_final[None, :])
    return out.reshape(batch, seq, d)
```

```python
import functools
import math
from typing import NamedTuple

import jax
import jax.numpy as jnp
import numpy as np
from jax import lax
from jax.experimental import pallas as pl
from jax.experimental.pallas import tpu as pltpu

F32 = jnp.float32
BF16 = jnp.bfloat16

HEAD_DIM = 128
N_HEADS = 8
WIDTH = N_HEADS * HEAD_DIM
MOBA_BLOCK = 256
MOBA_TOPK = 3
ROPE_THETA = 10000.0
RMS_EPS = 1e-6
LOG2E = math.log2(math.e)
QK_SCALE = HEAD_DIM ** -0.5
LANES = 128
BF16_SUBLANES = 16
MASK_BIAS = -1e30
SCORE_GROUP = 16
VMEM_LIMIT = 60 * 1024 * 1024


def _tiles():
    return dict(
        norm_tm=1024, proj_tm=2048, rot_tm=2048, proj_tn=1024,
        attn_tq=256, attn_heads=4, attn_tiles_per_step=4, fox_tiles_per_step=8,
        mix_tm=512,
        ffn_tm=1024, ffn_tf=512,
        cum_chunk=256,
    )


def _params(sem):
    return pltpu.CompilerParams(dimension_semantics=sem, vmem_limit_bytes=VMEM_LIMIT)


def _rms(x, g):
    ms = jnp.mean(x * x, axis=-1, keepdims=True)
    return x * lax.rsqrt(ms + RMS_EPS) * g


def _dot_nt(a, b):
    return lax.dot_general(a, b, (((1,), (1,)), ((), ())), preferred_element_type=F32)


class _Cast(NamedTuple):
    src: jax.Array
    cols: int
    col_block: int = 0


def _call_with_casts(kernel_fn, casts, args, *, grid, in_specs, out_specs, out_shape, **kw):
    steps = math.prod(grid)
    strides = [math.prod(grid[a + 1:]) for a in range(len(grid))]

    def step_of(*g):
        return sum(gi * st for gi, st in zip(g, strides))

    n_in, n_out, n_side = len(in_specs), len(out_shape), len(casts)
    side_in, side_out, side_shape = [], [], []
    for c in casts:
        rows = c.src.shape[0] // steps
        assert rows * steps == c.src.shape[0] and rows % BF16_SUBLANES == 0
        side_in.append(pl.BlockSpec((rows, c.cols), lambda *g, c=c: (step_of(*g), c.col_block)))
        side_out.append(pl.BlockSpec((rows, c.cols), lambda *g: (step_of(*g), 0)))
        side_shape.append(jax.ShapeDtypeStruct((c.src.shape[0], c.cols), BF16))

    def body(*refs):
        src = refs[n_in:n_in + n_side]
        dst = refs[n_in + n_side + n_out:n_in + 2 * n_side + n_out]
        for s_ref, d_ref in zip(src, dst):
            d_ref[...] = s_ref[...].astype(BF16)
        kernel_fn(*refs[:n_in], *refs[n_in + n_side:n_in + n_side + n_out],
                  *refs[n_in + 2 * n_side + n_out:])

    res = pl.pallas_call(
        body, grid=grid, in_specs=[*in_specs, *side_in], out_specs=(*out_specs, *side_out),
        out_shape=(*out_shape, *side_shape), **kw)(*args, *[c.src for c in casts])
    return res[:n_out], res[n_out:]


def _split_bf16(v):
    hi = v.astype(BF16)
    return hi, (v - hi.astype(F32)).astype(BF16)


def _norm_kernel(x_ref, g_ref, wf_ref, b_ref, tri_ref, h_ref, c_ref, carry_scr, *, chunk,
                 tiles_per_seq):
    h = _rms(x_ref[...], g_ref[...]).astype(BF16)
    h_ref[...] = h
    z = _dot_nt(h, wf_ref[...].astype(BF16)) + b_ref[...]
    lf = jnp.minimum(z, 0.0) - jnp.log1p(jnp.exp(-jnp.abs(z)))
    lft = lf.T[:N_HEADS, :]
    tri = tri_ref[...]

    @pl.when(pl.program_id(0) % tiles_per_seq == 0)
    def _():
        carry_scr[...] = jnp.zeros_like(carry_scr)

    carry = carry_scr[:, 0:1]
    for n in range(lft.shape[1] // chunk):
        xc = lft[:, n * chunk:(n + 1) * chunk]
        hi = xc.astype(BF16)
        r1 = xc - hi.astype(F32)
        mid = r1.astype(BF16)
        lo = (r1 - mid.astype(F32)).astype(BF16)
        loc = (jnp.dot(hi, tri, preferred_element_type=F32)
               + jnp.dot(mid, tri, preferred_element_type=F32)
               + jnp.dot(lo, tri, preferred_element_type=F32)) + carry
        c_ref[0, :, n * chunk:(n + 1) * chunk] = loc
        carry = loc[:, chunk - 1:chunk]
    carry_scr[...] = jnp.broadcast_to(carry, carry_scr.shape)


def _norm(x2, g, wt, b_pad, batch, seq):
    t = _tiles()
    tm, chunk = t["norm_tm"], t["cum_chunk"]
    rows, d = x2.shape
    f_blk = (6 * WIDTH) // LANES
    tiles_per_seq = seq // tm
    tri = np.triu(np.ones((chunk, chunk), np.float32)).astype(BF16)
    return pl.pallas_call(
        functools.partial(_norm_kernel, chunk=chunk, tiles_per_seq=tiles_per_seq),
        out_shape=(jax.ShapeDtypeStruct((rows, d), BF16),
                   jax.ShapeDtypeStruct((batch, N_HEADS, seq), F32)),
        grid=(rows // tm,),
        in_specs=[
            pl.BlockSpec((tm, d), lambda i: (i, 0)),
            pl.BlockSpec((1, d), lambda i: (0, 0)),
            pl.BlockSpec((LANES, d), lambda i: (f_blk, 0)),
            pl.BlockSpec((1, LANES), lambda i: (0, 0)),
            pl.BlockSpec((chunk, chunk), lambda i: (0, 0)),
        ],
        out_specs=(pl.BlockSpec((tm, d), lambda i: (i, 0)),
                   pl.BlockSpec((1, N_HEADS, tm), lambda i: (i // tiles_per_seq, 0, i % tiles_per_seq))),
        scratch_shapes=[pltpu.VMEM((N_HEADS, LANES), F32)],
        compiler_params=_params(("arbitrary",)),
        name="pre_norm",
    )(x2, g, wt, b_pad, tri)


def _cast_weight_tile(w_ref, w2_ref, w_scr, shift):
    if shift == 0:
        w_scr[...] = w_ref[...].astype(BF16)
    else:
        w_scr[...] = jnp.concatenate([w_ref[shift:, :], w2_ref[...]], axis=0).astype(BF16)


def _proj_kernel(h_ref, w_ref, w2_ref, o_ref, w_scr, *, n_aligned):
    j = pl.program_id(0)

    @pl.when(jnp.logical_and(pl.program_id(1) == 0, j < n_aligned))
    def _():
        _cast_weight_tile(w_ref, w2_ref, w_scr, 0)

    @pl.when(jnp.logical_and(pl.program_id(1) == 0, j >= n_aligned))
    def _():
        _cast_weight_tile(w_ref, w2_ref, w_scr, N_HEADS)

    o_ref[...] = _dot_nt(h_ref[...], w_scr[...]).astype(BF16)


def _proj_rot_kernel(h_ref, w_ref, cs_ref, sn_ref, o_ref, w_scr):
    @pl.when(pl.program_id(1) == 0)
    def _():
        _cast_weight_tile(w_ref, None, w_scr, 0)

    acc = _dot_nt(h_ref[...], w_scr[...])
    cs = cs_ref[...]
    sn = sn_ref[...]
    for hh in range(acc.shape[1] // HEAD_DIM):
        t = acc[:, hh * HEAD_DIM:(hh + 1) * HEAD_DIM]
        r = pltpu.roll(t, HEAD_DIM // 2, axis=1)
        o_ref[:, hh * HEAD_DIM:(hh + 1) * HEAD_DIM] = (t * cs + r * sn).astype(BF16)


def _proj_plain(h, wt, casts):
    t = _tiles()
    tm, tn = t["proj_tm"], t["proj_tn"]
    rows, d = h.shape
    per_group = WIDTH // tn
    n_aligned = 4 * per_group
    n_gate = (2 * d) // tn
    gate0 = (6 * WIDTH) // tn

    def w_blk(j, i):
        aligned = jnp.where(j < 3 * per_group, j, j + 2 * per_group)
        return jnp.where(j < n_aligned, aligned, gate0 + (j - n_aligned)), 0

    def w2_blk(j, i):
        nxt = jnp.maximum(j - n_aligned, 0) + 1
        return (gate0 + nxt) * (tn // N_HEADS), 0

    (p2,), cast = _call_with_casts(
        functools.partial(_proj_kernel, n_aligned=n_aligned), casts, (h, wt, wt),
        out_shape=(jax.ShapeDtypeStruct((rows, (n_aligned + n_gate) * tn), BF16),),
        grid=(n_aligned + n_gate, rows // tm),
        in_specs=[
            pl.BlockSpec((tm, d), lambda j, i: (i, 0)),
            pl.BlockSpec((tn, d), w_blk),
            pl.BlockSpec((N_HEADS, d), w2_blk),
        ],
        out_specs=(pl.BlockSpec((tm, tn), lambda j, i: (i, j)),),
        scratch_shapes=[pltpu.VMEM((tn, d), BF16)],
        compiler_params=_params(("arbitrary", "arbitrary")),
        name="proj_plain",
    )
    return p2, cast


def _proj_rot(h, wt, cs, sn, seq, casts):
    t = _tiles()
    tm, tn = t["rot_tm"], t["proj_tn"]
    rows, d = h.shape
    first = (3 * WIDTH) // tn
    pos_blocks = seq // tm
    (qk,), cast = _call_with_casts(
        _proj_rot_kernel, casts, (h, wt, cs, sn),
        out_shape=(jax.ShapeDtypeStruct((rows, 2 * WIDTH), BF16),),
        grid=((2 * WIDTH) // tn, rows // tm),
        in_specs=[
            pl.BlockSpec((tm, d), lambda j, i: (i, 0)),
            pl.BlockSpec((tn, d), lambda j, i: (first + j, 0)),
            pl.BlockSpec((tm, HEAD_DIM), lambda j, i: (i % pos_blocks, 0)),
            pl.BlockSpec((tm, HEAD_DIM), lambda j, i: (i % pos_blocks, 0)),
        ],
        out_specs=(pl.BlockSpec((tm, tn), lambda j, i: (i, j)),),
        scratch_shapes=[pltpu.VMEM((tn, d), BF16)],
        compiler_params=_params(("arbitrary", "arbitrary")),
        name="proj_rotary",
    )
    return qk, cast


def _head(ref, hh, rows=slice(None)):
    return ref[0, rows, hh * HEAD_DIM:(hh + 1) * HEAD_DIM]


def _causal_softmax_pv(s, v, tq):
    length = s.shape[1]
    row = lax.broadcasted_iota(jnp.int32, (tq, tq), 0)
    col = lax.broadcasted_iota(jnp.int32, (tq, tq), 1)
    own = jnp.where(col <= row, s[:, length - tq:], -jnp.inf)
    s = own if length == tq else jnp.concatenate([s[:, :length - tq], own], axis=1)
    m = jnp.max(s, axis=-1, keepdims=True)
    p = jnp.exp2(s - m).astype(BF16)
    v1 = jnp.concatenate([v, jnp.ones(v.shape, v.dtype)], axis=1)
    acc = jnp.dot(p, v1, preferred_element_type=F32)
    return (acc[:, :HEAD_DIM] / acc[:, HEAD_DIM:]).astype(BF16)


def _tile_rows(ii, tq):
    return slice(ii * tq, (ii + 1) * tq)


def _attend(scores, v_ref, o_ref, *, n_tiles, per_step, tq, heads):
    def run(tiles):
        jobs = [(ii, hh) for ii in sorted(tiles) for hh in range(heads)]
        for j0 in range(0, len(jobs), SCORE_GROUP):
            group = jobs[j0:j0 + SCORE_GROUP]
            ss = [scores(ii, hh) for ii, hh in group]
            for (ii, hh), s in zip(group, ss):
                o_ref[0, _tile_rows(ii, tq), hh * HEAD_DIM:(hh + 1) * HEAD_DIM] = (
                    _causal_softmax_pv(s, _head(v_ref, hh, slice(0, (ii + 1) * tq)), tq))

    n_steps = n_tiles // per_step
    assert per_step % 2 == 0 and n_steps * per_step == n_tiles
    step = pl.program_id(2)
    for pp in range(n_steps):
        lows = [pp + j * n_steps for j in range(per_step // 2)]
        tiles = [t for lo in lows for t in (lo, n_tiles - 1 - lo)]
        if n_steps == 1:
            run(tiles)
        else:
            pl.when(step == pp)(functools.partial(run, tiles))


def _fox_kernel(q_ref, k_ref, v_ref, c_ref, o_ref, *, tq, heads, n_tiles, per_step):
    def scores(ii, hh):
        length = (ii + 1) * tq
        q = (_head(q_ref, hh, _tile_rows(ii, tq)).astype(F32) * (QK_SCALE * LOG2E)).astype(BF16)
        c = c_ref[0, hh:hh + 1, :]
        for gg in range(1, N_HEADS // heads):
            c = jnp.where(pl.program_id(1) == gg, c_ref[0, gg * heads + hh:gg * heads + hh + 1, :], c)
        bias = (c[:, ii * tq:ii * tq + 1] - c[:, :length]) * LOG2E
        return _dot_nt(q, _head(k_ref, hh, slice(0, length))) + bias

    _attend(scores, v_ref, o_ref, n_tiles=n_tiles, per_step=per_step, tq=tq, heads=heads)


def _fox(p3, c, batch, seq, casts):
    t = _tiles()
    tq, heads, per_step = t["attn_tq"], t["attn_heads"], t["fox_tiles_per_step"]
    hw = heads * HEAD_DIM
    groups = N_HEADS // heads
    n_tiles = seq // tq
    (o,), cast = _call_with_casts(
        functools.partial(_fox_kernel, tq=tq, heads=heads, n_tiles=n_tiles, per_step=per_step),
        casts, (p3, p3, p3, c),
        out_shape=(jax.ShapeDtypeStruct((batch, seq, WIDTH), BF16),),
        grid=(batch, groups, n_tiles // per_step),
        in_specs=[
            pl.BlockSpec((1, seq, hw), lambda b, g, i: (b, 0, g)),
            pl.BlockSpec((1, seq, hw), lambda b, g, i: (b, 0, groups + g)),
            pl.BlockSpec((1, seq, hw), lambda b, g, i: (b, 0, 2 * groups + g)),
            pl.BlockSpec((1, N_HEADS, seq), lambda b, g, i: (b, 0, 0)),
        ],
        out_specs=(pl.BlockSpec((1, seq, hw), lambda b, g, i: (b, 0, g)),),
        compiler_params=_params(("arbitrary", "arbitrary", "arbitrary")),
        name="fox_attention",
    )
    return o, cast


def _moba_block_bias(q_raw, km, ii):
    km_hi, km_lo = _split_bf16(km)
    gate = _dot_nt(km_hi, q_raw) + _dot_nt(km_lo, q_raw)
    block = lax.broadcasted_iota(jnp.int32, gate.shape, 0)
    g = jnp.where(block < ii, gate, -jnp.inf)
    rank = jnp.zeros(gate.shape, jnp.int32)
    for n in range(ii):
        gn = g[n:n + 1, :]
        ahead = jnp.logical_or(gn > g, jnp.logical_and(gn == g, n < block))
        rank = rank + ahead.astype(jnp.int32)
    bias_t = jnp.where(rank < MOBA_TOPK, 0.0, MASK_BIAS)
    pad = jnp.zeros((LANES - bias_t.shape[0], bias_t.shape[1]), F32)
    return jnp.concatenate([bias_t, pad], axis=0).T


def _moba_kernel(q_ref, k_ref, v_ref, o_ref, km_scr, *, n_blocks, heads, per_step):
    blk = MOBA_BLOCK

    @pl.when(pl.program_id(2) == 0)
    def _():
        for hh in range(heads):
            for n in range(n_blocks):
                kb = _head(k_ref, hh, _tile_rows(n, blk)).astype(F32)
                km_scr[hh, n:n + 1, :] = jnp.mean(kb, axis=0, keepdims=True)

    def scores(ii, hh):
        length = (ii + 1) * blk
        q_raw = _head(q_ref, hh, _tile_rows(ii, blk))
        q = (q_raw.astype(F32) * (QK_SCALE * LOG2E)).astype(BF16)
        s = _dot_nt(q, _head(k_ref, hh, slice(0, length)))
        if ii <= MOBA_TOPK:
            return s
        bias = _moba_block_bias(q_raw, km_scr[hh], ii)
        past = [s[:, _tile_rows(j, blk)] + bias[:, j:j + 1] for j in range(ii)]
        return jnp.concatenate(past + [s[:, _tile_rows(ii, blk)]], axis=1)

    _attend(scores, v_ref, o_ref, n_tiles=n_blocks, per_step=per_step, tq=blk, heads=heads)


def _moba(qk3, p3, batch, seq, casts):
    n_blocks = seq // MOBA_BLOCK
    assert seq % MOBA_BLOCK == 0 and n_blocks <= LANES
    t = _tiles()
    heads, per_step = t["attn_heads"], t["attn_tiles_per_step"]
    hw = heads * HEAD_DIM
    groups = N_HEADS // heads
    (o,), cast = _call_with_casts(
        functools.partial(_moba_kernel, n_blocks=n_blocks, heads=heads, per_step=per_step), casts,
        (qk3, qk3, p3),
        out_shape=(jax.ShapeDtypeStruct((batch, seq, WIDTH), BF16),),
        grid=(batch, groups, n_blocks // per_step),
        in_specs=[
            pl.BlockSpec((1, seq, hw), lambda b, g, i: (b, 0, g)),
            pl.BlockSpec((1, seq, hw), lambda b, g, i: (b, 0, groups + g)),
            pl.BlockSpec((1, seq, hw), lambda b, g, i: (b, 0, 3 * groups + g)),
        ],
        out_specs=(pl.BlockSpec((1, seq, hw), lambda b, g, i: (b, 0, g)),),
        scratch_shapes=[pltpu.VMEM((heads, n_blocks, HEAD_DIM), F32)],
        compiler_params=_params(("arbitrary", "arbitrary", "arbitrary")),
        name="moba_attention",
    )
    return o, cast


def _mix_resid_kernel(oa_ref, ob_ref, ga_ref, gb_ref, x_ref, wa_ref, wb_ref, wo_ref, g_ref,
                      x1_ref, h_ref):
    ya = jnp.dot(oa_ref[...], wa_ref[...], preferred_element_type=F32)
    yb = jnp.dot(ob_ref[...], wb_ref[...], preferred_element_type=F32)
    ga = jax.nn.sigmoid(ga_ref[...].astype(F32))
    gb = jax.nn.sigmoid(gb_ref[...].astype(F32))
    y = (ga * ya + gb * yb).astype(BF16)
    x1 = x_ref[...] + jnp.dot(y, wo_ref[...], preferred_element_type=F32)
    x1_ref[...] = x1
    h_ref[...] = _rms(x1, g_ref[...]).astype(BF16)


def _mix_resid(oa, ob, p2, x2, wa, wb, wo, g):
    tm = _tiles()["mix_tm"]
    rows, kdim = oa.shape
    d = wa.shape[1]
    ga_blk = (4 * WIDTH) // d
    resident = pl.Buffered(1)
    return pl.pallas_call(
        _mix_resid_kernel,
        out_shape=(jax.ShapeDtypeStruct((rows, d), F32),
                   jax.ShapeDtypeStruct((rows, d), BF16)),
        grid=(rows // tm,),
        in_specs=[
            pl.BlockSpec((tm, kdim), lambda i: (i, 0)),
            pl.BlockSpec((tm, kdim), lambda i: (i, 0)),
            pl.BlockSpec((tm, d), lambda i: (i, ga_blk)),
            pl.BlockSpec((tm, d), lambda i: (i, ga_blk + 1)),
            pl.BlockSpec((tm, d), lambda i: (i, 0)),
            pl.BlockSpec((kdim, d), lambda i: (0, 0), pipeline_mode=resident),
            pl.BlockSpec((kdim, d), lambda i: (0, 0), pipeline_mode=resident),
            pl.BlockSpec((d, d), lambda i: (0, 0), pipeline_mode=resident),
            pl.BlockSpec((1, d), lambda i: (0, 0)),
        ],
        out_specs=(pl.BlockSpec((tm, d), lambda i: (i, 0)),
                   pl.BlockSpec((tm, d), lambda i: (i, 0))),
        compiler_params=_params(("parallel",)),
        name="mix_residual",
    )(oa, ob, p2, p2, x2, wa, wb, wo, g)


def _ffn_kernel(h_ref, x1_ref, wg_ref, wu_ref, wd_ref, gf_ref, o_ref):
    k = pl.program_id(1)

    def delta():
        h = h_ref[...]
        gate = jnp.dot(h, wg_ref[...], preferred_element_type=F32)
        up = jnp.dot(h, wu_ref[...], preferred_element_type=F32)
        a = (gate * jax.nn.sigmoid(gate) * up).astype(BF16)
        return jnp.dot(a, wd_ref[...], preferred_element_type=F32)

    @pl.when(k == 0)
    def _():
        o_ref[...] = x1_ref[...] + delta()

    last = pl.num_programs(1) - 1

    @pl.when(jnp.logical_and(k > 0, k < last))
    def _():
        o_ref[...] += delta()

    @pl.when(k == last)
    def _():
        o_ref[...] = _rms(o_ref[...] + delta(), gf_ref[...])


def _ffn(h2, x1, wg, wu, wd, gf):
    t = _tiles()
    tm, tf = t["ffn_tm"], t["ffn_tf"]
    rows, d = h2.shape
    dff = wd.shape[0]
    assert dff // tf >= 2
    return pl.pallas_call(
        _ffn_kernel,
        out_shape=jax.ShapeDtypeStruct((rows, d), F32),
        grid=(rows // tm, dff // tf),
        in_specs=[
            pl.BlockSpec((tm, d), lambda i, k: (i, 0)),
            pl.BlockSpec((tm, d), lambda i, k: (i, 0)),
            pl.BlockSpec((d, tf), lambda i, k: (0, k)),
            pl.BlockSpec((d, tf), lambda i, k: (0, k)),
            pl.BlockSpec((tf, d), lambda i, k: (k, 0)),
            pl.BlockSpec((1, d), lambda i, k: (0, 0)),
        ],
        out_specs=pl.BlockSpec((tm, d), lambda i, k: (i, 0)),
        compiler_params=_params(("parallel", "arbitrary")),
        name="swiglu_ffn",
    )(h2, x1, wg, wu, wd, gf)


def _rope_tables(seq):
    half = HEAD_DIM // 2
    inv_freq = ROPE_THETA ** (-np.arange(half, dtype=np.float64) / half)
    ang = np.arange(seq, dtype=np.float64)[:, None] * inv_freq[None, :]
    cos, sin = np.cos(ang), np.sin(ang)
    return (np.concatenate([cos, cos], axis=-1).astype(np.float32),
            np.concatenate([-sin, sin], axis=-1).astype(np.float32))


def kernel(x, norm_mix, w_in, b_forget, w_o_fox, w_o_moba, w_out, norm_ffn, w_gate_up, w_down,
           norm_final):
    batch, seq, d = x.shape
    assert w_in.shape[0] == 1
    cs, sn = _rope_tables(seq)
    x2 = x.reshape(batch * seq, d)
    wt = w_in[0].T
    b_pad = jnp.pad(b_forget[0][None, :], ((0, 0), (0, LANES - N_HEADS)))

    dff = w_down.shape[1]
    h, c = _norm(x2, norm_mix[0][None, :], wt, b_pad, batch, seq)
    p2, (wo_a, wo_b, w_res) = _proj_plain(
        h, wt, [_Cast(w_o_fox[0], d), _Cast(w_o_moba[0], d), _Cast(w_out[0], d)])
    qk, _ = _proj_rot(h, wt, cs, sn, seq, [])
    p3 = p2.reshape(batch, seq, p2.shape[1])
    o_a, (wg,) = _fox(p3, c, batch, seq, [_Cast(w_gate_up[0], dff, 0)])
    o_b, (wu, wd) = _moba(qk.reshape(batch, seq, qk.shape[1]), p3, batch, seq,
                          [_Cast(w_gate_up[0], dff, 1), _Cast(w_down[0], d)])
    x1, h2 = _mix_resid(o_a.reshape(batch * seq, WIDTH), o_b.reshape(batch * seq, WIDTH), p2, x2,
                        wo_a, wo_b, w_res, norm_ffn[0][None, :])
    out = _ffn(h2, x1, wg, wu, wd, norm_final[None, :])
    return out.reshape(batch, seq, d)
```

```python
import functools
import math
from typing import NamedTuple

import jax
import jax.numpy as jnp
import numpy as np
from jax import lax
from jax.experimental import pallas as pl
from jax.experimental.pallas import tpu as pltpu

F32 = jnp.float32
BF16 = jnp.bfloat16

HEAD_DIM = 128
N_HEADS = 8
WIDTH = N_HEADS * HEAD_DIM
MOBA_BLOCK = 256
MOBA_TOPK = 3
ROPE_THETA = 10000.0
RMS_EPS = 1e-6
LOG2E = math.log2(math.e)
QK_SCALE = HEAD_DIM ** -0.5
LANES = 128
BF16_SUBLANES = 16
MASK_BIAS = -1e30
SCORE_GROUP = 16
VMEM_LIMIT = 60 * 1024 * 1024


def _tiles():
    return dict(
        norm_tm=1024, proj_tm=2048, rot_tm=2048, proj_tn=1024,
        attn_tq=256, attn_heads=4, attn_tiles_per_step=4, fox_tiles_per_step=8,
        mix_tm=512,
        up_tm=2048, up_tf=512, down_tm=1024, down_tf=1408,
        cum_chunk=256,
    )


def _params(sem):
    return pltpu.CompilerParams(dimension_semantics=sem, vmem_limit_bytes=VMEM_LIMIT)


def _rms(x, g):
    ms = jnp.mean(x * x, axis=-1, keepdims=True)
    return x * lax.rsqrt(ms + RMS_EPS) * g


def _dot_nt(a, b):
    return lax.dot_general(a, b, (((1,), (1,)), ((), ())), preferred_element_type=F32)


class _Cast(NamedTuple):
    src: jax.Array
    cols: int
    col_block: int = 0


def _call_with_casts(kernel_fn, casts, args, *, grid, in_specs, out_specs, out_shape, **kw):
    steps = math.prod(grid)
    strides = [math.prod(grid[a + 1:]) for a in range(len(grid))]

    def step_of(*g):
        return sum(gi * st for gi, st in zip(g, strides))

    n_in, n_out, n_side = len(in_specs), len(out_shape), len(casts)
    side_in, side_out, side_shape = [], [], []
    for c in casts:
        rows = c.src.shape[0] // steps
        assert rows * steps == c.src.shape[0] and rows % BF16_SUBLANES == 0
        side_in.append(pl.BlockSpec((rows, c.cols), lambda *g, c=c: (step_of(*g), c.col_block)))
        side_out.append(pl.BlockSpec((rows, c.cols), lambda *g: (step_of(*g), 0)))
        side_shape.append(jax.ShapeDtypeStruct((c.src.shape[0], c.cols), BF16))

    def body(*refs):
        src = refs[n_in:n_in + n_side]
        dst = refs[n_in + n_side + n_out:n_in + 2 * n_side + n_out]
        for s_ref, d_ref in zip(src, dst):
            d_ref[...] = s_ref[...].astype(BF16)
        kernel_fn(*refs[:n_in], *refs[n_in + n_side:n_in + n_side + n_out],
                  *refs[n_in + 2 * n_side + n_out:])

    res = pl.pallas_call(
        body, grid=grid, in_specs=[*in_specs, *side_in], out_specs=(*out_specs, *side_out),
        out_shape=(*out_shape, *side_shape), **kw)(*args, *[c.src for c in casts])
    return res[:n_out], res[n_out:]


def _split_bf16(v):
    hi = v.astype(BF16)
    return hi, (v - hi.astype(F32)).astype(BF16)


def _norm_kernel(x_ref, g_ref, wf_ref, b_ref, tri_ref, h_ref, c_ref, carry_scr, *, chunk,
                 tiles_per_seq):
    h = _rms(x_ref[...], g_ref[...]).astype(BF16)
    h_ref[...] = h
    z = _dot_nt(h, wf_ref[...].astype(BF16)) + b_ref[...]
    lf = jnp.minimum(z, 0.0) - jnp.log1p(jnp.exp(-jnp.abs(z)))
    lft = lf.T[:N_HEADS, :]
    tri = tri_ref[...]

    @pl.when(pl.program_id(0) % tiles_per_seq == 0)
    def _():
        carry_scr[...] = jnp.zeros_like(carry_scr)

    carry = carry_scr[:, 0:1]
    for n in range(lft.shape[1] // chunk):
        xc = lft[:, n * chunk:(n + 1) * chunk]
        hi = xc.astype(BF16)
        r1 = xc - hi.astype(F32)
        mid = r1.astype(BF16)
        lo = (r1 - mid.astype(F32)).astype(BF16)
        loc = (jnp.dot(hi, tri, preferred_element_type=F32)
               + jnp.dot(mid, tri, preferred_element_type=F32)
               + jnp.dot(lo, tri, preferred_element_type=F32)) + carry
        c_ref[0, :, n * chunk:(n + 1) * chunk] = loc
        carry = loc[:, chunk - 1:chunk]
    carry_scr[...] = jnp.broadcast_to(carry, carry_scr.shape)


def _norm(x2, g, wt, b_pad, batch, seq):
    t = _tiles()
    tm, chunk = t["norm_tm"], t["cum_chunk"]
    rows, d = x2.shape
    f_blk = (6 * WIDTH) // LANES
    tiles_per_seq = seq // tm
    tri = np.triu(np.ones((chunk, chunk), np.float32)).astype(BF16)
    return pl.pallas_call(
        functools.partial(_norm_kernel, chunk=chunk, tiles_per_seq=tiles_per_seq),
        out_shape=(jax.ShapeDtypeStruct((rows, d), BF16),
                   jax.ShapeDtypeStruct((batch, N_HEADS, seq), F32)),
        grid=(rows // tm,),
        in_specs=[
            pl.BlockSpec((tm, d), lambda i: (i, 0)),
            pl.BlockSpec((1, d), lambda i: (0, 0)),
            pl.BlockSpec((LANES, d), lambda i: (f_blk, 0)),
            pl.BlockSpec((1, LANES), lambda i: (0, 0)),
            pl.BlockSpec((chunk, chunk), lambda i: (0, 0)),
        ],
        out_specs=(pl.BlockSpec((tm, d), lambda i: (i, 0)),
                   pl.BlockSpec((1, N_HEADS, tm), lambda i: (i // tiles_per_seq, 0, i % tiles_per_seq))),
        scratch_shapes=[pltpu.VMEM((N_HEADS, LANES), F32)],
        compiler_params=_params(("arbitrary",)),
        name="pre_norm",
    )(x2, g, wt, b_pad, tri)


def _cast_weight_tile(w_ref, w2_ref, w_scr, shift):
    if shift == 0:
        w_scr[...] = w_ref[...].astype(BF16)
    else:
        w_scr[...] = jnp.concatenate([w_ref[shift:, :], w2_ref[...]], axis=0).astype(BF16)


def _proj_kernel(h_ref, w_ref, w2_ref, o_ref, w_scr, *, n_aligned):
    j = pl.program_id(0)

    @pl.when(jnp.logical_and(pl.program_id(1) == 0, j < n_aligned))
    def _():
        _cast_weight_tile(w_ref, w2_ref, w_scr, 0)

    @pl.when(jnp.logical_and(pl.program_id(1) == 0, j >= n_aligned))
    def _():
        _cast_weight_tile(w_ref, w2_ref, w_scr, N_HEADS)

    o_ref[...] = _dot_nt(h_ref[...], w_scr[...]).astype(BF16)


def _proj_rot_kernel(h_ref, w_ref, cs_ref, sn_ref, o_ref, w_scr):
    @pl.when(pl.program_id(1) == 0)
    def _():
        _cast_weight_tile(w_ref, None, w_scr, 0)

    acc = _dot_nt(h_ref[...], w_scr[...])
    cs = cs_ref[...]
    sn = sn_ref[...]
    for hh in range(acc.shape[1] // HEAD_DIM):
        t = acc[:, hh * HEAD_DIM:(hh + 1) * HEAD_DIM]
        r = pltpu.roll(t, HEAD_DIM // 2, axis=1)
        o_ref[:, hh * HEAD_DIM:(hh + 1) * HEAD_DIM] = (t * cs + r * sn).astype(BF16)


def _proj_plain(h, wt, casts):
    t = _tiles()
    tm, tn = t["proj_tm"], t["proj_tn"]
    rows, d = h.shape
    per_group = WIDTH // tn
    n_aligned = 4 * per_group
    n_gate = (2 * d) // tn
    gate0 = (6 * WIDTH) // tn

    def w_blk(j, i):
        aligned = jnp.where(j < 3 * per_group, j, j + 2 * per_group)
        return jnp.where(j < n_aligned, aligned, gate0 + (j - n_aligned)), 0

    def w2_blk(j, i):
        nxt = jnp.maximum(j - n_aligned, 0) + 1
        return (gate0 + nxt) * (tn // N_HEADS), 0

    (p2,), cast = _call_with_casts(
        functools.partial(_proj_kernel, n_aligned=n_aligned), casts, (h, wt, wt),
        out_shape=(jax.ShapeDtypeStruct((rows, (n_aligned + n_gate) * tn), BF16),),
        grid=(n_aligned + n_gate, rows // tm),
        in_specs=[
            pl.BlockSpec((tm, d), lambda j, i: (i, 0)),
            pl.BlockSpec((tn, d), w_blk),
            pl.BlockSpec((N_HEADS, d), w2_blk),
        ],
        out_specs=(pl.BlockSpec((tm, tn), lambda j, i: (i, j)),),
        scratch_shapes=[pltpu.VMEM((tn, d), BF16)],
        compiler_params=_params(("arbitrary", "arbitrary")),
        name="proj_plain",
    )
    return p2, cast


def _proj_rot(h, wt, cs, sn, seq, casts):
    t = _tiles()
    tm, tn = t["rot_tm"], t["proj_tn"]
    rows, d = h.shape
    first = (3 * WIDTH) // tn
    pos_blocks = seq // tm
    (qk,), cast = _call_with_casts(
        _proj_rot_kernel, casts, (h, wt, cs, sn),
        out_shape=(jax.ShapeDtypeStruct((rows, 2 * WIDTH), BF16),),
        grid=((2 * WIDTH) // tn, rows // tm),
        in_specs=[
            pl.BlockSpec((tm, d), lambda j, i: (i, 0)),
            pl.BlockSpec((tn, d), lambda j, i: (first + j, 0)),
            pl.BlockSpec((tm, HEAD_DIM), lambda j, i: (i % pos_blocks, 0)),
            pl.BlockSpec((tm, HEAD_DIM), lambda j, i: (i % pos_blocks, 0)),
        ],
        out_specs=(pl.BlockSpec((tm, tn), lambda j, i: (i, j)),),
        scratch_shapes=[pltpu.VMEM((tn, d), BF16)],
        compiler_params=_params(("arbitrary", "arbitrary")),
        name="proj_rotary",
    )
    return qk, cast


def _head(ref, hh, rows=slice(None)):
    return ref[0, rows, hh * HEAD_DIM:(hh + 1) * HEAD_DIM]


def _causal_softmax_pv(s, v, tq):
    length = s.shape[1]
    row = lax.broadcasted_iota(jnp.int32, (tq, tq), 0)
    col = lax.broadcasted_iota(jnp.int32, (tq, tq), 1)
    own = jnp.where(col <= row, s[:, length - tq:], -jnp.inf)
    s = own if length == tq else jnp.concatenate([s[:, :length - tq], own], axis=1)
    m = jnp.max(s, axis=-1, keepdims=True)
    p = jnp.exp2(s - m).astype(BF16)
    v1 = jnp.concatenate([v, jnp.ones(v.shape, v.dtype)], axis=1)
    acc = jnp.dot(p, v1, preferred_element_type=F32)
    return (acc[:, :HEAD_DIM] / acc[:, HEAD_DIM:]).astype(BF16)


def _tile_rows(ii, tq):
    return slice(ii * tq, (ii + 1) * tq)


def _attend(scores, v_ref, o_ref, *, n_tiles, per_step, tq, heads):
    def run(tiles):
        jobs = [(ii, hh) for ii in sorted(tiles) for hh in range(heads)]
        for j0 in range(0, len(jobs), SCORE_GROUP):
            group = jobs[j0:j0 + SCORE_GROUP]
            ss = [scores(ii, hh) for ii, hh in group]
            for (ii, hh), s in zip(group, ss):
                o_ref[0, _tile_rows(ii, tq), hh * HEAD_DIM:(hh + 1) * HEAD_DIM] = (
                    _causal_softmax_pv(s, _head(v_ref, hh, slice(0, (ii + 1) * tq)), tq))

    n_steps = n_tiles // per_step
    assert per_step % 2 == 0 and n_steps * per_step == n_tiles
    step = pl.program_id(2)
    for pp in range(n_steps):
        lows = [pp + j * n_steps for j in range(per_step // 2)]
        tiles = [t for lo in lows for t in (lo, n_tiles - 1 - lo)]
        if n_steps == 1:
            run(tiles)
        else:
            pl.when(step == pp)(functools.partial(run, tiles))


def _fox_kernel(q_ref, k_ref, v_ref, c_ref, o_ref, *, tq, heads, n_tiles, per_step):
    def scores(ii, hh):
        length = (ii + 1) * tq
        q = (_head(q_ref, hh, _tile_rows(ii, tq)).astype(F32) * (QK_SCALE * LOG2E)).astype(BF16)
        c = c_ref[0, hh:hh + 1, :]
        for gg in range(1, N_HEADS // heads):
            c = jnp.where(pl.program_id(1) == gg, c_ref[0, gg * heads + hh:gg * heads + hh + 1, :], c)
        bias = (c[:, ii * tq:ii * tq + 1] - c[:, :length]) * LOG2E
        return _dot_nt(q, _head(k_ref, hh, slice(0, length))) + bias

    _attend(scores, v_ref, o_ref, n_tiles=n_tiles, per_step=per_step, tq=tq, heads=heads)


def _fox(p3, c, batch, seq, casts):
    t = _tiles()
    tq, heads, per_step = t["attn_tq"], t["attn_heads"], t["fox_tiles_per_step"]
    hw = heads * HEAD_DIM
    groups = N_HEADS // heads
    n_tiles = seq // tq
    (o,), cast = _call_with_casts(
        functools.partial(_fox_kernel, tq=tq, heads=heads, n_tiles=n_tiles, per_step=per_step),
        casts, (p3, p3, p3, c),
        out_shape=(jax.ShapeDtypeStruct((batch, seq, WIDTH), BF16),),
        grid=(batch, groups, n_tiles // per_step),
        in_specs=[
            pl.BlockSpec((1, seq, hw), lambda b, g, i: (b, 0, g)),
            pl.BlockSpec((1, seq, hw), lambda b, g, i: (b, 0, groups + g)),
            pl.BlockSpec((1, seq, hw), lambda b, g, i: (b, 0, 2 * groups + g)),
            pl.BlockSpec((1, N_HEADS, seq), lambda b, g, i: (b, 0, 0)),
        ],
        out_specs=(pl.BlockSpec((1, seq, hw), lambda b, g, i: (b, 0, g)),),
        compiler_params=_params(("arbitrary", "arbitrary", "arbitrary")),
        name="fox_attention",
    )
    return o, cast


def _moba_block_bias(q_raw, km, ii):
    km_hi, km_lo = _split_bf16(km)
    gate = _dot_nt(km_hi, q_raw) + _dot_nt(km_lo, q_raw)
    block = lax.broadcasted_iota(jnp.int32, gate.shape, 0)
    g = jnp.where(block < ii, gate, -jnp.inf)
    rank = jnp.zeros(gate.shape, jnp.int32)
    for n in range(ii):
        gn = g[n:n + 1, :]
        ahead = jnp.logical_or(gn > g, jnp.logical_and(gn == g, n < block))
        rank = rank + ahead.astype(jnp.int32)
    bias_t = jnp.where(rank < MOBA_TOPK, 0.0, MASK_BIAS)
    pad = jnp.zeros((LANES - bias_t.shape[0], bias_t.shape[1]), F32)
    return jnp.concatenate([bias_t, pad], axis=0).T


def _moba_kernel(q_ref, k_ref, v_ref, o_ref, km_scr, *, n_blocks, heads, per_step):
    blk = MOBA_BLOCK

    @pl.when(pl.program_id(2) == 0)
    def _():
        for hh in range(heads):
            for n in range(n_blocks):
                kb = _head(k_ref, hh, _tile_rows(n, blk)).astype(F32)
                km_scr[hh, n:n + 1, :] = jnp.mean(kb, axis=0, keepdims=True)

    def scores(ii, hh):
        length = (ii + 1) * blk
        q_raw = _head(q_ref, hh, _tile_rows(ii, blk))
        q = (q_raw.astype(F32) * (QK_SCALE * LOG2E)).astype(BF16)
        s = _dot_nt(q, _head(k_ref, hh, slice(0, length)))
        if ii <= MOBA_TOPK:
            return s
        bias = _moba_block_bias(q_raw, km_scr[hh], ii)
        past = [s[:, _tile_rows(j, blk)] + bias[:, j:j + 1] for j in range(ii)]
        return jnp.concatenate(past + [s[:, _tile_rows(ii, blk)]], axis=1)

    _attend(scores, v_ref, o_ref, n_tiles=n_blocks, per_step=per_step, tq=blk, heads=heads)


def _moba(qk3, p3, batch, seq, casts):
    n_blocks = seq // MOBA_BLOCK
    assert seq % MOBA_BLOCK == 0 and n_blocks <= LANES
    t = _tiles()
    heads, per_step = t["attn_heads"], t["attn_tiles_per_step"]
    hw = heads * HEAD_DIM
    groups = N_HEADS // heads
    (o,), cast = _call_with_casts(
        functools.partial(_moba_kernel, n_blocks=n_blocks, heads=heads, per_step=per_step), casts,
        (qk3, qk3, p3),
        out_shape=(jax.ShapeDtypeStruct((batch, seq, WIDTH), BF16),),
        grid=(batch, groups, n_blocks // per_step),
        in_specs=[
            pl.BlockSpec((1, seq, hw), lambda b, g, i: (b, 0, g)),
            pl.BlockSpec((1, seq, hw), lambda b, g, i: (b, 0, groups + g)),
            pl.BlockSpec((1, seq, hw), lambda b, g, i: (b, 0, 3 * groups + g)),
        ],
        out_specs=(pl.BlockSpec((1, seq, hw), lambda b, g, i: (b, 0, g)),),
        scratch_shapes=[pltpu.VMEM((heads, n_blocks, HEAD_DIM), F32)],
        compiler_params=_params(("arbitrary", "arbitrary", "arbitrary")),
        name="moba_attention",
    )
    return o, cast


def _mix_resid_kernel(oa_ref, ob_ref, ga_ref, gb_ref, x_ref, wa_ref, wb_ref, wo_ref, g_ref,
                      x1_ref, h_ref):
    ya = jnp.dot(oa_ref[...], wa_ref[...], preferred_element_type=F32)
    yb = jnp.dot(ob_ref[...], wb_ref[...], preferred_element_type=F32)
    ga = jax.nn.sigmoid(ga_ref[...].astype(F32))
    gb = jax.nn.sigmoid(gb_ref[...].astype(F32))
    y = (ga * ya + gb * yb).astype(BF16)
    x1 = x_ref[...] + jnp.dot(y, wo_ref[...], preferred_element_type=F32)
    x1_ref[...] = x1
    h_ref[...] = _rms(x1, g_ref[...]).astype(BF16)


def _mix_resid(oa, ob, p2, x2, wa, wb, wo, g):
    tm = _tiles()["mix_tm"]
    rows, kdim = oa.shape
    d = wa.shape[1]
    ga_blk = (4 * WIDTH) // d
    resident = pl.Buffered(1)
    return pl.pallas_call(
        _mix_resid_kernel,
        out_shape=(jax.ShapeDtypeStruct((rows, d), F32),
                   jax.ShapeDtypeStruct((rows, d), BF16)),
        grid=(rows // tm,),
        in_specs=[
            pl.BlockSpec((tm, kdim), lambda i: (i, 0)),
            pl.BlockSpec((tm, kdim), lambda i: (i, 0)),
            pl.BlockSpec((tm, d), lambda i: (i, ga_blk)),
            pl.BlockSpec((tm, d), lambda i: (i, ga_blk + 1)),
            pl.BlockSpec((tm, d), lambda i: (i, 0)),
            pl.BlockSpec((kdim, d), lambda i: (0, 0), pipeline_mode=resident),
            pl.BlockSpec((kdim, d), lambda i: (0, 0), pipeline_mode=resident),
            pl.BlockSpec((d, d), lambda i: (0, 0), pipeline_mode=resident),
            pl.BlockSpec((1, d), lambda i: (0, 0)),
        ],
        out_specs=(pl.BlockSpec((tm, d), lambda i: (i, 0)),
                   pl.BlockSpec((tm, d), lambda i: (i, 0))),
        compiler_params=_params(("parallel",)),
        name="mix_residual",
    )(oa, ob, p2, p2, x2, wa, wb, wo, g)


def _ffn_up_kernel(h_ref, wg_ref, wu_ref, u_ref):
    h = h_ref[...]
    gate = jnp.dot(h, wg_ref[...], preferred_element_type=F32)
    up = jnp.dot(h, wu_ref[...], preferred_element_type=F32)
    u_ref[...] = (gate * jax.nn.sigmoid(gate) * up).astype(BF16)


def _ffn_up(h2, wg, wu):
    t = _tiles()
    tm, tf = t["up_tm"], t["up_tf"]
    rows, d = h2.shape
    dff = wg.shape[1]
    return pl.pallas_call(
        _ffn_up_kernel,
        out_shape=jax.ShapeDtypeStruct((rows, dff), BF16),
        grid=(dff // tf, rows // tm),
        in_specs=[
            pl.BlockSpec((tm, d), lambda k, i: (i, 0)),
            pl.BlockSpec((d, tf), lambda k, i: (0, k)),
            pl.BlockSpec((d, tf), lambda k, i: (0, k)),
        ],
        out_specs=pl.BlockSpec((tm, tf), lambda k, i: (i, k)),
        compiler_params=_params(("parallel", "parallel")),
        name="swiglu_up",
    )(h2, wg, wu)


def _ffn_kernel(u_ref, x1_ref, wd_ref, gf_ref, o_ref):
    k = pl.program_id(1)

    def delta():
        return jnp.dot(u_ref[...], wd_ref[...], preferred_element_type=F32)

    @pl.when(k == 0)
    def _():
        o_ref[...] = x1_ref[...] + delta()

    last = pl.num_programs(1) - 1

    @pl.when(jnp.logical_and(k > 0, k < last))
    def _():
        o_ref[...] += delta()

    @pl.when(k == last)
    def _():
        o_ref[...] = _rms(o_ref[...] + delta(), gf_ref[...])


def _ffn_down(u, x1, wd, gf):
    t = _tiles()
    tm, tf = t["down_tm"], t["down_tf"]
    rows, d = x1.shape
    dff = wd.shape[0]
    assert dff // tf >= 2
    return pl.pallas_call(
        _ffn_kernel,
        out_shape=jax.ShapeDtypeStruct((rows, d), F32),
        grid=(rows // tm, dff // tf),
        in_specs=[
            pl.BlockSpec((tm, tf), lambda i, k: (i, k)),
            pl.BlockSpec((tm, d), lambda i, k: (i, 0)),
            pl.BlockSpec((tf, d), lambda i, k: (k, 0)),
            pl.BlockSpec((1, d), lambda i, k: (0, 0)),
        ],
        out_specs=pl.BlockSpec((tm, d), lambda i, k: (i, 0)),
        compiler_params=_params(("parallel", "arbitrary")),
        name="swiglu_down",
    )(u, x1, wd, gf)


def _rope_tables(seq):
    half = HEAD_DIM // 2
    inv_freq = ROPE_THETA ** (-np.arange(half, dtype=np.float64) / half)
    ang = np.arange(seq, dtype=np.float64)[:, None] * inv_freq[None, :]
    cos, sin = np.cos(ang), np.sin(ang)
    return (np.concatenate([cos, cos], axis=-1).astype(np.float32),
            np.concatenate([-sin, sin], axis=-1).astype(np.float32))


def kernel(x, norm_mix, w_in, b_forget, w_o_fox, w_o_moba, w_out, norm_ffn, w_gate_up, w_down,
           norm_final):
    batch, seq, d = x.shape
    assert w_in.shape[0] == 1
    cs, sn = _rope_tables(seq)
    x2 = x.reshape(batch * seq, d)
    wt = w_in[0].T
    b_pad = jnp.pad(b_forget[0][None, :], ((0, 0), (0, LANES - N_HEADS)))

    dff = w_down.shape[1]
    h, c = _norm(x2, norm_mix[0][None, :], wt, b_pad, batch, seq)
    p2, (wo_a, wo_b, w_res) = _proj_plain(
        h, wt, [_Cast(w_o_fox[0], d), _Cast(w_o_moba[0], d), _Cast(w_out[0], d)])
    qk, _ = _proj_rot(h, wt, cs, sn, seq, [])
    p3 = p2.reshape(batch, seq, p2.shape[1])
    o_a, (wg,) = _fox(p3, c, batch, seq, [_Cast(w_gate_up[0], dff, 0)])
    o_b, (wu, wd) = _moba(qk.reshape(batch, seq, qk.shape[1]), p3, batch, seq,
                          [_Cast(w_gate_up[0], dff, 1), _Cast(w_down[0], d)])
    x1, h2 = _mix_resid(o_a.reshape(batch * seq, WIDTH), o_b.reshape(batch * seq, WIDTH), p2, x2,
                        wo_a, wo_b, w_res, norm_ffn[0][None, :])
    out = _ffn_down(_ffn_up(h2, wg, wu), x1, wd, norm_final[None, :])
    return out.reshape(batch, seq, d)
```

```python
import functools
import math
from typing import NamedTuple

import jax
import jax.numpy as jnp
import numpy as np
from jax import lax
from jax.experimental import pallas as pl
from jax.experimental.pallas import tpu as pltpu

F32 = jnp.float32
BF16 = jnp.bfloat16

HEAD_DIM = 128
N_HEADS = 8
WIDTH = N_HEADS * HEAD_DIM
MOBA_BLOCK = 256
MOBA_TOPK = 3
ROPE_THETA = 10000.0
RMS_EPS = 1e-6
LOG2E = math.log2(math.e)
QK_SCALE = HEAD_DIM ** -0.5
LANES = 128
BF16_SUBLANES = 16
MASK_BIAS = -1e30
SCORE_GROUP = 16
VMEM_LIMIT = 60 * 1024 * 1024


def _tiles():
    return dict(
        norm_tm=1024, proj_tm=2048, rot_tm=2048, proj_tn=1024,
        attn_tq=256, attn_heads=4, attn_tiles_per_step=4, fox_tiles_per_step=8,
        mix_tm=512,
        ffn_tm=1024, ffn_tf=512,
        cum_chunk=256,
    )


def _params(sem):
    return pltpu.CompilerParams(dimension_semantics=sem, vmem_limit_bytes=VMEM_LIMIT)


def _rms(x, g):
    ms = jnp.mean(x * x, axis=-1, keepdims=True)
    return x * lax.rsqrt(ms + RMS_EPS) * g


def _sigmoid(v):
    return 0.5 * jnp.tanh(0.5 * v) + 0.5


def _dot_nt(a, b):
    return lax.dot_general(a, b, (((1,), (1,)), ((), ())), preferred_element_type=F32)


class _Cast(NamedTuple):
    src: jax.Array
    cols: int
    col_block: int = 0


def _call_with_casts(kernel_fn, casts, args, *, grid, in_specs, out_specs, out_shape, **kw):
    steps = math.prod(grid)
    strides = [math.prod(grid[a + 1:]) for a in range(len(grid))]

    def step_of(*g):
        return sum(gi * st for gi, st in zip(g, strides))

    n_in, n_out, n_side = len(in_specs), len(out_shape), len(casts)
    side_in, side_out, side_shape = [], [], []
    for c in casts:
        rows = c.src.shape[0] // steps
        assert rows * steps == c.src.shape[0] and rows % BF16_SUBLANES == 0
        side_in.append(pl.BlockSpec((rows, c.cols), lambda *g, c=c: (step_of(*g), c.col_block)))
        side_out.append(pl.BlockSpec((rows, c.cols), lambda *g: (step_of(*g), 0)))
        side_shape.append(jax.ShapeDtypeStruct((c.src.shape[0], c.cols), BF16))

    def body(*refs):
        src = refs[n_in:n_in + n_side]
        dst = refs[n_in + n_side + n_out:n_in + 2 * n_side + n_out]
        for s_ref, d_ref in zip(src, dst):
            d_ref[...] = s_ref[...].astype(BF16)
        kernel_fn(*refs[:n_in], *refs[n_in + n_side:n_in + n_side + n_out],
                  *refs[n_in + 2 * n_side + n_out:])

    res = pl.pallas_call(
        body, grid=grid, in_specs=[*in_specs, *side_in], out_specs=(*out_specs, *side_out),
        out_shape=(*out_shape, *side_shape), **kw)(*args, *[c.src for c in casts])
    return res[:n_out], res[n_out:]


def _split_bf16(v):
    hi = v.astype(BF16)
    return hi, (v - hi.astype(F32)).astype(BF16)


def _norm_kernel(x_ref, g_ref, wf_ref, b_ref, tri_ref, h_ref, c_ref, carry_scr, *, chunk,
                 tiles_per_seq):
    h = _rms(x_ref[...], g_ref[...]).astype(BF16)
    h_ref[...] = h
    z = _dot_nt(h, wf_ref[...].astype(BF16)) + b_ref[...]
    lf = jnp.minimum(z, 0.0) - jnp.log1p(jnp.exp(-jnp.abs(z)))
    lft = lf.T[:N_HEADS, :]
    tri = tri_ref[...]

    @pl.when(pl.program_id(0) % tiles_per_seq == 0)
    def _():
        carry_scr[...] = jnp.zeros_like(carry_scr)

    carry = carry_scr[:, 0:1]
    for n in range(lft.shape[1] // chunk):
        xc = lft[:, n * chunk:(n + 1) * chunk]
        hi = xc.astype(BF16)
        r1 = xc - hi.astype(F32)
        mid = r1.astype(BF16)
        lo = (r1 - mid.astype(F32)).astype(BF16)
        loc = (jnp.dot(hi, tri, preferred_element_type=F32)
               + jnp.dot(mid, tri, preferred_element_type=F32)
               + jnp.dot(lo, tri, preferred_element_type=F32)) + carry
        c_ref[0, :, n * chunk:(n + 1) * chunk] = loc
        carry = loc[:, chunk - 1:chunk]
    carry_scr[...] = jnp.broadcast_to(carry, carry_scr.shape)


def _norm(x2, g, wt, b_pad, batch, seq):
    t = _tiles()
    tm, chunk = t["norm_tm"], t["cum_chunk"]
    rows, d = x2.shape
    f_blk = (6 * WIDTH) // LANES
    tiles_per_seq = seq // tm
    tri = np.triu(np.ones((chunk, chunk), np.float32)).astype(BF16)
    return pl.pallas_call(
        functools.partial(_norm_kernel, chunk=chunk, tiles_per_seq=tiles_per_seq),
        out_shape=(jax.ShapeDtypeStruct((rows, d), BF16),
                   jax.ShapeDtypeStruct((batch, N_HEADS, seq), F32)),
        grid=(rows // tm,),
        in_specs=[
            pl.BlockSpec((tm, d), lambda i: (i, 0)),
            pl.BlockSpec((1, d), lambda i: (0, 0)),
            pl.BlockSpec((LANES, d), lambda i: (f_blk, 0)),
            pl.BlockSpec((1, LANES), lambda i: (0, 0)),
            pl.BlockSpec((chunk, chunk), lambda i: (0, 0)),
        ],
        out_specs=(pl.BlockSpec((tm, d), lambda i: (i, 0)),
                   pl.BlockSpec((1, N_HEADS, tm), lambda i: (i // tiles_per_seq, 0, i % tiles_per_seq))),
        scratch_shapes=[pltpu.VMEM((N_HEADS, LANES), F32)],
        compiler_params=_params(("arbitrary",)),
        name="pre_norm",
    )(x2, g, wt, b_pad, tri)


def _cast_weight_tile(w_ref, w2_ref, w_scr, shift):
    if shift == 0:
        w_scr[...] = w_ref[...].astype(BF16)
    else:
        w_scr[...] = jnp.concatenate([w_ref[shift:, :], w2_ref[...]], axis=0).astype(BF16)


def _proj_kernel(h_ref, w_ref, w2_ref, o_ref, w_scr, *, n_aligned):
    j = pl.program_id(0)

    @pl.when(jnp.logical_and(pl.program_id(1) == 0, j < n_aligned))
    def _():
        _cast_weight_tile(w_ref, w2_ref, w_scr, 0)

    @pl.when(jnp.logical_and(pl.program_id(1) == 0, j >= n_aligned))
    def _():
        _cast_weight_tile(w_ref, w2_ref, w_scr, N_HEADS)

    o_ref[...] = _dot_nt(h_ref[...], w_scr[...]).astype(BF16)


def _proj_rot_kernel(h_ref, w_ref, cs_ref, sn_ref, o_ref, w_scr):
    @pl.when(pl.program_id(1) == 0)
    def _():
        _cast_weight_tile(w_ref, None, w_scr, 0)

    acc = _dot_nt(h_ref[...], w_scr[...])
    cs = cs_ref[...]
    sn = sn_ref[...]
    for hh in range(acc.shape[1] // HEAD_DIM):
        t = acc[:, hh * HEAD_DIM:(hh + 1) * HEAD_DIM]
        r = pltpu.roll(t, HEAD_DIM // 2, axis=1)
        o_ref[:, hh * HEAD_DIM:(hh + 1) * HEAD_DIM] = (t * cs + r * sn).astype(BF16)


def _proj_plain(h, wt, casts):
    t = _tiles()
    tm, tn = t["proj_tm"], t["proj_tn"]
    rows, d = h.shape
    per_group = WIDTH // tn
    n_aligned = 4 * per_group
    n_gate = (2 * d) // tn
    gate0 = (6 * WIDTH) // tn

    def w_blk(j, i):
        aligned = jnp.where(j < 3 * per_group, j, j + 2 * per_group)
        return jnp.where(j < n_aligned, aligned, gate0 + (j - n_aligned)), 0

    def w2_blk(j, i):
        nxt = jnp.maximum(j - n_aligned, 0) + 1
        return (gate0 + nxt) * (tn // N_HEADS), 0

    (p2,), cast = _call_with_casts(
        functools.partial(_proj_kernel, n_aligned=n_aligned), casts, (h, wt, wt),
        out_shape=(jax.ShapeDtypeStruct((rows, (n_aligned + n_gate) * tn), BF16),),
        grid=(n_aligned + n_gate, rows // tm),
        in_specs=[
            pl.BlockSpec((tm, d), lambda j, i: (i, 0)),
            pl.BlockSpec((tn, d), w_blk),
            pl.BlockSpec((N_HEADS, d), w2_blk),
        ],
        out_specs=(pl.BlockSpec((tm, tn), lambda j, i: (i, j)),),
        scratch_shapes=[pltpu.VMEM((tn, d), BF16)],
        compiler_params=_params(("arbitrary", "arbitrary")),
        name="proj_plain",
    )
    return p2, cast


def _proj_rot(h, wt, cs, sn, seq, casts):
    t = _tiles()
    tm, tn = t["rot_tm"], t["proj_tn"]
    rows, d = h.shape
    first = (3 * WIDTH) // tn
    pos_blocks = seq // tm
    (qk,), cast = _call_with_casts(
        _proj_rot_kernel, casts, (h, wt, cs, sn),
        out_shape=(jax.ShapeDtypeStruct((rows, 2 * WIDTH), BF16),),
        grid=((2 * WIDTH) // tn, rows // tm),
        in_specs=[
            pl.BlockSpec((tm, d), lambda j, i: (i, 0)),
            pl.BlockSpec((tn, d), lambda j, i: (first + j, 0)),
            pl.BlockSpec((tm, HEAD_DIM), lambda j, i: (i % pos_blocks, 0)),
            pl.BlockSpec((tm, HEAD_DIM), lambda j, i: (i % pos_blocks, 0)),
        ],
        out_specs=(pl.BlockSpec((tm, tn), lambda j, i: (i, j)),),
        scratch_shapes=[pltpu.VMEM((tn, d), BF16)],
        compiler_params=_params(("arbitrary", "arbitrary")),
        name="proj_rotary",
    )
    return qk, cast


def _head(ref, hh, rows=slice(None)):
    return ref[0, rows, hh * HEAD_DIM:(hh + 1) * HEAD_DIM]


def _causal_softmax_pv(s, v, tq):
    length = s.shape[1]
    row = lax.broadcasted_iota(jnp.int32, (tq, tq), 0)
    col = lax.broadcasted_iota(jnp.int32, (tq, tq), 1)
    own = jnp.where(col <= row, s[:, length - tq:], -jnp.inf)
    s = own if length == tq else jnp.concatenate([s[:, :length - tq], own], axis=1)
    m = jnp.max(s, axis=-1, keepdims=True)
    p = jnp.exp2(s - m).astype(BF16)
    v1 = jnp.concatenate([v, jnp.ones(v.shape, v.dtype)], axis=1)
    acc = jnp.dot(p, v1, preferred_element_type=F32)
    return (acc[:, :HEAD_DIM] / acc[:, HEAD_DIM:]).astype(BF16)


def _tile_rows(ii, tq):
    return slice(ii * tq, (ii + 1) * tq)


def _attend(scores, v_ref, o_ref, *, n_tiles, per_step, tq, heads):
    def run(tiles):
        jobs = [(ii, hh) for ii in sorted(tiles) for hh in range(heads)]
        for j0 in range(0, len(jobs), SCORE_GROUP):
            group = jobs[j0:j0 + SCORE_GROUP]
            ss = [scores(ii, hh) for ii, hh in group]
            for (ii, hh), s in zip(group, ss):
                o_ref[0, _tile_rows(ii, tq), hh * HEAD_DIM:(hh + 1) * HEAD_DIM] = (
                    _causal_softmax_pv(s, _head(v_ref, hh, slice(0, (ii + 1) * tq)), tq))

    n_steps = n_tiles // per_step
    assert per_step % 2 == 0 and n_steps * per_step == n_tiles
    step = pl.program_id(2)
    for pp in range(n_steps):
        lows = [pp + j * n_steps for j in range(per_step // 2)]
        tiles = [t for lo in lows for t in (lo, n_tiles - 1 - lo)]
        if n_steps == 1:
            run(tiles)
        else:
            pl.when(step == pp)(functools.partial(run, tiles))


def _fox_kernel(q_ref, k_ref, v_ref, c_ref, o_ref, *, tq, heads, n_tiles, per_step):
    def scores(ii, hh):
        length = (ii + 1) * tq
        q = (_head(q_ref, hh, _tile_rows(ii, tq)).astype(F32) * (QK_SCALE * LOG2E)).astype(BF16)
        c = c_ref[0, hh:hh + 1, :]
        for gg in range(1, N_HEADS // heads):
            c = jnp.where(pl.program_id(1) == gg, c_ref[0, gg * heads + hh:gg * heads + hh + 1, :], c)
        bias = (c[:, ii * tq:ii * tq + 1] - c[:, :length]) * LOG2E
        return _dot_nt(q, _head(k_ref, hh, slice(0, length))) + bias

    _attend(scores, v_ref, o_ref, n_tiles=n_tiles, per_step=per_step, tq=tq, heads=heads)


def _fox(p3, c, batch, seq, casts):
    t = _tiles()
    tq, heads, per_step = t["attn_tq"], t["attn_heads"], t["fox_tiles_per_step"]
    hw = heads * HEAD_DIM
    groups = N_HEADS // heads
    n_tiles = seq // tq
    (o,), cast = _call_with_casts(
        functools.partial(_fox_kernel, tq=tq, heads=heads, n_tiles=n_tiles, per_step=per_step),
        casts, (p3, p3, p3, c),
        out_shape=(jax.ShapeDtypeStruct((batch, seq, WIDTH), BF16),),
        grid=(batch, groups, n_tiles // per_step),
        in_specs=[
            pl.BlockSpec((1, seq, hw), lambda b, g, i: (b, 0, g)),
            pl.BlockSpec((1, seq, hw), lambda b, g, i: (b, 0, groups + g)),
            pl.BlockSpec((1, seq, hw), lambda b, g, i: (b, 0, 2 * groups + g)),
            pl.BlockSpec((1, N_HEADS, seq), lambda b, g, i: (b, 0, 0)),
        ],
        out_specs=(pl.BlockSpec((1, seq, hw), lambda b, g, i: (b, 0, g)),),
        compiler_params=_params(("arbitrary", "arbitrary", "arbitrary")),
        name="fox_attention",
    )
    return o, cast


def _moba_block_bias(q_raw, km, ii):
    km_hi, km_lo = _split_bf16(km)
    gate = _dot_nt(km_hi, q_raw) + _dot_nt(km_lo, q_raw)
    block = lax.broadcasted_iota(jnp.int32, gate.shape, 0)
    g = jnp.where(block < ii, gate, -jnp.inf)
    rank = jnp.zeros(gate.shape, jnp.int32)
    for n in range(ii):
        gn = g[n:n + 1, :]
        ahead = jnp.logical_or(gn > g, jnp.logical_and(gn == g, n < block))
        rank = rank + ahead.astype(jnp.int32)
    bias_t = jnp.where(rank < MOBA_TOPK, 0.0, MASK_BIAS)
    pad = jnp.zeros((LANES - bias_t.shape[0], bias_t.shape[1]), F32)
    return jnp.concatenate([bias_t, pad], axis=0).T


def _moba_kernel(q_ref, k_ref, v_ref, o_ref, km_scr, *, n_blocks, heads, per_step):
    blk = MOBA_BLOCK

    @pl.when(pl.program_id(2) == 0)
    def _():
        for hh in range(heads):
            for n in range(n_blocks):
                kb = _head(k_ref, hh, _tile_rows(n, blk)).astype(F32)
                km_scr[hh, n:n + 1, :] = jnp.mean(kb, axis=0, keepdims=True)

    def scores(ii, hh):
        length = (ii + 1) * blk
        q_raw = _head(q_ref, hh, _tile_rows(ii, blk))
        q = (q_raw.astype(F32) * (QK_SCALE * LOG2E)).astype(BF16)
        s = _dot_nt(q, _head(k_ref, hh, slice(0, length)))
        if ii <= MOBA_TOPK:
            return s
        bias = _moba_block_bias(q_raw, km_scr[hh], ii)
        past = [s[:, _tile_rows(j, blk)] + bias[:, j:j + 1] for j in range(ii)]
        return jnp.concatenate(past + [s[:, _tile_rows(ii, blk)]], axis=1)

    _attend(scores, v_ref, o_ref, n_tiles=n_blocks, per_step=per_step, tq=blk, heads=heads)


def _moba(qk3, p3, batch, seq, casts):
    n_blocks = seq // MOBA_BLOCK
    assert seq % MOBA_BLOCK == 0 and n_blocks <= LANES
    t = _tiles()
    heads, per_step = t["attn_heads"], t["attn_tiles_per_step"]
    hw = heads * HEAD_DIM
    groups = N_HEADS // heads
    (o,), cast = _call_with_casts(
        functools.partial(_moba_kernel, n_blocks=n_blocks, heads=heads, per_step=per_step), casts,
        (qk3, qk3, p3),
        out_shape=(jax.ShapeDtypeStruct((batch, seq, WIDTH), BF16),),
        grid=(batch, groups, n_blocks // per_step),
        in_specs=[
            pl.BlockSpec((1, seq, hw), lambda b, g, i: (b, 0, g)),
            pl.BlockSpec((1, seq, hw), lambda b, g, i: (b, 0, groups + g)),
            pl.BlockSpec((1, seq, hw), lambda b, g, i: (b, 0, 3 * groups + g)),
        ],
        out_specs=(pl.BlockSpec((1, seq, hw), lambda b, g, i: (b, 0, g)),),
        scratch_shapes=[pltpu.VMEM((heads, n_blocks, HEAD_DIM), F32)],
        compiler_params=_params(("arbitrary", "arbitrary", "arbitrary")),
        name="moba_attention",
    )
    return o, cast


def _mix_resid_kernel(oa_ref, ob_ref, ga_ref, gb_ref, x_ref, wa_ref, wb_ref, wo_ref, g_ref,
                      x1_ref, h_ref):
    ya = jnp.dot(oa_ref[...], wa_ref[...], preferred_element_type=F32)
    yb = jnp.dot(ob_ref[...], wb_ref[...], preferred_element_type=F32)
    ga = _sigmoid(ga_ref[...].astype(F32))
    gb = _sigmoid(gb_ref[...].astype(F32))
    y = (ga * ya + gb * yb).astype(BF16)
    x1 = x_ref[...] + jnp.dot(y, wo_ref[...], preferred_element_type=F32)
    x1_ref[...] = x1
    h_ref[...] = _rms(x1, g_ref[...]).astype(BF16)


def _mix_resid(oa, ob, p2, x2, wa, wb, wo, g):
    tm = _tiles()["mix_tm"]
    rows, kdim = oa.shape
    d = wa.shape[1]
    ga_blk = (4 * WIDTH) // d
    resident = pl.Buffered(1)
    return pl.pallas_call(
        _mix_resid_kernel,
        out_shape=(jax.ShapeDtypeStruct((rows, d), F32),
                   jax.ShapeDtypeStruct((rows, d), BF16)),
        grid=(rows // tm,),
        in_specs=[
            pl.BlockSpec((tm, kdim), lambda i: (i, 0)),
            pl.BlockSpec((tm, kdim), lambda i: (i, 0)),
            pl.BlockSpec((tm, d), lambda i: (i, ga_blk)),
            pl.BlockSpec((tm, d), lambda i: (i, ga_blk + 1)),
            pl.BlockSpec((tm, d), lambda i: (i, 0)),
            pl.BlockSpec((kdim, d), lambda i: (0, 0), pipeline_mode=resident),
            pl.BlockSpec((kdim, d), lambda i: (0, 0), pipeline_mode=resident),
            pl.BlockSpec((d, d), lambda i: (0, 0), pipeline_mode=resident),
            pl.BlockSpec((1, d), lambda i: (0, 0)),
        ],
        out_specs=(pl.BlockSpec((tm, d), lambda i: (i, 0)),
                   pl.BlockSpec((tm, d), lambda i: (i, 0))),
        compiler_params=_params(("parallel",)),
        name="mix_residual",
    )(oa, ob, p2, p2, x2, wa, wb, wo, g)


def _ffn_kernel(h_ref, x1_ref, wg_ref, wu_ref, wd_ref, gf_ref, o_ref):
    k = pl.program_id(1)

    def delta():
        h = h_ref[...]
        gate = jnp.dot(h, wg_ref[...], preferred_element_type=F32)
        up = jnp.dot(h, wu_ref[...], preferred_element_type=F32)
        a = (gate * _sigmoid(gate) * up).astype(BF16)
        return jnp.dot(a, wd_ref[...], preferred_element_type=F32)

    @pl.when(k == 0)
    def _():
        o_ref[...] = x1_ref[...] + delta()

    last = pl.num_programs(1) - 1

    @pl.when(jnp.logical_and(k > 0, k < last))
    def _():
        o_ref[...] += delta()

    @pl.when(k == last)
    def _():
        o_ref[...] = _rms(o_ref[...] + delta(), gf_ref[...])


def _ffn(h2, x1, wg, wu, wd, gf):
    t = _tiles()
    tm, tf = t["ffn_tm"], t["ffn_tf"]
    rows, d = h2.shape
    dff = wd.shape[0]
    assert dff // tf >= 2
    return pl.pallas_call(
        _ffn_kernel,
        out_shape=jax.ShapeDtypeStruct((rows, d), F32),
        grid=(rows // tm, dff // tf),
        in_specs=[
            pl.BlockSpec((tm, d), lambda i, k: (i, 0)),
            pl.BlockSpec((tm, d), lambda i, k: (i, 0)),
            pl.BlockSpec((d, tf), lambda i, k: (0, k)),
            pl.BlockSpec((d, tf), lambda i, k: (0, k)),
            pl.BlockSpec((tf, d), lambda i, k: (k, 0)),
            pl.BlockSpec((1, d), lambda i, k: (0, 0)),
        ],
        out_specs=pl.BlockSpec((tm, d), lambda i, k: (i, 0)),
        compiler_params=_params(("parallel", "arbitrary")),
        name="swiglu_ffn",
    )(h2, x1, wg, wu, wd, gf)


def _rope_tables(seq):
    half = HEAD_DIM // 2
    inv_freq = ROPE_THETA ** (-np.arange(half, dtype=np.float64) / half)
    ang = np.arange(seq, dtype=np.float64)[:, None] * inv_freq[None, :]
    cos, sin = np.cos(ang), np.sin(ang)
    return (np.concatenate([cos, cos], axis=-1).astype(np.float32),
            np.concatenate([-sin, sin], axis=-1).astype(np.float32))


def kernel(x, norm_mix, w_in, b_forget, w_o_fox, w_o_moba, w_out, norm_ffn, w_gate_up, w_down,
           norm_final):
    batch, seq, d = x.shape
    assert w_in.shape[0] == 1
    cs, sn = _rope_tables(seq)
    x2 = x.reshape(batch * seq, d)
    wt = w_in[0].T
    b_pad = jnp.pad(b_forget[0][None, :], ((0, 0), (0, LANES - N_HEADS)))

    dff = w_down.shape[1]
    h, c = _norm(x2, norm_mix[0][None, :], wt, b_pad, batch, seq)
    p2, (wo_a, wo_b, w_res) = _proj_plain(
        h, wt, [_Cast(w_o_fox[0], d), _Cast(w_o_moba[0], d), _Cast(w_out[0], d)])
    qk, _ = _proj_rot(h, wt, cs, sn, seq, [])
    p3 = p2.reshape(batch, seq, p2.shape[1])
    o_a, (wg,) = _fox(p3, c, batch, seq, [_Cast(w_gate_up[0], dff, 0)])
    o_b, (wu, wd) = _moba(qk.reshape(batch, seq, qk.shape[1]), p3, batch, seq,
                          [_Cast(w_gate_up[0], dff, 1), _Cast(w_down[0], d)])
    x1, h2 = _mix_resid(o_a.reshape(batch * seq, WIDTH), o_b.reshape(batch * seq, WIDTH), p2, x2,
                        wo_a, wo_b, w_res, norm_ffn[0][None, :])
    out = _ffn(h2, x1, wg, wu, wd, norm_final[None, :])
    return out.reshape(batch, seq, d)
```
